```python
import math
import jax, jax.numpy as jnp
from jax import lax
import numpy as np

D_MODEL = 1024
BATCH = 32
SEQ = 2048
DEPTH = 4

MIX_WIDTH = D_MODEL
A_WIDTH = MIX_WIDTH // 2
B_WIDTH = MIX_WIDTH // 2
C_WIDTH = MIX_WIDTH // 2
D_WIDTH = MIX_WIDTH // 2
CHUNK = 128
A_GROUPS = 4
A_GROUP_DIM = A_WIDTH // A_GROUPS
HYENA_ORDER = 2
HYENA_DIRS = 2
FILTER_BANDS = 16
FILTER_EMB = 1 + 2 * FILTER_BANDS
FILTER_HIDDEN = 64
FILTER_OUT = HYENA_DIRS * HYENA_ORDER * B_WIDTH
DECAY_TARGET = 1e-2
FAST_DECAY_PCT = 0.3
SLOW_DECAY_PCT = 1.5
POOL_WINDOWS = (2, 4, 8, 16)
C_GROUPS = len(POOL_WINDOWS)
C_GROUP_DIM = C_WIDTH // C_GROUPS
SHORT_CONV = 3
D_FF = -(-8 * D_MODEL // (3 * 256)) * 256
N_EVEN = (DEPTH + 1) // 2
N_ODD = DEPTH // 2
AB_IN = 2 * A_WIDTH + 3 * B_WIDTH
CD_IN = C_WIDTH + 3 * D_WIDTH
RMS_EPS = 1e-6
LN_EPS = 1e-5

kernel_name = "hybrid_gmlp_hyena_pool_shortconv_encoder"


def rmsnorm(x, g):
    xf = x.astype(jnp.float32)
    y = xf * lax.rsqrt(jnp.mean(xf * xf, axis=-1, keepdims=True) + RMS_EPS)
    return (y * g.astype(jnp.float32)).astype(x.dtype)


def layernorm(x, g):
    xf = x.astype(jnp.float32)
    mu = jnp.mean(xf, axis=-1, keepdims=True)
    xc = xf - mu
    y = xc * lax.rsqrt(jnp.mean(xc * xc, axis=-1, keepdims=True) + LN_EPS)
    return (y * g.astype(jnp.float32)).astype(x.dtype)


def conv3_centred(x, w):
    xp = jnp.pad(x, ((0, 0), (1, 1), (0, 0)))
    return xp[:, :-2] * w[0] + xp[:, 1:-1] * w[1] + xp[:, 2:] * w[2]


def spatial_gating(u, v, ln_g, w_s, b_s):
    bsz, L, _ = v.shape
    v = layernorm(v, ln_g)
    vc = v.reshape(bsz, L // CHUNK, CHUNK, A_GROUPS, A_GROUP_DIM)
    s = jnp.einsum("gpq,bnqgc->bnpgc", w_s, vc) + b_s.T[:, :, None]
    return u * s.reshape(bsz, L, A_WIDTH)


def hyena_filter_spectra(L, w1, b1, freq, w2, b2, w3, decay):
    t = jnp.arange(L, dtype=jnp.float32)
    t01 = t / max(L - 1, 1)
    w = 2 * math.pi * t / L
    bands = jnp.linspace(1e-4, FILTER_BANDS - 1, FILTER_BANDS, dtype=jnp.float32)
    fw = w[:, None] * bands[None, :]
    z = jnp.concatenate([t01[:, None], jnp.cos(fw), -jnp.sin(fw)], axis=-1)
    f32 = lambda a: a.astype(jnp.float32)
    h = jnp.sin(f32(freq[0]) * (z @ f32(w1) + f32(b1)))
    h = jnp.sin(f32(freq[1]) * (h @ f32(w2) + f32(b2)))
    h = h @ f32(w3)
    h = h * jnp.exp(-t01[:, None] * jnp.abs(f32(decay))[None, :])
    h = h.reshape(L, HYENA_DIRS, HYENA_ORDER, B_WIDTH)
    fwd, bwd = h[:, 0], h[:, 1]
    buf = jnp.concatenate([fwd, jnp.zeros_like(fwd[:1]), bwd[:L - 1][::-1]], axis=0)
    return jnp.fft.rfft(buf, axis=0)


def fft_long_conv(z, h_spec, skip):
    L = z.shape[1]
    zf = z.astype(jnp.float32)
    zs = jnp.fft.rfft(zf, n=2 * L, axis=1)
    y = jnp.fft.irfft(zs * h_spec[None], n=2 * L, axis=1)[:, :L]
    return (y + zf * skip.astype(jnp.float32)).astype(z.dtype)


def hyena_mixer(proj, conv_w, w1, b1, freq, w2, b2, w3, decay, skip):
    L = proj.shape[1]
    pc = conv3_centred(proj, conv_w)
    v = pc[..., :B_WIDTH]
    gates = (pc[..., B_WIDTH:2 * B_WIDTH], pc[..., 2 * B_WIDTH:])
    spec = hyena_filter_spectra(L, w1, b1, freq, w2, b2, w3, decay)
    z = v
    for o in range(HYENA_ORDER):
        z = gates[o] * fft_long_conv(z, spec[:, o], skip[o])
    return z


def multiscale_pool(x, w_grp, scale):
    L = x.shape[1]
    t = jnp.arange(L)
    outs = []
    for g, w in enumerate(POOL_WINDOWS):
        left = w // 2
        right = w - left - 1
        xg = x[..., g * C_GROUP_DIM:(g + 1) * C_GROUP_DIM].astype(jnp.float32)
        xp = jnp.pad(xg, ((0, 0), (left, right), (0, 0)))
        cs = jnp.concatenate([jnp.zeros_like(xp[:, :1]), jnp.cumsum(xp, axis=1)], axis=1)
        s = cs[:, w:] - cs[:, :L]
        cnt = (jnp.minimum(t + right, L - 1) - jnp.maximum(t - left, 0) + 1).astype(jnp.float32)
        pooled = (s / cnt[None, :, None] - xg).astype(x.dtype)
        outs.append(pooled @ w_grp[g])
    return jnp.concatenate(outs, axis=-1) * scale


def short_gated_conv(proj, conv_w):
    b = proj[..., :D_WIDTH]
    c = proj[..., D_WIDTH:2 * D_WIDTH]
    h = proj[..., 2 * D_WIDTH:]
    return b * conv3_centred(c * h, conv_w)


def swiglu(x, w_gu, w_down):
    gu = x @ w_gu
    return (jax.nn.silu(gu[..., :D_FF]) * gu[..., D_FF:]) @ w_down


def setup_inputs(seed: int = 0) -> dict:
    key = jax.random.key(seed)
    ks = jax.random.split(key, 24)

    def nrm(k, shape, scale):
        return jax.random.normal(k, shape, jnp.float32) * scale

    min_decay = math.log(DECAY_TARGET) / SLOW_DECAY_PCT
    max_decay = math.log(DECAY_TARGET) / FAST_DECAY_PCT
    base_decay = jnp.linspace(min_decay, max_decay, FILTER_OUT, dtype=jnp.float32)
    return {
        "x": nrm(ks[0], (BATCH, SEQ, D_MODEL), 1.0),
        "norm_g": 1.0 + nrm(ks[1], (DEPTH, 4, D_MODEL), 0.1),
        "ffn_w_gu": nrm(ks[2], (DEPTH, D_MODEL, 2 * D_FF), D_MODEL ** -0.5),
        "ffn_w_down": nrm(ks[3], (DEPTH, D_FF, D_MODEL), D_FF ** -0.5),
        "ab_w_in": nrm(ks[4], (N_EVEN, D_MODEL, AB_IN), D_MODEL ** -0.5),
        "ab_w_out": nrm(ks[5], (N_EVEN, A_WIDTH + B_WIDTH, D_MODEL), (A_WIDTH + B_WIDTH) ** -0.5),
        "a_ln_g": 1.0 + nrm(ks[6], (N_EVEN, A_WIDTH), 0.1),
        "a_w_s": nrm(ks[7], (N_EVEN, A_GROUPS, CHUNK, CHUNK), CHUNK ** -0.5),
        "a_b_s": 1.0 + nrm(ks[8], (N_EVEN, A_GROUPS, CHUNK), 0.1),
        "b_conv_w": nrm(ks[9], (N_EVEN, SHORT_CONV, 3 * B_WIDTH), SHORT_CONV ** -0.5),
        "b_filt_w1": nrm(ks[10], (N_EVEN, FILTER_EMB, FILTER_HIDDEN), FILTER_EMB ** -0.5),
        "b_filt_b1": nrm(ks[11], (N_EVEN, FILTER_HIDDEN), 0.1),
        "b_filt_freq": 1.0 + nrm(ks[12], (N_EVEN, 2, FILTER_HIDDEN), 0.05),
        "b_filt_w2": nrm(ks[13], (N_EVEN, FILTER_HIDDEN, FILTER_HIDDEN), FILTER_HIDDEN ** -0.5),
        "b_filt_b2": nrm(ks[14], (N_EVEN, FILTER_HIDDEN), 0.1),
        "b_filt_w3": nrm(ks[15], (N_EVEN, FILTER_HIDDEN, FILTER_OUT), 0.02),
        "b_decay": base_decay[None, :] * (1.0 + nrm(ks[16], (N_EVEN, FILTER_OUT), 0.05)),
        "b_skip": nrm(ks[17], (N_EVEN, HYENA_ORDER, B_WIDTH), 1.0),
        "cd_w_in": nrm(ks[18], (N_ODD, D_MODEL, CD_IN), D_MODEL ** -0.5),
        "cd_w_out": nrm(ks[19], (N_ODD, C_WIDTH + D_WIDTH, D_MODEL), (C_WIDTH + D_WIDTH) ** -0.5),
        "c_w": nrm(ks[20], (N_ODD, C_GROUPS, C_GROUP_DIM, C_GROUP_DIM), C_GROUP_DIM ** -0.5),
        "c_scale": 1.0 + nrm(ks[21], (N_ODD, C_WIDTH), 0.1),
        "d_conv_w": nrm(ks[22], (N_ODD, SHORT_CONV, D_WIDTH), SHORT_CONV ** -0.5),
    }


def reference(x, norm_g, ffn_w_gu, ffn_w_down, ab_w_in, ab_w_out, a_ln_g, a_w_s, a_b_s,
              b_conv_w, b_filt_w1, b_filt_b1, b_filt_freq, b_filt_w2, b_filt_b2, b_filt_w3,
              b_decay, b_skip, cd_w_in, cd_w_out, c_w, c_scale, d_conv_w):
    for i in range(DEPTH):
        g = norm_g[i]
        h = rmsnorm(x, g[0])
        if i % 2 == 0:
            j = i // 2
            proj = h @ ab_w_in[j]
            za = jax.nn.gelu(proj[..., :2 * A_WIDTH])
            y_a = spatial_gating(za[..., :A_WIDTH], za[..., A_WIDTH:], a_ln_g[j], a_w_s[j], a_b_s[j])
            y_b = hyena_mixer(proj[..., 2 * A_WIDTH:], b_conv_w[j], b_filt_w1[j], b_filt_b1[j],
                              b_filt_freq[j], b_filt_w2[j], b_filt_b2[j], b_filt_w3[j],
                              b_decay[j], b_skip[j])
            mix = jnp.concatenate([y_a, y_b], axis=-1) @ ab_w_out[j]
        else:
            j = i // 2
            proj = h @ cd_w_in[j]
            y_c = multiscale_pool(proj[..., :C_WIDTH], c_w[j], c_scale[j])
            y_d = short_gated_conv(proj[..., C_WIDTH:], d_conv_w[j])
            mix = jnp.concatenate([y_c, y_d], axis=-1) @ cd_w_out[j]
        x = x + rmsnorm(mix, g[1])
        f = swiglu(rmsnorm(x, g[2]), ffn_w_gu[i], ffn_w_down[i])
        x = x + rmsnorm(f, g[3])
    return x
```

```python
import functools
import math

import jax
import jax.numpy as jnp
from jax import lax
from jax.experimental import pallas as pl
from jax.experimental.pallas import tpu as pltpu

F32 = jnp.float32
BF16 = jnp.bfloat16

D_MODEL = 1024
SEQ = 2048
DEPTH = 4
HALF = D_MODEL // 2
CHUNK = 128
N_CHUNKS = SEQ // CHUNK
A_GROUPS = 4
GROUP_DIM = HALF // A_GROUPS
FILTER_BANDS = 16
FILTER_HIDDEN = 64
POOL_WINDOWS = (2, 4, 8, 16)
D_FF = 2816
RMS_EPS = 1e-6
LN_EPS = 1e-5
NFFT = 2 * SEQ

ROW_TILE = 512
FF_CHUNK = 256
HY_CT = 256
DFT_TILE = 512
PW_ROWS = 128
VMEM_LIMIT = 56 * 1024 * 1024


def _params(n_axes, vmem=VMEM_LIMIT):
    return pltpu.CompilerParams(
        dimension_semantics=("arbitrary",) * n_axes, vmem_limit_bytes=vmem)


def _resident(shape, index_map):
    return pl.BlockSpec(shape, index_map, pipeline_mode=pl.Buffered(1))


def _rms(x, g):
    return x * lax.rsqrt(jnp.mean(x * x, axis=-1, keepdims=True) + RMS_EPS) * g


def _bdot(a, b):
    return jnp.dot(a, b, preferred_element_type=F32)


def _fdot(a, b):
    return jnp.dot(a, b, preferred_element_type=F32, precision=lax.Precision.HIGHEST)


def _shift_rows(x, k, row):
    n = x.shape[0]
    y = pltpu.roll(x, k % n, axis=0)
    if k > 0:
        return jnp.where(row >= k, y, 0.0)
    return jnp.where(row < n + k, y, 0.0)


def _conv3(x, w, row):
    return (_shift_rows(x, 1, row) * w[0:1, :] + x * w[1:2, :]
            + _shift_rows(x, -1, row) * w[2:3, :])


def _dft_kernel(c_ref, s_ref):
    rows = c_ref.shape[0]
    f = pl.program_id(0) * rows + lax.broadcasted_iota(jnp.int32, (rows, 128), 0)
    lane = lax.broadcasted_iota(jnp.int32, (rows, 128), 1)
    scale = 2.0 * math.pi / NFFT
    ang_b = ((f * lane) & (NFFT - 1)).astype(F32) * scale
    ang_a = ((f * (lane * 128)) & (NFFT - 1)).astype(F32) * scale
    cb, sb = jnp.cos(ang_b), jnp.sin(ang_b)
    ca, sa = jnp.cos(ang_a), jnp.sin(ang_a)
    for a in range(SEQ // 128):
        ca_a = ca[:, a:a + 1]
        sa_a = sa[:, a:a + 1]
        c_ref[:, a * 128:(a + 1) * 128] = (ca_a * cb - sa_a * sb).astype(BF16)
        s_ref[:, a * 128:(a + 1) * 128] = (sa_a * cb + ca_a * sb).astype(BF16)


def _dft_tables():
    rows = 256
    return pl.pallas_call(
        _dft_kernel,
        out_shape=(jax.ShapeDtypeStruct((SEQ, SEQ), BF16),) * 2,
        grid=(SEQ // rows,),
        out_specs=(pl.BlockSpec((rows, SEQ), lambda i: (i, 0)),) * 2,
        compiler_params=_params(1),
        name="dft_tables",
    )()


def _filter_kernel(bands_ref, w1_ref, b1_ref, freq_ref, w2_ref, b2_ref, w3f_ref, w3b_ref,
                   decf_ref, decb_ref, c_ref, s_ref, hre_ref, him_ref, hny_ref):
    L = SEQ
    t = lax.broadcasted_iota(jnp.int32, (L, 1), 0).astype(F32)
    t01 = t / (L - 1)
    fw = (2 * math.pi * t / L) * bands_ref[...]
    pre = (t01 * w1_ref[0:1, :] + _fdot(jnp.cos(fw), w1_ref[1:1 + FILTER_BANDS, :])
           + _fdot(-jnp.sin(fw), w1_ref[1 + FILTER_BANDS:, :]) + b1_ref[...])
    h = jnp.sin(freq_ref[0:1, :] * pre)
    h = jnp.sin(freq_ref[1:2, :] * (_fdot(h, w2_ref[...]) + b2_ref[...]))
    fwd = _fdot(h, w3f_ref[...]) * jnp.exp(-t01 * jnp.abs(decf_ref[...]))
    bwd = _fdot(h, w3b_ref[...]) * jnp.exp(-t01 * jnp.abs(decb_ref[...]))
    row = lax.broadcasted_iota(jnp.int32, fwd.shape, 0)
    bsh = _shift_rows(bwd, 1, row)
    ev = fwd + bsh
    od = fwd - bsh
    wgt = jnp.where(row == 0, 1.0 / NFFT, 2.0 / NFFT)
    hre_ref[...] = _bdot(c_ref[...], ev.astype(BF16)) * wgt
    him_ref[...] = -_bdot(s_ref[...], od.astype(BF16)) * wgt
    sign = jnp.where((row & 1) == 0, 1.0, -1.0)
    hny_ref[...] = jnp.sum(ev * sign, axis=0, keepdims=True) * (1.0 / NFFT)


def _filter_spectra(bands, w1, b1, freq, w2, b2, w3, decay, cmat, smat):
    const = lambda *_: (0, 0)
    ct = HY_CT
    nct = HALF // ct
    fcol = lambda o, c: (0, o * nct + c)
    bcol = lambda o, c: (0, (2 + o) * nct + c)
    return pl.pallas_call(
        _filter_kernel,
        out_shape=(jax.ShapeDtypeStruct((2, SEQ, HALF), F32),
                   jax.ShapeDtypeStruct((2, SEQ, HALF), F32),
                   jax.ShapeDtypeStruct((2, 1, HALF), F32)),
        grid=(2, nct),
        in_specs=[
            pl.BlockSpec((1, FILTER_BANDS), const),
            pl.BlockSpec((1 + 2 * FILTER_BANDS, FILTER_HIDDEN), const),
            pl.BlockSpec((1, FILTER_HIDDEN), const),
            pl.BlockSpec((2, FILTER_HIDDEN), const),
            pl.BlockSpec((FILTER_HIDDEN, FILTER_HIDDEN), const),
            pl.BlockSpec((1, FILTER_HIDDEN), const),
            pl.BlockSpec((FILTER_HIDDEN, ct), fcol),
            pl.BlockSpec((FILTER_HIDDEN, ct), bcol),
            pl.BlockSpec((1, ct), fcol),
            pl.BlockSpec((1, ct), bcol),
            _resident((SEQ, SEQ), const),
            _resident((SEQ, SEQ), const),
        ],
        out_specs=(pl.BlockSpec((None, SEQ, ct), lambda o, c: (o, 0, c)),
                   pl.BlockSpec((None, SEQ, ct), lambda o, c: (o, 0, c)),
                   pl.BlockSpec((None, 1, ct), lambda o, c: (o, 0, c))),
        compiler_params=_params(2),
        name="hyena_filter_spectra",
    )(bands, w1, b1, freq, w2, b2, w3, w3, decay, decay, cmat, smat)


def _in_proj_kernel(x_ref, g_ref, w_ref, o_ref, *, gelu_cols):
    h = _rms(x_ref[...], g_ref[...]).astype(BF16)
    n_out = o_ref.shape[1]
    for c0 in range(0, n_out, 512):
        p = _bdot(h, w_ref[:, c0:c0 + 512])
        if c0 < gelu_cols:
            p = jax.nn.gelu(p)
        o_ref[:, c0:c0 + 512] = p.astype(BF16)


def _in_proj(x2d, g, w, gelu_cols):
    m = x2d.shape[0]
    n_out = w.shape[1]
    return pl.pallas_call(
        functools.partial(_in_proj_kernel, gelu_cols=gelu_cols),
        out_shape=jax.ShapeDtypeStruct((m, n_out), BF16),
        grid=(m // ROW_TILE,),
        in_specs=[pl.BlockSpec((ROW_TILE, D_MODEL), lambda i: (i, 0)),
                  pl.BlockSpec((1, D_MODEL), lambda i: (0, 0)),
                  _resident((D_MODEL, n_out), lambda i: (0, 0))],
        out_specs=pl.BlockSpec((ROW_TILE, n_out), lambda i: (i, 0)),
        compiler_params=_params(1),
        name="rms_in_proj",
    )(x2d, g, w)


def _gmlp_kernel(za_ref, lng_ref, ws_ref, bs_ref, o_ref, vn_ref):
    def ln_body(i, carry):
        r = pl.ds(pl.multiple_of(i * PW_ROWS, PW_ROWS), PW_ROWS)
        v = za_ref[r, HALF:].astype(F32)
        xc = v - jnp.mean(v, axis=-1, keepdims=True)
        y = xc * lax.rsqrt(jnp.mean(xc * xc, axis=-1, keepdims=True) + LN_EPS) * lng_ref[...]
        vn_ref[r, :] = y.astype(BF16)
        return carry
    lax.fori_loop(0, SEQ // PW_ROWS, ln_body, 0)

    for g in range(A_GROUPS):
        cols = slice(g * GROUP_DIM, (g + 1) * GROUP_DIM)
        vg = jnp.concatenate(
            [vn_ref[n * CHUNK:(n + 1) * CHUNK, cols] for n in range(N_CHUNKS)], axis=1)
        s = _bdot(ws_ref[g], vg) + bs_ref[g]
        for n in range(N_CHUNKS):
            rows = slice(n * CHUNK, (n + 1) * CHUNK)
            u = za_ref[rows, cols].astype(F32)
            o_ref[rows, cols] = (u * s[:, n * GROUP_DIM:(n + 1) * GROUP_DIM]).astype(BF16)


def _gmlp(proj3d, ln_g, w_s, b_s):
    bsz = proj3d.shape[0]
    return pl.pallas_call(
        _gmlp_kernel,
        out_shape=jax.ShapeDtypeStruct((bsz, SEQ, HALF), BF16),
        grid=(bsz,),
        in_specs=[pl.BlockSpec((None, SEQ, 2 * HALF), lambda b: (b, 0, 0)),
                  pl.BlockSpec((1, HALF), lambda b: (0, 0)),
                  pl.BlockSpec((A_GROUPS, CHUNK, CHUNK), lambda b: (0, 0, 0)),
                  pl.BlockSpec((A_GROUPS, CHUNK, 1), lambda b: (0, 0, 0))],
        out_specs=pl.BlockSpec((None, SEQ, HALF), lambda b: (b, 0, 0)),
        scratch_shapes=[pltpu.VMEM((SEQ, HALF), BF16)],
        compiler_params=_params(1),
        name="gmlp_spatial_gating",
    )(proj3d, ln_g, w_s, b_s)


def _hyena_kernel(v_ref, g1_ref, g2_ref, wv_ref, wg1_ref, wg2_ref, skip_ref,
                  hre_ref, him_ref, hny_ref, c_ref, s_ref, o_ref,
                  z_ref, zb_ref, yre_ref, yim_ref, gate_ref):
    L = SEQ
    ct = o_ref.shape[1]
    row = lax.broadcasted_iota(jnp.int32, (L, ct), 0)
    sign = jnp.where((row & 1) == 0, 1.0, -1.0)

    z_ref[...] = _conv3(v_ref[...].astype(F32), wv_ref[...], row)
    for o, (g_ref, wg_ref) in enumerate(((g1_ref, wg1_ref), (g2_ref, wg2_ref))):
        gate_ref[...] = _conv3(g_ref[...].astype(F32), wg_ref[...], row)
        z = z_ref[...]
        zb_ref[...] = z.astype(BF16)
        y_ny = jnp.sum(z * sign, axis=0, keepdims=True) * hny_ref[o]

        def fwd_body(i, carry):
            r = pl.ds(pl.multiple_of(i * DFT_TILE, DFT_TILE), DFT_TILE)
            zre = _bdot(c_ref[r, :], zb_ref[...])
            zmi = _bdot(s_ref[r, :], zb_ref[...])
            hre = hre_ref[o, r, :]
            him = him_ref[o, r, :]
            yre_ref[r, :] = (zre * hre + zmi * him).astype(BF16)
            yim_ref[r, :] = (zmi * hre - zre * him).astype(BF16)
            return carry
        lax.fori_loop(0, L // DFT_TILE, fwd_body, 0)

        def inv_body(i, carry):
            r = pl.ds(pl.multiple_of(i * DFT_TILE, DFT_TILE), DFT_TILE)
            sgn = jnp.where((lax.broadcasted_iota(jnp.int32, (DFT_TILE, ct), 0) & 1) == 0,
                            1.0, -1.0)
            conv = (_bdot(c_ref[r, :], yre_ref[...]) + _bdot(s_ref[r, :], yim_ref[...])
                    + sgn * y_ny)
            z_ref[r, :] = gate_ref[r, :] * (conv + z_ref[r, :] * skip_ref[o:o + 1, :])
            return carry
        lax.fori_loop(0, L // DFT_TILE, inv_body, 0)
    o_ref[...] = z_ref[...].astype(BF16)


def _hyena(proj3d, conv_w, skip, hre, him, hny, cmat, smat):
    bsz = proj3d.shape[0]
    ct = HY_CT
    nct = HALF // ct
    base = 2 * HALF // ct

    def slab(part):
        return pl.BlockSpec((None, SEQ, ct), lambda c, b: (b, 0, base + part * nct + c))

    def cw(part):
        return pl.BlockSpec((3, ct), lambda c, b: (0, part * nct + c))

    hspec = _resident((2, SEQ, ct), lambda c, b: (0, 0, c))
    const = lambda c, b: (0, 0)
    return pl.pallas_call(
        _hyena_kernel,
        out_shape=jax.ShapeDtypeStruct((bsz, SEQ, HALF), BF16),
        grid=(nct, bsz),
        in_specs=[slab(0), slab(1), slab(2), cw(0), cw(1), cw(2),
                  pl.BlockSpec((2, ct), lambda c, b: (0, c)),
                  hspec, hspec,
                  pl.BlockSpec((2, 1, ct), lambda c, b: (0, 0, c)),
                  _resident((SEQ, SEQ), const), _resident((SEQ, SEQ), const)],
        out_specs=pl.BlockSpec((None, SEQ, ct), lambda c, b: (b, 0, c)),
        scratch_shapes=[pltpu.VMEM((SEQ, ct), F32), pltpu.VMEM((SEQ, ct), BF16),
                        pltpu.VMEM((SEQ, ct), BF16), pltpu.VMEM((SEQ, ct), BF16),
                        pltpu.VMEM((SEQ, ct), F32)],
        compiler_params=_params(2),
        name="hyena_long_conv",
    )(proj3d, proj3d, proj3d, conv_w, conv_w, conv_w, skip, hre, him, hny, cmat, smat)


def _pool_conv_kernel(p_ref, b_ref, c_ref, h_ref, cw_ref, scale_ref, dw_ref, yc_ref, yd_ref):
    L = SEQ
    row = lax.broadcasted_iota(jnp.int32, (L, GROUP_DIM), 0)
    for g, w in enumerate(POOL_WINDOWS):
        cols = slice(g * GROUP_DIM, (g + 1) * GROUP_DIM)
        left = w // 2
        right = w - left - 1
        x = p_ref[:, cols].astype(F32)
        fwd = x
        bwd = x
        k = 1
        while k < w // 2:
            fwd = fwd + _shift_rows(fwd, -k, row)
            bwd = bwd + _shift_rows(bwd, k, row)
            k *= 2
        s = _shift_rows(bwd, 1, row) + fwd
        cnt = (jnp.minimum(row + right, L - 1) - jnp.maximum(row - left, 0) + 1).astype(F32)
        pooled = s / cnt - x
        y = _bdot(pooled.astype(BF16), cw_ref[g]) * scale_ref[:, cols]
        yc_ref[:, cols] = y.astype(BF16)

        ch = c_ref[:, cols].astype(F32) * h_ref[:, cols].astype(F32)
        yd = b_ref[:, cols].astype(F32) * _conv3(ch, dw_ref[:, cols], row)
        yd_ref[:, cols] = yd.astype(BF16)


def _pool_conv(proj3d, c_w, c_scale, d_conv_w):
    bsz = proj3d.shape[0]

    def part(k):
        return pl.BlockSpec((None, SEQ, HALF), lambda b: (b, 0, k))

    out = pl.BlockSpec((None, SEQ, HALF), lambda b: (b, 0, 0))
    return pl.pallas_call(
        _pool_conv_kernel,
        out_shape=(jax.ShapeDtypeStruct((bsz, SEQ, HALF), BF16),) * 2,
        grid=(bsz,),
        in_specs=[part(0), part(1), part(2), part(3),
                  pl.BlockSpec((len(POOL_WINDOWS), GROUP_DIM, GROUP_DIM), lambda b: (0, 0, 0)),
                  pl.BlockSpec((1, HALF), lambda b: (0, 0)),
                  pl.BlockSpec((3, HALF), lambda b: (0, 0))],
        out_specs=(out, out),
        compiler_params=_params(1),
        name="pool_short_conv",
    )(proj3d, proj3d, proj3d, proj3d, c_w, c_scale, d_conv_w)


def _out_ffn_kernel(x_ref, y0_ref, y1_ref, wo_ref, g_ref, wgu_ref, wd_ref, o_ref, a_ref):
    mix = _bdot(y0_ref[...], wo_ref[:HALF, :]) + _bdot(y1_ref[...], wo_ref[HALF:, :])
    x1 = x_ref[...] + _rms(mix, g_ref[1:2, :])
    h = _rms(x1, g_ref[2:3, :]).astype(BF16)
    for k in range(D_FF // FF_CHUNK):
        gate = _bdot(h, wgu_ref[:, k * FF_CHUNK:(k + 1) * FF_CHUNK])
        up = _bdot(h, wgu_ref[:, D_FF + k * FF_CHUNK:D_FF + (k + 1) * FF_CHUNK])
        a_ref[:, k * FF_CHUNK:(k + 1) * FF_CHUNK] = (gate * jax.nn.sigmoid(gate) * up).astype(BF16)
    f = _bdot(a_ref[...], wd_ref[...])
    o_ref[...] = x1 + _rms(f, g_ref[3:4, :])


def _out_ffn(x2d, y0, y1, w_out, g, w_gu, w_down):
    m = x2d.shape[0]
    tile = lambda n: pl.BlockSpec((ROW_TILE, n), lambda i: (i, 0))
    const = lambda i: (0, 0)
    return pl.pallas_call(
        _out_ffn_kernel,
        out_shape=jax.ShapeDtypeStruct((m, D_MODEL), F32),
        grid=(m // ROW_TILE,),
        in_specs=[tile(D_MODEL), tile(HALF), tile(HALF),
                  _resident((D_MODEL, D_MODEL), const),
                  pl.BlockSpec((4, D_MODEL), const),
                  _resident((D_MODEL, 2 * D_FF), const),
                  _resident((D_FF, D_MODEL), const)],
        out_specs=tile(D_MODEL),
        scratch_shapes=[pltpu.VMEM((ROW_TILE, D_FF), BF16)],
        compiler_params=_params(1),
        name="out_proj_ffn",
    )(x2d, y0, y1, w_out, g, w_gu, w_down)


def kernel(x, norm_g, ffn_w_gu, ffn_w_down, ab_w_in, ab_w_out, a_ln_g, a_w_s, a_b_s, b_conv_w, b_filt_w1, b_filt_b1, b_filt_freq, b_filt_w2, b_filt_b2, b_filt_w3, b_decay, b_skip, cd_w_in, cd_w_out, c_w, c_scale, d_conv_w):
    bsz, seq, d = x.shape
    assert (seq, d) == (SEQ, D_MODEL) and norm_g.shape[0] == DEPTH
    m = bsz * seq
    bands = jnp.linspace(1e-4, FILTER_BANDS - 1, FILTER_BANDS, dtype=F32)[None, :]
    cmat, smat = _dft_tables()
    x2d = x.reshape(m, d)
    for i in range(DEPTH):
        j = i // 2
        g = norm_g[i]
        if i % 2 == 0:
            proj = _in_proj(x2d, g[0:1], ab_w_in[j].astype(BF16), 2 * HALF).reshape(bsz, seq, -1)
            y0 = _gmlp(proj, a_ln_g[j][None, :], a_w_s[j].astype(BF16), a_b_s[j][:, :, None])
            hre, him, hny = _filter_spectra(
                bands, b_filt_w1[j], b_filt_b1[j][None, :], b_filt_freq[j], b_filt_w2[j],
                b_filt_b2[j][None, :], b_filt_w3[j], b_decay[j][None, :], cmat, smat)
            y1 = _hyena(proj, b_conv_w[j], b_skip[j], hre, him, hny, cmat, smat)
            w_out = ab_w_out[j]
        else:
            proj = _in_proj(x2d, g[0:1], cd_w_in[j].astype(BF16), 0).reshape(bsz, seq, -1)
            y0, y1 = _pool_conv(proj, c_w[j].astype(BF16), c_scale[j][None, :], d_conv_w[j])
            w_out = cd_w_out[j]
        x2d = _out_ffn(x2d, y0.reshape(m, HALF), y1.reshape(m, HALF), w_out.astype(BF16), g,
                       ffn_w_gu[i].astype(BF16), ffn_w_down[i].astype(BF16))
    return x2d.reshape(bsz, seq, d)
```

```python
import functools
import math

import jax
import jax.numpy as jnp
from jax import lax
from jax.experimental import pallas as pl
from jax.experimental.pallas import tpu as pltpu

F32 = jnp.float32
BF16 = jnp.bfloat16

D_MODEL = 1024
SEQ = 2048
DEPTH = 4
HALF = D_MODEL // 2
CHUNK = 128
N_CHUNKS = SEQ // CHUNK
A_GROUPS = 4
GROUP_DIM = HALF // A_GROUPS
FILTER_BANDS = 16
FILTER_HIDDEN = 64
POOL_WINDOWS = (2, 4, 8, 16)
D_FF = 2816
RMS_EPS = 1e-6
LN_EPS = 1e-5
HY_P = 512
HY_NB = SEQ // HY_P
HY_LAGS = 2 * HY_NB - 1
NFFT = 2 * HY_P

ROW_TILE = 512
FF_CHUNK = 256
HY_CT = 256
MIX_ROWS = 16
PW_ROWS = 128
VMEM_LIMIT = 56 * 1024 * 1024


def _params(n_axes, vmem=VMEM_LIMIT):
    return pltpu.CompilerParams(
        dimension_semantics=("arbitrary",) * n_axes, vmem_limit_bytes=vmem)


def _resident(shape, index_map):
    return pl.BlockSpec(shape, index_map, pipeline_mode=pl.Buffered(1))


def _rms(x, g):
    return x * lax.rsqrt(jnp.mean(x * x, axis=-1, keepdims=True) + RMS_EPS) * g


def _bdot(a, b):
    return jnp.dot(a, b, preferred_element_type=F32)


def _fdot(a, b):
    return jnp.dot(a, b, preferred_element_type=F32, precision=lax.Precision.HIGHEST)


def _shift_rows(x, k, row):
    n = x.shape[0]
    y = pltpu.roll(x, k % n, axis=0)
    if k > 0:
        return jnp.where(row >= k, y, 0.0)
    return jnp.where(row < n + k, y, 0.0)


def _conv3(x, w, row):
    return (_shift_rows(x, 1, row) * w[0:1, :] + x * w[1:2, :]
            + _shift_rows(x, -1, row) * w[2:3, :])


def _dft_kernel(csv_ref, csh_ref):
    P = HY_P
    f = lax.broadcasted_iota(jnp.int32, (P, 128), 0)
    lane = lax.broadcasted_iota(jnp.int32, (P, 128), 1)
    scale = 2.0 * math.pi / NFFT
    ang_b = ((f * lane) & (NFFT - 1)).astype(F32) * scale
    ang_a = ((f * (lane * 128)) & (NFFT - 1)).astype(F32) * scale
    cb, sb = jnp.cos(ang_b), jnp.sin(ang_b)
    ca, sa = jnp.cos(ang_a), jnp.sin(ang_a)
    for a in range(P // 128):
        cols = slice(a * 128, (a + 1) * 128)
        ca_a = ca[:, a:a + 1]
        sa_a = sa[:, a:a + 1]
        c = (ca_a * cb - sa_a * sb).astype(BF16)
        s = (sa_a * cb + ca_a * sb).astype(BF16)
        csv_ref[:P, cols] = c
        csv_ref[P:, cols] = s
        csh_ref[:, a * 128:(a + 1) * 128] = c
        csh_ref[:, P + a * 128:P + (a + 1) * 128] = s


def _dft_tables():
    return pl.pallas_call(
        _dft_kernel,
        out_shape=(jax.ShapeDtypeStruct((2 * HY_P, HY_P), BF16),
                   jax.ShapeDtypeStruct((HY_P, 2 * HY_P), BF16)),
        compiler_params=_params(0),
        name="dft_tables",
    )()


def _filter_hidden_kernel(bands_ref, w1_ref, b1_ref, freq_ref, w2_ref, b2_ref, h_ref):
    L = SEQ
    r = lax.broadcasted_iota(jnp.int32, (L, 1), 0)
    t = jnp.where(pl.program_id(0) == 0, r, L - 1 - r).astype(F32)
    t01 = t / (L - 1)
    fw = (2 * math.pi * t / L) * bands_ref[...]
    pre = (t01 * w1_ref[0:1, :] + _fdot(jnp.cos(fw), w1_ref[1:1 + FILTER_BANDS, :])
           + _fdot(-jnp.sin(fw), w1_ref[1 + FILTER_BANDS:, :]) + b1_ref[...])
    h = jnp.sin(freq_ref[0:1, :] * pre)
    h_ref[...] = jnp.sin(freq_ref[1:2, :] * (_fdot(h, w2_ref[...]) + b2_ref[...]))


def _filter_hidden(bands, w1, b1, freq, w2, b2):
    const = lambda s: (0, 0)
    return pl.pallas_call(
        _filter_hidden_kernel,
        out_shape=jax.ShapeDtypeStruct((2, SEQ, FILTER_HIDDEN), F32),
        grid=(2,),
        in_specs=[pl.BlockSpec((1, FILTER_BANDS), const),
                  pl.BlockSpec((1 + 2 * FILTER_BANDS, FILTER_HIDDEN), const),
                  pl.BlockSpec((1, FILTER_HIDDEN), const),
                  pl.BlockSpec((2, FILTER_HIDDEN), const),
                  pl.BlockSpec((FILTER_HIDDEN, FILTER_HIDDEN), const),
                  pl.BlockSpec((1, FILTER_HIDDEN), const)],
        out_specs=pl.BlockSpec((None, SEQ, FILTER_HIDDEN), lambda s: (s, 0, 0)),
        compiler_params=_params(1),
        name="hyena_filter_hidden",
    )(bands, w1, b1, freq, w2, b2)


def _filter_kernel(hid_ref, w3f_ref, w3b_ref, decf_ref, decb_ref, csv_ref,
                   hre_ref, him_ref, hny_ref):
    L, P, NB = SEQ, HY_P, HY_NB
    ct = hre_ref.shape[-1]
    row = lax.broadcasted_iota(jnp.int32, (L, ct), 0)
    t_asc = row.astype(F32) / (L - 1)
    t_dsc = (L - 1 - row).astype(F32) / (L - 1)
    dec_f = jnp.abs(decf_ref[...])
    dec_b = jnp.abs(decb_ref[...])
    h_asc = hid_ref[0]
    h_dsc = hid_ref[1]
    inner = (row & (P - 1)) != 0
    f_asc = _fdot(h_asc, w3f_ref[...]) * jnp.exp(-t_asc * dec_f)
    b_dsc = _fdot(h_dsc, w3b_ref[...]) * jnp.exp(-t_dsc * dec_b)
    f_dsc = _fdot(h_dsc, w3f_ref[...]) * jnp.exp(-t_dsc * dec_f)
    f_dsc = jnp.where(inner, _shift_rows(f_dsc, 1, row), 0.0)
    b_asc = _fdot(h_asc, w3b_ref[...]) * jnp.exp(-t_asc * dec_b)
    b_asc = jnp.where(inner, _shift_rows(b_asc, 1, row), 0.0)

    def blk(x, i):
        return x[i * P:(i + 1) * P, :]

    frow = lax.broadcasted_iota(jnp.int32, (P, ct), 0)
    wgt = jnp.where(frow == 0, 1.0 / NFFT, 2.0 / NFFT)
    sign = jnp.where((frow & 1) == 0, 1.0, -1.0)
    for idx in range(HY_LAGS):
        d = idx - (NB - 1)
        if d >= 1:
            kp, kn = blk(f_asc, d), blk(f_dsc, NB - d)
        elif d == 0:
            kp, kn = blk(f_asc, 0), blk(b_asc, 0)
        else:
            kp, kn = blk(b_dsc, NB + d), blk(b_asc, -d)
        ev = kp + kn
        od = kp - kn
        hre_ref[idx] = _bdot(csv_ref[:P, :], ev.astype(BF16)) * wgt
        him_ref[idx] = -_bdot(csv_ref[P:, :], od.astype(BF16)) * wgt
        hny_ref[idx:idx + 1, :] = jnp.sum(ev * sign, axis=0, keepdims=True) * (1.0 / NFFT)


def _filter_spectra(hidden, w3, decay, csv):
    ct = HY_CT
    nct = HALF // ct
    fcol = lambda o, c: (0, o * nct + c)
    bcol = lambda o, c: (0, (2 + o) * nct + c)
    hspec = pl.BlockSpec((None, HY_LAGS, HY_P, ct), lambda o, c: (o, 0, 0, c))
    return pl.pallas_call(
        _filter_kernel,
        out_shape=(jax.ShapeDtypeStruct((2, HY_LAGS, HY_P, HALF), F32),
                   jax.ShapeDtypeStruct((2, HY_LAGS, HY_P, HALF), F32),
                   jax.ShapeDtypeStruct((2, HY_LAGS, HALF), F32)),
        grid=(2, nct),
        in_specs=[
            pl.BlockSpec((2, SEQ, FILTER_HIDDEN), lambda o, c: (0, 0, 0)),
            pl.BlockSpec((FILTER_HIDDEN, ct), fcol),
            pl.BlockSpec((FILTER_HIDDEN, ct), bcol),
            pl.BlockSpec((1, ct), fcol),
            pl.BlockSpec((1, ct), bcol),
            pl.BlockSpec((2 * HY_P, HY_P), lambda o, c: (0, 0)),
        ],
        out_specs=(hspec, hspec,
                   pl.BlockSpec((None, HY_LAGS, ct), lambda o, c: (o, 0, c))),
        compiler_params=_params(2),
        name="hyena_filter_spectra",
    )(hidden, w3, w3, decay, decay, csv)


def _in_proj_kernel(x_ref, g_ref, w_ref, o_ref, *, gelu_cols):
    h = _rms(x_ref[...], g_ref[...]).astype(BF16)
    n_out = o_ref.shape[1]
    for c0 in range(0, n_out, 512):
        p = _bdot(h, w_ref[:, c0:c0 + 512])
        if c0 < gelu_cols:
            p = jax.nn.gelu(p)
        o_ref[:, c0:c0 + 512] = p.astype(BF16)


def _in_proj(x2d, g, w, gelu_cols):
    m = x2d.shape[0]
    n_out = w.shape[1]
    return pl.pallas_call(
        functools.partial(_in_proj_kernel, gelu_cols=gelu_cols),
        out_shape=jax.ShapeDtypeStruct((m, n_out), BF16),
        grid=(m // ROW_TILE,),
        in_specs=[pl.BlockSpec((ROW_TILE, D_MODEL), lambda i: (i, 0)),
                  pl.BlockSpec((1, D_MODEL), lambda i: (0, 0)),
                  _resident((D_MODEL, n_out), lambda i: (0, 0))],
        out_specs=pl.BlockSpec((ROW_TILE, n_out), lambda i: (i, 0)),
        compiler_params=_params(1),
        name="rms_in_proj",
    )(x2d, g, w)


def _gmlp_kernel(za_ref, lng_ref, ws_ref, bs_ref, o_ref, vn_ref):
    def ln_body(i, carry):
        r = pl.ds(pl.multiple_of(i * PW_ROWS, PW_ROWS), PW_ROWS)
        v = za_ref[r, HALF:].astype(F32)
        xc = v - jnp.mean(v, axis=-1, keepdims=True)
        y = xc * lax.rsqrt(jnp.mean(xc * xc, axis=-1, keepdims=True) + LN_EPS) * lng_ref[...]
        vn_ref[r, :] = y.astype(BF16)
        return carry
    lax.fori_loop(0, SEQ // PW_ROWS, ln_body, 0)

    for g in range(A_GROUPS):
        cols = slice(g * GROUP_DIM, (g + 1) * GROUP_DIM)
        vg = jnp.concatenate(
            [vn_ref[n * CHUNK:(n + 1) * CHUNK, cols] for n in range(N_CHUNKS)], axis=1)
        s = _bdot(ws_ref[g], vg) + bs_ref[g]
        for n in range(N_CHUNKS):
            rows = slice(n * CHUNK, (n + 1) * CHUNK)
            u = za_ref[rows, cols].astype(F32)
            o_ref[rows, cols] = (u * s[:, n * GROUP_DIM:(n + 1) * GROUP_DIM]).astype(BF16)


def _gmlp(proj3d, ln_g, w_s, b_s):
    bsz = proj3d.shape[0]
    return pl.pallas_call(
        _gmlp_kernel,
        out_shape=jax.ShapeDtypeStruct((bsz, SEQ, HALF), BF16),
        grid=(bsz,),
        in_specs=[pl.BlockSpec((None, SEQ, 2 * HALF), lambda b: (b, 0, 0)),
                  pl.BlockSpec((1, HALF), lambda b: (0, 0)),
                  pl.BlockSpec((A_GROUPS, CHUNK, CHUNK), lambda b: (0, 0, 0)),
                  pl.BlockSpec((A_GROUPS, CHUNK, 1), lambda b: (0, 0, 0))],
        out_specs=pl.BlockSpec((None, SEQ, HALF), lambda b: (b, 0, 0)),
        scratch_shapes=[pltpu.VMEM((SEQ, HALF), BF16)],
        compiler_params=_params(1),
        name="gmlp_spatial_gating",
    )(proj3d, ln_g, w_s, b_s)


def _hyena_kernel(v_ref, g1_ref, g2_ref, wv_ref, wg1_ref, wg2_ref, skip_ref,
                  hre_ref, him_ref, hny_ref, csv_ref, csh_ref, o_ref,
                  z_ref, gate_ref, zs_ref, ys_ref):
    L, P, NB = SEQ, HY_P, HY_NB
    ct = o_ref.shape[1]
    row = lax.broadcasted_iota(jnp.int32, (L, ct), 0)
    sign = jnp.where((lax.broadcasted_iota(jnp.int32, (P, ct), 0) & 1) == 0, 1.0, -1.0)

    z_ref[...] = _conv3(v_ref[...].astype(F32), wv_ref[...], row)
    for o, (g_ref, wg_ref) in enumerate(((g1_ref, wg1_ref), (g2_ref, wg2_ref))):
        gate_ref[...] = _conv3(g_ref[...].astype(F32), wg_ref[...], row)
        z_ny = []
        for j in range(NB):
            zj = z_ref[j * P:(j + 1) * P, :]
            z_ny.append(jnp.sum(zj * sign, axis=0, keepdims=True))
            zs_ref[:, j * ct:(j + 1) * ct] = _bdot(csv_ref[...], zj.astype(BF16))

        def mix_body(c, carry):
            re = pl.ds(pl.multiple_of(c * MIX_ROWS, MIX_ROWS), MIX_ROWS)
            im = pl.ds(pl.multiple_of(P + c * MIX_ROWS, MIX_ROWS), MIX_ROWS)
            zre = [zs_ref[re, j * ct:(j + 1) * ct] for j in range(NB)]
            zmi = [zs_ref[im, j * ct:(j + 1) * ct] for j in range(NB)]
            hre = [hre_ref[o, d, re, :] for d in range(HY_LAGS)]
            him = [him_ref[o, d, re, :] for d in range(HY_LAGS)]
            for i in range(NB):
                yre = 0.0
                ymi = 0.0
                for j in range(NB):
                    d = i - j + NB - 1
                    yre = yre + (zre[j] * hre[d] + zmi[j] * him[d])
                    ymi = ymi + (zmi[j] * hre[d] - zre[j] * him[d])
                ys_ref[re, i * ct:(i + 1) * ct] = yre.astype(BF16)
                ys_ref[im, i * ct:(i + 1) * ct] = ymi.astype(BF16)
            return carry
        lax.fori_loop(0, P // MIX_ROWS, mix_body, 0)

        for i in range(NB):
            rows = slice(i * P, (i + 1) * P)
            y_ny = sum(z_ny[j] * hny_ref[o, i - j + NB - 1:i - j + NB, :] for j in range(NB))
            conv = _bdot(csh_ref[...], ys_ref[:, i * ct:(i + 1) * ct]) + sign * y_ny
            z_ref[rows, :] = gate_ref[rows, :] * (conv + z_ref[rows, :] * skip_ref[o:o + 1, :])
    o_ref[...] = z_ref[...].astype(BF16)


def _hyena(proj3d, conv_w, skip, hre, him, hny, csv, csh):
    bsz = proj3d.shape[0]
    ct = HY_CT
    nct = HALF // ct
    base = 2 * HALF // ct

    def slab(part):
        return pl.BlockSpec((None, SEQ, ct), lambda c, b: (b, 0, base + part * nct + c))

    def cw(part):
        return pl.BlockSpec((3, ct), lambda c, b: (0, part * nct + c))

    hspec = _resident((2, HY_LAGS, HY_P, ct), lambda c, b: (0, 0, 0, c))
    const = lambda c, b: (0, 0)
    return pl.pallas_call(
        _hyena_kernel,
        out_shape=jax.ShapeDtypeStruct((bsz, SEQ, HALF), BF16),
        grid=(nct, bsz),
        in_specs=[slab(0), slab(1), slab(2), cw(0), cw(1), cw(2),
                  pl.BlockSpec((2, ct), lambda c, b: (0, c)),
                  hspec, hspec,
                  pl.BlockSpec((2, HY_LAGS, ct), lambda c, b: (0, 0, c)),
                  pl.BlockSpec((2 * HY_P, HY_P), const), pl.BlockSpec((HY_P, 2 * HY_P), const)],
        out_specs=pl.BlockSpec((None, SEQ, ct), lambda c, b: (b, 0, c)),
        scratch_shapes=[pltpu.VMEM((SEQ, ct), F32), pltpu.VMEM((SEQ, ct), F32),
                        pltpu.VMEM((2 * HY_P, HY_NB * ct), F32),
                        pltpu.VMEM((2 * HY_P, HY_NB * ct), BF16)],
        compiler_params=_params(2),
        name="hyena_long_conv",
    )(proj3d, proj3d, proj3d, conv_w, conv_w, conv_w, skip, hre, him, hny, csv, csh)


def _pool_conv_kernel(p_ref, b_ref, c_ref, h_ref, cw_ref, scale_ref, dw_ref, yc_ref, yd_ref):
    L = SEQ
    row = lax.broadcasted_iota(jnp.int32, (L, GROUP_DIM), 0)
    for g, w in enumerate(POOL_WINDOWS):
        cols = slice(g * GROUP_DIM, (g + 1) * GROUP_DIM)
        left = w // 2
        right = w - left - 1
        x = p_ref[:, cols].astype(F32)
        fwd = x
        bwd = x
        k = 1
        while k < w // 2:
            fwd = fwd + _shift_rows(fwd, -k, row)
            bwd = bwd + _shift_rows(bwd, k, row)
            k *= 2
        s = _shift_rows(bwd, 1, row) + fwd
        cnt = (jnp.minimum(row + right, L - 1) - jnp.maximum(row - left, 0) + 1).astype(F32)
        pooled = s / cnt - x
        y = _bdot(pooled.astype(BF16), cw_ref[g]) * scale_ref[:, cols]
        yc_ref[:, cols] = y.astype(BF16)

        ch = c_ref[:, cols].astype(F32) * h_ref[:, cols].astype(F32)
        yd = b_ref[:, cols].astype(F32) * _conv3(ch, dw_ref[:, cols], row)
        yd_ref[:, cols] = yd.astype(BF16)


def _pool_conv(proj3d, c_w, c_scale, d_conv_w):
    bsz = proj3d.shape[0]

    def part(k):
        return pl.BlockSpec((None, SEQ, HALF), lambda b: (b, 0, k))

    out = pl.BlockSpec((None, SEQ, HALF), lambda b: (b, 0, 0))
    return pl.pallas_call(
        _pool_conv_kernel,
        out_shape=(jax.ShapeDtypeStruct((bsz, SEQ, HALF), BF16),) * 2,
        grid=(bsz,),
        in_specs=[part(0), part(1), part(2), part(3),
                  pl.BlockSpec((len(POOL_WINDOWS), GROUP_DIM, GROUP_DIM), lambda b: (0, 0, 0)),
                  pl.BlockSpec((1, HALF), lambda b: (0, 0)),
                  pl.BlockSpec((3, HALF), lambda b: (0, 0))],
        out_specs=(out, out),
        compiler_params=_params(1),
        name="pool_short_conv",
    )(proj3d, proj3d, proj3d, proj3d, c_w, c_scale, d_conv_w)


def _out_ffn_kernel(x_ref, y0_ref, y1_ref, wo_ref, g_ref, wgu_ref, wd_ref, o_ref, a_ref):
    mix = _bdot(y0_ref[...], wo_ref[:HALF, :]) + _bdot(y1_ref[...], wo_ref[HALF:, :])
    x1 = x_ref[...] + _rms(mix, g_ref[1:2, :])
    h = _rms(x1, g_ref[2:3, :]).astype(BF16)
    for k in range(D_FF // FF_CHUNK):
        gate = _bdot(h, wgu_ref[:, k * FF_CHUNK:(k + 1) * FF_CHUNK])
        up = _bdot(h, wgu_ref[:, D_FF + k * FF_CHUNK:D_FF + (k + 1) * FF_CHUNK])
        a_ref[:, k * FF_CHUNK:(k + 1) * FF_CHUNK] = (gate * jax.nn.sigmoid(gate) * up).astype(BF16)
    f = _bdot(a_ref[...], wd_ref[...])
    o_ref[...] = x1 + _rms(f, g_ref[3:4, :])


def _out_ffn(x2d, y0, y1, w_out, g, w_gu, w_down):
    m = x2d.shape[0]
    tile = lambda n: pl.BlockSpec((ROW_TILE, n), lambda i: (i, 0))
    const = lambda i: (0, 0)
    return pl.pallas_call(
        _out_ffn_kernel,
        out_shape=jax.ShapeDtypeStruct((m, D_MODEL), F32),
        grid=(m // ROW_TILE,),
        in_specs=[tile(D_MODEL), tile(HALF), tile(HALF),
                  _resident((D_MODEL, D_MODEL), const),
                  pl.BlockSpec((4, D_MODEL), const),
                  _resident((D_MODEL, 2 * D_FF), const),
                  _resident((D_FF, D_MODEL), const)],
        out_specs=tile(D_MODEL),
        scratch_shapes=[pltpu.VMEM((ROW_TILE, D_FF), BF16)],
        compiler_params=_params(1),
        name="out_proj_ffn",
    )(x2d, y0, y1, w_out, g, w_gu, w_down)


def kernel(x, norm_g, ffn_w_gu, ffn_w_down, ab_w_in, ab_w_out, a_ln_g, a_w_s, a_b_s, b_conv_w, b_filt_w1, b_filt_b1, b_filt_freq, b_filt_w2, b_filt_b2, b_filt_w3, b_decay, b_skip, cd_w_in, cd_w_out, c_w, c_scale, d_conv_w):
    bsz, seq, d = x.shape
    assert (seq, d) == (SEQ, D_MODEL) and norm_g.shape[0] == DEPTH
    m = bsz * seq
    bands = jnp.linspace(1e-4, FILTER_BANDS - 1, FILTER_BANDS, dtype=F32)[None, :]
    csv, csh = _dft_tables()
    x2d = x.reshape(m, d)
    for i in range(DEPTH):
        j = i // 2
        g = norm_g[i]
        if i % 2 == 0:
            proj = _in_proj(x2d, g[0:1], ab_w_in[j].astype(BF16), 2 * HALF).reshape(bsz, seq, -1)
            y0 = _gmlp(proj, a_ln_g[j][None, :], a_w_s[j].astype(BF16), a_b_s[j][:, :, None])
            hidden = _filter_hidden(bands, b_filt_w1[j], b_filt_b1[j][None, :], b_filt_freq[j],
                                    b_filt_w2[j], b_filt_b2[j][None, :])
            hre, him, hny = _filter_spectra(hidden, b_filt_w3[j], b_decay[j][None, :], csv)
            y1 = _hyena(proj, b_conv_w[j], b_skip[j], hre, him, hny, csv, csh)
            w_out = ab_w_out[j]
        else:
            proj = _in_proj(x2d, g[0:1], cd_w_in[j].astype(BF16), 0).reshape(bsz, seq, -1)
            y0, y1 = _pool_conv(proj, c_w[j].astype(BF16), c_scale[j][None, :], d_conv_w[j])
            w_out = cd_w_out[j]
        x2d = _out_ffn(x2d, y0.reshape(m, HALF), y1.reshape(m, HALF), w_out.astype(BF16), g,
                       ffn_w_gu[i].astype(BF16), ffn_w_down[i].astype(BF16))
    return x2d.reshape(bsz, seq, d)
```

```python
import functools
import math

import jax
import jax.numpy as jnp
from jax import lax
from jax.experimental import pallas as pl
from jax.experimental.pallas import tpu as pltpu

F32 = jnp.float32
BF16 = jnp.bfloat16

D_MODEL = 1024
SEQ = 2048
DEPTH = 4
HALF = D_MODEL // 2
CHUNK = 128
N_CHUNKS = SEQ // CHUNK
A_GROUPS = 4
GROUP_DIM = HALF // A_GROUPS
FILTER_BANDS = 16
FILTER_HIDDEN = 64
POOL_WINDOWS = (2, 4, 8, 16)
D_FF = 2816
RMS_EPS = 1e-6
LN_EPS = 1e-5
HY_P = 512
HY_NB = SEQ // HY_P
HY_LAGS = 2 * HY_NB - 1
NFFT = 2 * HY_P

ROW_TILE = 512
IN_TILE = 1024
FF_CHUNK = 256
HY_CT = 256
MIX_ROWS = 16
PW_ROWS = 128
HALO = 8
VMEM_LIMIT = 56 * 1024 * 1024


def _params(n_axes, vmem=VMEM_LIMIT):
    return pltpu.CompilerParams(
        dimension_semantics=("arbitrary",) * n_axes, vmem_limit_bytes=vmem)


def _resident(shape, index_map):
    return pl.BlockSpec(shape, index_map, pipeline_mode=pl.Buffered(1))


def _rms(x, g):
    return x * lax.rsqrt(jnp.mean(x * x, axis=-1, keepdims=True) + RMS_EPS) * g


def _bdot(a, b):
    return jnp.dot(a, b, preferred_element_type=F32)


def _fdot(a, b):
    return jnp.dot(a, b, preferred_element_type=F32, precision=lax.Precision.HIGHEST)


def _shift_rows(x, k, row):
    n = x.shape[0]
    y = pltpu.roll(x, k % n, axis=0)
    if k > 0:
        return jnp.where(row >= k, y, 0.0)
    return jnp.where(row < n + k, y, 0.0)


def _conv3(x, w, row):
    return (_shift_rows(x, 1, row) * w[0:1, :] + x * w[1:2, :]
            + _shift_rows(x, -1, row) * w[2:3, :])


def _dft_kernel(csv_ref, csh_ref):
    P = HY_P
    f = lax.broadcasted_iota(jnp.int32, (P, 128), 0)
    lane = lax.broadcasted_iota(jnp.int32, (P, 128), 1)
    scale = 2.0 * math.pi / NFFT
    ang_b = ((f * lane) & (NFFT - 1)).astype(F32) * scale
    ang_a = ((f * (lane * 128)) & (NFFT - 1)).astype(F32) * scale
    cb, sb = jnp.cos(ang_b), jnp.sin(ang_b)
    ca, sa = jnp.cos(ang_a), jnp.sin(ang_a)
    for a in range(P // 128):
        cols = slice(a * 128, (a + 1) * 128)
        ca_a = ca[:, a:a + 1]
        sa_a = sa[:, a:a + 1]
        c = (ca_a * cb - sa_a * sb).astype(BF16)
        s = (sa_a * cb + ca_a * sb).astype(BF16)
        csv_ref[:P, cols] = c
        csv_ref[P:, cols] = s
        csh_ref[:, a * 128:(a + 1) * 128] = c
        csh_ref[:, P + a * 128:P + (a + 1) * 128] = s


def _dft_tables():
    return pl.pallas_call(
        _dft_kernel,
        out_shape=(jax.ShapeDtypeStruct((2 * HY_P, HY_P), BF16),
                   jax.ShapeDtypeStruct((HY_P, 2 * HY_P), BF16)),
        compiler_params=_params(0),
        name="dft_tables",
    )()


def _filter_hidden_kernel(bands_ref, w1_ref, b1_ref, freq_ref, w2_ref, b2_ref, h_ref):
    L = SEQ
    r = lax.broadcasted_iota(jnp.int32, (L, 1), 0)
    t = jnp.where(pl.program_id(0) == 0, r, L - 1 - r).astype(F32)
    t01 = t / (L - 1)
    fw = (2 * math.pi * t / L) * bands_ref[...]
    pre = (t01 * w1_ref[0:1, :] + _fdot(jnp.cos(fw), w1_ref[1:1 + FILTER_BANDS, :])
           + _fdot(-jnp.sin(fw), w1_ref[1 + FILTER_BANDS:, :]) + b1_ref[...])
    h = jnp.sin(freq_ref[0:1, :] * pre)
    h_ref[...] = jnp.sin(freq_ref[1:2, :] * (_fdot(h, w2_ref[...]) + b2_ref[...]))


def _filter_hidden(bands, w1, b1, freq, w2, b2):
    const = lambda s: (0, 0)
    return pl.pallas_call(
        _filter_hidden_kernel,
        out_shape=jax.ShapeDtypeStruct((2, SEQ, FILTER_HIDDEN), F32),
        grid=(2,),
        in_specs=[pl.BlockSpec((1, FILTER_BANDS), const),
                  pl.BlockSpec((1 + 2 * FILTER_BANDS, FILTER_HIDDEN), const),
                  pl.BlockSpec((1, FILTER_HIDDEN), const),
                  pl.BlockSpec((2, FILTER_HIDDEN), const),
                  pl.BlockSpec((FILTER_HIDDEN, FILTER_HIDDEN), const),
                  pl.BlockSpec((1, FILTER_HIDDEN), const)],
        out_specs=pl.BlockSpec((None, SEQ, FILTER_HIDDEN), lambda s: (s, 0, 0)),
        compiler_params=_params(1),
        name="hyena_filter_hidden",
    )(bands, w1, b1, freq, w2, b2)


def _filter_kernel(hid_ref, w3f_ref, w3b_ref, decf_ref, decb_ref, csv_ref,
                   hre_ref, him_ref, hny_ref):
    L, P, NB = SEQ, HY_P, HY_NB
    ct = hre_ref.shape[-1]
    row = lax.broadcasted_iota(jnp.int32, (L, ct), 0)
    t_asc = row.astype(F32) / (L - 1)
    t_dsc = (L - 1 - row).astype(F32) / (L - 1)
    dec_f = jnp.abs(decf_ref[...])
    dec_b = jnp.abs(decb_ref[...])
    h_asc = hid_ref[0]
    h_dsc = hid_ref[1]
    inner = (row & (P - 1)) != 0
    f_asc = _fdot(h_asc, w3f_ref[...]) * jnp.exp(-t_asc * dec_f)
    b_dsc = _fdot(h_dsc, w3b_ref[...]) * jnp.exp(-t_dsc * dec_b)
    f_dsc = _fdot(h_dsc, w3f_ref[...]) * jnp.exp(-t_dsc * dec_f)
    f_dsc = jnp.where(inner, _shift_rows(f_dsc, 1, row), 0.0)
    b_asc = _fdot(h_asc, w3b_ref[...]) * jnp.exp(-t_asc * dec_b)
    b_asc = jnp.where(inner, _shift_rows(b_asc, 1, row), 0.0)

    def blk(x, i):
        return x[i * P:(i + 1) * P, :]

    frow = lax.broadcasted_iota(jnp.int32, (P, ct), 0)
    wgt = jnp.where(frow == 0, 1.0 / NFFT, 2.0 / NFFT)
    sign = jnp.where((frow & 1) == 0, 1.0, -1.0)
    for idx in range(HY_LAGS):
        d = idx - (NB - 1)
        if d >= 1:
            kp, kn = blk(f_asc, d), blk(f_dsc, NB - d)
        elif d == 0:
            kp, kn = blk(f_asc, 0), blk(b_asc, 0)
        else:
            kp, kn = blk(b_dsc, NB + d), blk(b_asc, -d)
        ev = kp + kn
        od = kp - kn
        hre_ref[idx] = _bdot(csv_ref[:P, :], ev.astype(BF16)) * wgt
        him_ref[idx] = -_bdot(csv_ref[P:, :], od.astype(BF16)) * wgt
        hny_ref[idx:idx + 1, :] = jnp.sum(ev * sign, axis=0, keepdims=True) * (1.0 / NFFT)


def _filter_spectra(hidden, w3, decay, csv):
    ct = HY_CT
    nct = HALF // ct
    fcol = lambda o, c: (0, o * nct + c)
    bcol = lambda o, c: (0, (2 + o) * nct + c)
    hspec = pl.BlockSpec((None, HY_LAGS, HY_P, ct), lambda o, c: (o, 0, 0, c))
    return pl.pallas_call(
        _filter_kernel,
        out_shape=(jax.ShapeDtypeStruct((2, HY_LAGS, HY_P, HALF), F32),
                   jax.ShapeDtypeStruct((2, HY_LAGS, HY_P, HALF), F32),
                   jax.ShapeDtypeStruct((2, HY_LAGS, HALF), F32)),
        grid=(2, nct),
        in_specs=[
            pl.BlockSpec((2, SEQ, FILTER_HIDDEN), lambda o, c: (0, 0, 0)),
            pl.BlockSpec((FILTER_HIDDEN, ct), fcol),
            pl.BlockSpec((FILTER_HIDDEN, ct), bcol),
            pl.BlockSpec((1, ct), fcol),
            pl.BlockSpec((1, ct), bcol),
            pl.BlockSpec((2 * HY_P, HY_P), lambda o, c: (0, 0)),
        ],
        out_specs=(hspec, hspec,
                   pl.BlockSpec((None, HY_LAGS, ct), lambda o, c: (o, 0, c))),
        compiler_params=_params(2),
        name="hyena_filter_spectra",
    )(hidden, w3, w3, decay, decay, csv)


def _in_proj_kernel(x_ref, g_ref, w_ref, o_ref, *, gelu_cols):
    h = _rms(x_ref[...], g_ref[...]).astype(BF16)
    n_out = o_ref.shape[1]
    for c0 in range(0, n_out, 512):
        p = _bdot(h, w_ref[:, c0:c0 + 512])
        if c0 < gelu_cols:
            p = jax.nn.gelu(p)
        o_ref[:, c0:c0 + 512] = p.astype(BF16)


def _in_proj(x2d, g, w, gelu_cols):
    m = x2d.shape[0]
    n_out = w.shape[1]
    return pl.pallas_call(
        functools.partial(_in_proj_kernel, gelu_cols=gelu_cols),
        out_shape=jax.ShapeDtypeStruct((m, n_out), BF16),
        grid=(m // IN_TILE,),
        in_specs=[pl.BlockSpec((IN_TILE, D_MODEL), lambda i: (i, 0)),
                  pl.BlockSpec((1, D_MODEL), lambda i: (0, 0)),
                  _resident((D_MODEL, n_out), lambda i: (0, 0))],
        out_specs=pl.BlockSpec((IN_TILE, n_out), lambda i: (i, 0)),
        compiler_params=_params(1),
        name="rms_in_proj",
    )(x2d, g, w)


def _gmlp_kernel(za_ref, lng_ref, ws_ref, bs_ref, o_ref, vn_ref):
    def ln_body(i, carry):
        r = pl.ds(pl.multiple_of(i * PW_ROWS, PW_ROWS), PW_ROWS)
        v = za_ref[r, HALF:].astype(F32)
        xc = v - jnp.mean(v, axis=-1, keepdims=True)
        y = xc * lax.rsqrt(jnp.mean(xc * xc, axis=-1, keepdims=True) + LN_EPS) * lng_ref[...]
        vn_ref[r, :] = y.astype(BF16)
        return carry
    lax.fori_loop(0, SEQ // PW_ROWS, ln_body, 0, unroll=4)

    for g in range(A_GROUPS):
        cols = slice(g * GROUP_DIM, (g + 1) * GROUP_DIM)
        vg = jnp.concatenate(
            [vn_ref[n * CHUNK:(n + 1) * CHUNK, cols] for n in range(N_CHUNKS)], axis=1)
        s = _bdot(ws_ref[g], vg) + bs_ref[g]
        for n in range(N_CHUNKS):
            rows = slice(n * CHUNK, (n + 1) * CHUNK)
            u = za_ref[rows, cols].astype(F32)
            o_ref[rows, cols] = (u * s[:, n * GROUP_DIM:(n + 1) * GROUP_DIM]).astype(BF16)


def _gmlp(proj3d, ln_g, w_s, b_s):
    bsz = proj3d.shape[0]
    return pl.pallas_call(
        _gmlp_kernel,
        out_shape=jax.ShapeDtypeStruct((bsz, SEQ, HALF), BF16),
        grid=(bsz,),
        in_specs=[pl.BlockSpec((None, SEQ, 2 * HALF), lambda b: (b, 0, 0)),
                  pl.BlockSpec((1, HALF), lambda b: (0, 0)),
                  pl.BlockSpec((A_GROUPS, CHUNK, CHUNK), lambda b: (0, 0, 0)),
                  pl.BlockSpec((A_GROUPS, CHUNK, 1), lambda b: (0, 0, 0))],
        out_specs=pl.BlockSpec((None, SEQ, HALF), lambda b: (b, 0, 0)),
        scratch_shapes=[pltpu.VMEM((SEQ, HALF), BF16)],
        compiler_params=_params(1),
        name="gmlp_spatial_gating",
    )(proj3d, ln_g, w_s, b_s)


def _hyena_kernel(v_ref, g1_ref, g2_ref, wv_ref, wg1_ref, wg2_ref, skip_ref,
                  hre_ref, him_ref, hny_ref, csv_ref, csh_ref, o_ref,
                  z_ref, gate_ref, zs_ref, ys_ref):
    L, P, NB = SEQ, HY_P, HY_NB
    ct = o_ref.shape[1]
    row = lax.broadcasted_iota(jnp.int32, (L, ct), 0)
    sign = jnp.where((lax.broadcasted_iota(jnp.int32, (P, ct), 0) & 1) == 0, 1.0, -1.0)

    z_ref[...] = _conv3(v_ref[...].astype(F32), wv_ref[...], row)
    for o, (g_ref, wg_ref) in enumerate(((g1_ref, wg1_ref), (g2_ref, wg2_ref))):
        gate_ref[...] = _conv3(g_ref[...].astype(F32), wg_ref[...], row)
        z_ny = []
        for j in range(NB):
            zj = z_ref[j * P:(j + 1) * P, :]
            z_ny.append(jnp.sum(zj * sign, axis=0, keepdims=True))
            zs_ref[:, j * ct:(j + 1) * ct] = _bdot(csv_ref[...], zj.astype(BF16))

        def mix_body(c, carry):
            re = pl.ds(pl.multiple_of(c * MIX_ROWS, MIX_ROWS), MIX_ROWS)
            im = pl.ds(pl.multiple_of(P + c * MIX_ROWS, MIX_ROWS), MIX_ROWS)
            zre = [zs_ref[re, j * ct:(j + 1) * ct] for j in range(NB)]
            zmi = [zs_ref[im, j * ct:(j + 1) * ct] for j in range(NB)]
            hre = [hre_ref[o, d, re, :] for d in range(HY_LAGS)]
            him = [him_ref[o, d, re, :] for d in range(HY_LAGS)]
            for i in range(NB):
                yre = 0.0
                ymi = 0.0
                for j in range(NB):
                    d = i - j + NB - 1
                    yre = yre + (zre[j] * hre[d] + zmi[j] * him[d])
                    ymi = ymi + (zmi[j] * hre[d] - zre[j] * him[d])
                ys_ref[re, i * ct:(i + 1) * ct] = yre.astype(BF16)
                ys_ref[im, i * ct:(i + 1) * ct] = ymi.astype(BF16)
            return carry
        lax.fori_loop(0, P // MIX_ROWS, mix_body, 0)

        for i in range(NB):
            rows = slice(i * P, (i + 1) * P)
            y_ny = sum(z_ny[j] * hny_ref[o, i - j + NB - 1:i - j + NB, :] for j in range(NB))
            conv = _bdot(csh_ref[...], ys_ref[:, i * ct:(i + 1) * ct]) + sign * y_ny
            z_ref[rows, :] = gate_ref[rows, :] * (conv + z_ref[rows, :] * skip_ref[o:o + 1, :])
    o_ref[...] = z_ref[...].astype(BF16)


def _hyena(proj3d, conv_w, skip, hre, him, hny, csv, csh):
    bsz = proj3d.shape[0]
    ct = HY_CT
    nct = HALF // ct
    base = 2 * HALF // ct

    def slab(part):
        return pl.BlockSpec((None, SEQ, ct), lambda c, b: (b, 0, base + part * nct + c))

    def cw(part):
        return pl.BlockSpec((3, ct), lambda c, b: (0, part * nct + c))

    hspec = _resident((2, HY_LAGS, HY_P, ct), lambda c, b: (0, 0, 0, c))
    const = lambda c, b: (0, 0)
    return pl.pallas_call(
        _hyena_kernel,
        out_shape=jax.ShapeDtypeStruct((bsz, SEQ, HALF), BF16),
        grid=(nct, bsz),
        in_specs=[slab(0), slab(1), slab(2), cw(0), cw(1), cw(2),
                  pl.BlockSpec((2, ct), lambda c, b: (0, c)),
                  hspec, hspec,
                  pl.BlockSpec((2, HY_LAGS, ct), lambda c, b: (0, 0, c)),
                  pl.BlockSpec((2 * HY_P, HY_P), const), pl.BlockSpec((HY_P, 2 * HY_P), const)],
        out_specs=pl.BlockSpec((None, SEQ, ct), lambda c, b: (b, 0, c)),
        scratch_shapes=[pltpu.VMEM((SEQ, ct), F32), pltpu.VMEM((SEQ, ct), F32),
                        pltpu.VMEM((2 * HY_P, HY_NB * ct), F32),
                        pltpu.VMEM((2 * HY_P, HY_NB * ct), BF16)],
        compiler_params=_params(2),
        name="hyena_long_conv",
    )(proj3d, proj3d, proj3d, conv_w, conv_w, conv_w, skip, hre, him, hny, csv, csh)


def _pool_conv_kernel(p_ref, b_ref, c_ref, h_ref, cw_ref, scale_ref, dw_ref, yc_ref, yd_ref,
                      xs_ref, chs_ref, pooled_ref):
    L, R, n = SEQ, PW_ROWS, PW_ROWS + 2 * HALO
    zeros = jnp.zeros((HALO, HALF), F32)
    for ref in (xs_ref, chs_ref):
        ref[0:HALO, :] = zeros
        ref[L + HALO:L + 2 * HALO, :] = zeros

    def fill(i, carry):
        r = pl.ds(pl.multiple_of(i * R, R), R)
        rp = pl.ds(pl.multiple_of(i * R + HALO, HALO), R)
        xs_ref[rp, :] = p_ref[r, :].astype(F32)
        chs_ref[rp, :] = c_ref[r, :].astype(F32) * h_ref[r, :].astype(F32)
        return carry
    lax.fori_loop(0, L // R, fill, 0)

    for g, w in enumerate(POOL_WINDOWS):
        cols = slice(g * GROUP_DIM, (g + 1) * GROUP_DIM)
        left = w // 2
        right = w - left - 1

        def body(i, carry):
            r0 = pl.multiple_of(i * R, R)
            a = xs_ref[pl.ds(r0, n), cols]
            s = a
            k = 1
            while k < w:
                s = s + pltpu.roll(s, k, axis=0)
                k *= 2
            if right:
                s = pltpu.roll(s, n - right, axis=0)
            t = r0 + lax.broadcasted_iota(jnp.int32, (R, GROUP_DIM), 0)
            cnt = (jnp.minimum(t + right, L - 1) - jnp.maximum(t - left, 0) + 1).astype(F32)
            pooled = s[HALO:HALO + R] / cnt - a[HALO:HALO + R]
            pooled_ref[pl.ds(r0, R), cols] = pooled.astype(BF16)

            e = chs_ref[pl.ds(r0, n), cols]
            dw = dw_ref[:, cols]
            conv = (pltpu.roll(e, 1, axis=0) * dw[0:1, :] + e * dw[1:2, :]
                    + pltpu.roll(e, n - 1, axis=0) * dw[2:3, :])
            yd = b_ref[pl.ds(r0, R), cols].astype(F32) * conv[HALO:HALO + R]
            yd_ref[pl.ds(r0, R), cols] = yd.astype(BF16)
            return carry
        lax.fori_loop(0, L // R, body, 0)

        y = _bdot(pooled_ref[:, cols], cw_ref[g]) * scale_ref[:, cols]
        yc_ref[:, cols] = y.astype(BF16)


def _pool_conv(proj3d, c_w, c_scale, d_conv_w):
    bsz = proj3d.shape[0]

    def part(k):
        return pl.BlockSpec((None, SEQ, HALF), lambda b: (b, 0, k))

    out = pl.BlockSpec((None, SEQ, HALF), lambda b: (b, 0, 0))
    return pl.pallas_call(
        _pool_conv_kernel,
        out_shape=(jax.ShapeDtypeStruct((bsz, SEQ, HALF), BF16),) * 2,
        grid=(bsz,),
        in_specs=[part(0), part(1), part(2), part(3),
                  pl.BlockSpec((len(POOL_WINDOWS), GROUP_DIM, GROUP_DIM), lambda b: (0, 0, 0)),
                  pl.BlockSpec((1, HALF), lambda b: (0, 0)),
                  pl.BlockSpec((3, HALF), lambda b: (0, 0))],
        out_specs=(out, out),
        scratch_shapes=[pltpu.VMEM((SEQ + 2 * HALO, HALF), F32),
                        pltpu.VMEM((SEQ + 2 * HALO, HALF), F32),
                        pltpu.VMEM((SEQ, HALF), BF16)],
        compiler_params=_params(1),
        name="pool_short_conv",
    )(proj3d, proj3d, proj3d, proj3d, c_w, c_scale, d_conv_w)


def _out_ffn_kernel(x_ref, y0_ref, y1_ref, wo_ref, g_ref, wgu_ref, wd_ref, o_ref, a_ref):
    mix = _bdot(y0_ref[...], wo_ref[:HALF, :]) + _bdot(y1_ref[...], wo_ref[HALF:, :])
    x1 = x_ref[...] + _rms(mix, g_ref[1:2, :])
    h = _rms(x1, g_ref[2:3, :]).astype(BF16)
    for k in range(D_FF // FF_CHUNK):
        gate = _bdot(h, wgu_ref[:, k * FF_CHUNK:(k + 1) * FF_CHUNK])
        up = _bdot(h, wgu_ref[:, D_FF + k * FF_CHUNK:D_FF + (k + 1) * FF_CHUNK])
        a_ref[:, k * FF_CHUNK:(k + 1) * FF_CHUNK] = (gate * jax.nn.sigmoid(gate) * up).astype(BF16)
    f = _bdot(a_ref[...], wd_ref[...])
    o_ref[...] = x1 + _rms(f, g_ref[3:4, :])


def _out_ffn(x2d, y0, y1, w_out, g, w_gu, w_down):
    m = x2d.shape[0]
    tile = lambda n: pl.BlockSpec((ROW_TILE, n), lambda i: (i, 0))
    const = lambda i: (0, 0)
    return pl.pallas_call(
        _out_ffn_kernel,
        out_shape=jax.ShapeDtypeStruct((m, D_MODEL), F32),
        grid=(m // ROW_TILE,),
        in_specs=[tile(D_MODEL), tile(HALF), tile(HALF),
                  _resident((D_MODEL, D_MODEL), const),
                  pl.BlockSpec((4, D_MODEL), const),
                  _resident((D_MODEL, 2 * D_FF), const),
                  _resident((D_FF, D_MODEL), const)],
        out_specs=tile(D_MODEL),
        scratch_shapes=[pltpu.VMEM((ROW_TILE, D_FF), BF16)],
        compiler_params=_params(1),
        name="out_proj_ffn",
    )(x2d, y0, y1, w_out, g, w_gu, w_down)


def kernel(x, norm_g, ffn_w_gu, ffn_w_down, ab_w_in, ab_w_out, a_ln_g, a_w_s, a_b_s, b_conv_w, b_filt_w1, b_filt_b1, b_filt_freq, b_filt_w2, b_filt_b2, b_filt_w3, b_decay, b_skip, cd_w_in, cd_w_out, c_w, c_scale, d_conv_w):
    bsz, seq, d = x.shape
    assert (seq, d) == (SEQ, D_MODEL) and norm_g.shape[0] == DEPTH
    m = bsz * seq
    bands = jnp.linspace(1e-4, FILTER_BANDS - 1, FILTER_BANDS, dtype=F32)[None, :]
    csv, csh = _dft_tables()
    x2d = x.reshape(m, d)
    for i in range(DEPTH):
        j = i // 2
        g = norm_g[i]
        if i % 2 == 0:
            proj = _in_proj(x2d, g[0:1], ab_w_in[j].astype(BF16), 2 * HALF).reshape(bsz, seq, -1)
            y0 = _gmlp(proj, a_ln_g[j][None, :], a_w_s[j].astype(BF16), a_b_s[j][:, :, None])
            hidden = _filter_hidden(bands, b_filt_w1[j], b_filt_b1[j][None, :], b_filt_freq[j],
                                    b_filt_w2[j], b_filt_b2[j][None, :])
            hre, him, hny = _filter_spectra(hidden, b_filt_w3[j], b_decay[j][None, :], csv)
            y1 = _hyena(proj, b_conv_w[j], b_skip[j], hre, him, hny, csv, csh)
            w_out = ab_w_out[j]
        else:
            proj = _in_proj(x2d, g[0:1], cd_w_in[j].astype(BF16), 0).reshape(bsz, seq, -1)
            y0, y1 = _pool_conv(proj, c_w[j].astype(BF16), c_scale[j][None, :], d_conv_w[j])
            w_out = cd_w_out[j]
        x2d = _out_ffn(x2d, y0.reshape(m, HALF), y1.reshape(m, HALF), w_out.astype(BF16), g,
                       ffn_w_gu[i].astype(BF16), ffn_w_down[i].astype(BF16))
    return x2d.reshape(bsz, seq, d)
```

```python
import functools
import math

import jax
import jax.numpy as jnp
from jax import lax
from jax.experimental import pallas as pl
from jax.experimental.pallas import tpu as pltpu

F32 = jnp.float32
BF16 = jnp.bfloat16

D_MODEL = 1024
SEQ = 2048
DEPTH = 4
HALF = D_MODEL // 2
CHUNK = 128
N_CHUNKS = SEQ // CHUNK
A_GROUPS = 4
GROUP_DIM = HALF // A_GROUPS
FILTER_BANDS = 16
FILTER_HIDDEN = 64
POOL_WINDOWS = (2, 4, 8, 16)
D_FF = 2816
RMS_EPS = 1e-6
LN_EPS = 1e-5
HY_P = 512
HY_NB = SEQ // HY_P
HY_LAGS = 2 * HY_NB - 1
NFFT = 2 * HY_P
HY_NQ = 2
HY_FQ = HY_P // HY_NQ
HY_COEF = 9

ROW_TILE = 512
FFN_SUBTILES = 2
IN_TILE = 1024
IN_SUBTILES = 2
IN_CHUNK = 256
FF_CHUNK = 256
HY_CT = 256
MIX_ROWS = 16
MIX_LANES = 128
PW_ROWS = 128
HALO = 8
VMEM_LIMIT = 56 * 1024 * 1024


def _params(n_axes, vmem=VMEM_LIMIT):
    return pltpu.CompilerParams(
        dimension_semantics=("arbitrary",) * n_axes, vmem_limit_bytes=vmem)


def _resident(shape, index_map):
    return pl.BlockSpec(shape, index_map, pipeline_mode=pl.Buffered(1))


def _rms(x, g):
    return x * lax.rsqrt(jnp.mean(x * x, axis=-1, keepdims=True) + RMS_EPS) * g


def _bdot(a, b):
    return jnp.dot(a, b, preferred_element_type=F32)


def _fdot(a, b):
    return jnp.dot(a, b, preferred_element_type=F32, precision=lax.Precision.HIGHEST)


def _shift_rows(x, k, row):
    n = x.shape[0]
    y = pltpu.roll(x, k % n, axis=0)
    if k > 0:
        return jnp.where(row >= k, y, 0.0)
    return jnp.where(row < n + k, y, 0.0)


def _conv3(x, w, row):
    return (_shift_rows(x, 1, row) * w[0:1, :] + x * w[1:2, :]
            + _shift_rows(x, -1, row) * w[2:3, :])


def _dft_kernel(csv_ref, csh_ref):
    P, FQ = HY_P, HY_FQ
    f = lax.broadcasted_iota(jnp.int32, (P, 128), 0)
    lane = lax.broadcasted_iota(jnp.int32, (P, 128), 1)
    scale = 2.0 * math.pi / NFFT
    ang_b = ((f * lane) & (NFFT - 1)).astype(F32) * scale
    ang_a = ((f * (lane * 128)) & (NFFT - 1)).astype(F32) * scale
    cb, sb = jnp.cos(ang_b), jnp.sin(ang_b)
    ca, sa = jnp.cos(ang_a), jnp.sin(ang_a)
    for a in range(P // 128):
        cols = slice(a * 128, (a + 1) * 128)
        ca_a = ca[:, a:a + 1]
        sa_a = sa[:, a:a + 1]
        c = (ca_a * cb - sa_a * sb).astype(BF16)
        s = (sa_a * cb + ca_a * sb).astype(BF16)
        for q in range(HY_NQ):
            csv_ref[2 * q * FQ:(2 * q + 1) * FQ, cols] = c[q * FQ:(q + 1) * FQ]
            csv_ref[(2 * q + 1) * FQ:(2 * q + 2) * FQ, cols] = s[q * FQ:(q + 1) * FQ]
        q, off = divmod(a * 128, FQ)
        csh_ref[:, 2 * q * FQ + off:2 * q * FQ + off + 128] = c
        csh_ref[:, (2 * q + 1) * FQ + off:(2 * q + 1) * FQ + off + 128] = s


def _dft_tables():
    return pl.pallas_call(
        _dft_kernel,
        out_shape=(jax.ShapeDtypeStruct((2 * HY_P, HY_P), BF16),
                   jax.ShapeDtypeStruct((HY_P, 2 * HY_P), BF16)),
        compiler_params=_params(0),
        name="dft_tables",
    )()


def _filter_hidden_kernel(bands_ref, w1_ref, b1_ref, freq_ref, w2_ref, b2_ref, h_ref):
    L = SEQ
    r = lax.broadcasted_iota(jnp.int32, (L, 1), 0)
    t = jnp.where(pl.program_id(0) == 0, r, L - 1 - r).astype(F32)
    t01 = t / (L - 1)
    fw = (2 * math.pi * t / L) * bands_ref[...]
    pre = (t01 * w1_ref[0:1, :] + _fdot(jnp.cos(fw), w1_ref[1:1 + FILTER_BANDS, :])
           + _fdot(-jnp.sin(fw), w1_ref[1 + FILTER_BANDS:, :]) + b1_ref[...])
    h = jnp.sin(freq_ref[0:1, :] * pre)
    h_ref[...] = jnp.sin(freq_ref[1:2, :] * (_fdot(h, w2_ref[...]) + b2_ref[...]))


def _filter_hidden(bands, w1, b1, freq, w2, b2):
    const = lambda s: (0, 0)
    return pl.pallas_call(
        _filter_hidden_kernel,
        out_shape=jax.ShapeDtypeStruct((2, SEQ, FILTER_HIDDEN), F32),
        grid=(2,),
        in_specs=[pl.BlockSpec((1, FILTER_BANDS), const),
                  pl.BlockSpec((1 + 2 * FILTER_BANDS, FILTER_HIDDEN), const),
                  pl.BlockSpec((1, FILTER_HIDDEN), const),
                  pl.BlockSpec((2, FILTER_HIDDEN), const),
                  pl.BlockSpec((FILTER_HIDDEN, FILTER_HIDDEN), const),
                  pl.BlockSpec((1, FILTER_HIDDEN), const)],
        out_specs=pl.BlockSpec((None, SEQ, FILTER_HIDDEN), lambda s: (s, 0, 0)),
        compiler_params=_params(1),
        name="hyena_filter_hidden",
    )(bands, w1, b1, freq, w2, b2)


def _filter_kernel(hid_ref, w3f_ref, w3b_ref, decf_ref, decb_ref, csv_ref,
                   kre_ref, kim_ref, hny_ref):
    L, P, NB = SEQ, HY_P, HY_NB
    ct = kre_ref.shape[-1]
    row = lax.broadcasted_iota(jnp.int32, (L, ct), 0)
    t_asc = row.astype(F32) / (L - 1)
    t_dsc = (L - 1 - row).astype(F32) / (L - 1)
    dec_f = jnp.abs(decf_ref[...])
    dec_b = jnp.abs(decb_ref[...])
    h_asc = hid_ref[0]
    h_dsc = hid_ref[1]
    inner = (row & (P - 1)) != 0
    f_asc = _fdot(h_asc, w3f_ref[...]) * jnp.exp(-t_asc * dec_f)
    b_dsc = _fdot(h_dsc, w3b_ref[...]) * jnp.exp(-t_dsc * dec_b)
    f_dsc = _fdot(h_dsc, w3f_ref[...]) * jnp.exp(-t_dsc * dec_f)
    f_dsc = jnp.where(inner, _shift_rows(f_dsc, 1, row), 0.0)
    b_asc = _fdot(h_asc, w3b_ref[...]) * jnp.exp(-t_asc * dec_b)
    b_asc = jnp.where(inner, _shift_rows(b_asc, 1, row), 0.0)

    def blk(x, i):
        return x[i * P:(i + 1) * P, :]

    frow = lax.broadcasted_iota(jnp.int32, (P, ct), 0)
    wgt = jnp.where(frow == 0, 1.0 / NFFT, 2.0 / NFFT)
    sign = jnp.where((frow & 1) == 0, 1.0, -1.0)
    FQ = HY_FQ
    h = {}
    for idx in range(HY_LAGS):
        d = idx - (NB - 1)
        if d >= 1:
            kp, kn = blk(f_asc, d), blk(f_dsc, NB - d)
        elif d == 0:
            kp, kn = blk(f_asc, 0), blk(b_asc, 0)
        else:
            kp, kn = blk(b_dsc, NB + d), blk(b_asc, -d)
        ev = kp + kn
        od = kp - kn
        evb = ev.astype(BF16)
        odb = od.astype(BF16)
        re = jnp.concatenate([_bdot(csv_ref[2 * q * FQ:(2 * q + 1) * FQ, :], evb)
                              for q in range(HY_NQ)], axis=0)
        im = jnp.concatenate([_bdot(csv_ref[(2 * q + 1) * FQ:(2 * q + 2) * FQ, :], odb)
                              for q in range(HY_NQ)], axis=0)
        h[d] = (re * wgt, -im * wgt)
        hny_ref[idx:idx + 1, :] = jnp.sum(ev * sign, axis=0, keepdims=True) * (1.0 / NFFT)

    def sub(a, b):
        return (a[0] - b[0], a[1] - b[1])

    def coef3(a0, a_dn, a_up):
        return [a0, sub(a_dn, a0), sub(a_up, a0)]

    coefs = (coef3(h[0], h[1], h[-1])
             + coef3(sub(h[2], h[0]), sub(h[3], h[1]), sub(h[1], h[-1]))
             + coef3(sub(h[-2], h[0]), sub(h[-1], h[1]), sub(h[-3], h[-1])))
    for c, (re, im) in enumerate(coefs):
        kre_ref[c] = re
        kim_ref[c] = im


def _filter_spectra(hidden, w3, decay, csv):
    ct = HY_CT
    nct = HALF // ct
    fcol = lambda o, c: (0, o * nct + c)
    bcol = lambda o, c: (0, (2 + o) * nct + c)
    hspec = pl.BlockSpec((None, HY_COEF, HY_P, ct), lambda o, c: (o, 0, 0, c))
    return pl.pallas_call(
        _filter_kernel,
        out_shape=(jax.ShapeDtypeStruct((2, HY_COEF, HY_P, HALF), F32),
                   jax.ShapeDtypeStruct((2, HY_COEF, HY_P, HALF), F32),
                   jax.ShapeDtypeStruct((2, HY_LAGS, HALF), F32)),
        grid=(2, nct),
        in_specs=[
            pl.BlockSpec((2, SEQ, FILTER_HIDDEN), lambda o, c: (0, 0, 0)),
            pl.BlockSpec((FILTER_HIDDEN, ct), fcol),
            pl.BlockSpec((FILTER_HIDDEN, ct), bcol),
            pl.BlockSpec((1, ct), fcol),
            pl.BlockSpec((1, ct), bcol),
            pl.BlockSpec((2 * HY_P, HY_P), lambda o, c: (0, 0)),
        ],
        out_specs=(hspec, hspec,
                   pl.BlockSpec((None, HY_LAGS, ct), lambda o, c: (o, 0, c))),
        compiler_params=_params(2),
        name="hyena_filter_spectra",
    )(hidden, w3, w3, decay, decay, csv)


def _in_proj_kernel(x_ref, g_ref, w_ref, o_ref, h_ref, *, gelu_cols):
    sub = IN_TILE // IN_SUBTILES
    rows = [slice(s * sub, (s + 1) * sub) for s in range(IN_SUBTILES)]
    for r in rows:
        h_ref[r, :] = _rms(x_ref[r, :], g_ref[...]).astype(BF16)
    n_out = o_ref.shape[1]
    for r in rows:
        for c0 in range(0, n_out, IN_CHUNK):
            p = _bdot(h_ref[r, :], w_ref[:, c0:c0 + IN_CHUNK])
            if c0 < gelu_cols:
                p = jax.nn.gelu(p)
            o_ref[r, c0:c0 + IN_CHUNK] = p.astype(BF16)


def _in_proj(x2d, g, w, gelu_cols):
    m = x2d.shape[0]
    n_out = w.shape[1]
    return pl.pallas_call(
        functools.partial(_in_proj_kernel, gelu_cols=gelu_cols),
        out_shape=jax.ShapeDtypeStruct((m, n_out), BF16),
        grid=(m // IN_TILE,),
        in_specs=[pl.BlockSpec((IN_TILE, D_MODEL), lambda i: (i, 0)),
                  pl.BlockSpec((1, D_MODEL), lambda i: (0, 0)),
                  _resident((D_MODEL, n_out), lambda i: (0, 0))],
        out_specs=pl.BlockSpec((IN_TILE, n_out), lambda i: (i, 0)),
        scratch_shapes=[pltpu.VMEM((IN_TILE, D_MODEL), BF16)],
        compiler_params=_params(1),
        name="rms_in_proj",
    )(x2d, g, w)


def _gmlp_kernel(za_ref, lng_ref, ws_ref, bs_ref, o_ref, vn_ref):
    def ln_body(i, carry):
        r = pl.ds(pl.multiple_of(i * PW_ROWS, PW_ROWS), PW_ROWS)
        v = za_ref[r, HALF:].astype(F32)
        xc = v - jnp.mean(v, axis=-1, keepdims=True)
        y = xc * lax.rsqrt(jnp.mean(xc * xc, axis=-1, keepdims=True) + LN_EPS) * lng_ref[...]
        vn_ref[r, :] = y.astype(BF16)
        return carry
    lax.fori_loop(0, SEQ // PW_ROWS, ln_body, 0, unroll=4)

    for g in range(A_GROUPS):
        cols = slice(g * GROUP_DIM, (g + 1) * GROUP_DIM)
        vg = jnp.concatenate(
            [vn_ref[n * CHUNK:(n + 1) * CHUNK, cols] for n in range(N_CHUNKS)], axis=1)
        s = _bdot(ws_ref[g], vg) + bs_ref[g]
        for n in range(N_CHUNKS):
            rows = slice(n * CHUNK, (n + 1) * CHUNK)
            u = za_ref[rows, cols].astype(F32)
            o_ref[rows, cols] = (u * s[:, n * GROUP_DIM:(n + 1) * GROUP_DIM]).astype(BF16)


def _gmlp(proj3d, ln_g, w_s, b_s):
    bsz = proj3d.shape[0]
    return pl.pallas_call(
        _gmlp_kernel,
        out_shape=jax.ShapeDtypeStruct((bsz, SEQ, HALF), BF16),
        grid=(bsz,),
        in_specs=[pl.BlockSpec((None, SEQ, 2 * HALF), lambda b: (b, 0, 0)),
                  pl.BlockSpec((1, HALF), lambda b: (0, 0)),
                  pl.BlockSpec((A_GROUPS, CHUNK, CHUNK), lambda b: (0, 0, 0)),
                  pl.BlockSpec((A_GROUPS, CHUNK, 1), lambda b: (0, 0, 0))],
        out_specs=pl.BlockSpec((None, SEQ, HALF), lambda b: (b, 0, 0)),
        scratch_shapes=[pltpu.VMEM((SEQ, HALF), BF16)],
        compiler_params=_params(1),
        name="gmlp_spatial_gating",
    )(proj3d, ln_g, w_s, b_s)


def _cmul(z, h):
    return z[0] * h[0] + z[1] * h[1], z[1] * h[0] - z[0] * h[1]


def _cadd(a, b):
    return a[0] + b[0], a[1] + b[1]


def _mix_chunk(o, q, s, lh, ct, zs_ref, kre_ref, kim_ref, ys_ref):
    FQ = HY_FQ
    fr = slice(q * FQ + s * MIX_ROWS, q * FQ + (s + 1) * MIX_ROWS)
    rr = slice(2 * q * FQ + s * MIX_ROWS, 2 * q * FQ + (s + 1) * MIX_ROWS)
    rm = slice((2 * q + 1) * FQ + s * MIX_ROWS, (2 * q + 1) * FQ + (s + 1) * MIX_ROWS)

    def lanes(blk):
        return slice(blk * ct + lh * MIX_LANES, blk * ct + (lh + 1) * MIX_LANES)

    def coef(c):
        return kre_ref[o, c, fr, lanes(0)], kim_ref[o, c, fr, lanes(0)]

    def toep2(c, v0, v1):
        q1 = _cmul(_cadd(v0, v1), coef(c))
        q2 = _cmul(v0, coef(c + 1))
        q3 = _cmul(v1, coef(c + 2))
        return _cadd(q1, q3), _cadd(q1, q2)

    z = [(zs_ref[rr, lanes(j)], zs_ref[rm, lanes(j)]) for j in range(HY_NB)]
    p1 = toep2(0, _cadd(z[0], z[2]), _cadd(z[1], z[3]))
    p2 = toep2(3, z[0], z[1])
    p3 = toep2(6, z[2], z[3])
    y = [_cadd(p1[0], p3[0]), _cadd(p1[1], p3[1]), _cadd(p1[0], p2[0]), _cadd(p1[1], p2[1])]
    for i in range(HY_NB):
        ys_ref[rr, lanes(i)] = y[i][0].astype(BF16)
        ys_ref[rm, lanes(i)] = y[i][1].astype(BF16)


def _hyena_kernel(v_ref, g1_ref, g2_ref, wv_ref, wg1_ref, wg2_ref, skip_ref,
                  kre_ref, kim_ref, hny_ref, csv_ref, csh_ref, o_ref,
                  z_ref, gate_ref, zcat_ref, zs_ref, ys_ref, acc_ref):
    L, P, NB, FQ = SEQ, HY_P, HY_NB, HY_FQ
    ct = o_ref.shape[1]
    row = lax.broadcasted_iota(jnp.int32, (L, ct), 0)
    sign = jnp.where((lax.broadcasted_iota(jnp.int32, (P, ct), 0) & 1) == 0, 1.0, -1.0)

    def mix(o, q):
        for s in range(FQ // MIX_ROWS):
            for lh in range(ct // MIX_LANES):
                _mix_chunk(o, q, s, lh, ct, zs_ref, kre_ref, kim_ref, ys_ref)

    def chunk_rows(q):
        return slice(2 * q * FQ, 2 * (q + 1) * FQ)

    z_ref[...] = _conv3(v_ref[...].astype(F32), wv_ref[...], row)
    for o, (g_ref, wg_ref) in enumerate(((g1_ref, wg1_ref), (g2_ref, wg2_ref))):
        gate_ref[...] = _conv3(g_ref[...].astype(F32), wg_ref[...], row)
        z_ny = []
        for j in range(NB):
            zj = z_ref[j * P:(j + 1) * P, :]
            z_ny.append(jnp.sum(zj * sign, axis=0, keepdims=True))
            zcat_ref[:, j * ct:(j + 1) * ct] = zj.astype(BF16)
        for q in range(HY_NQ):
            zs_ref[chunk_rows(q), :] = _bdot(csv_ref[chunk_rows(q), :], zcat_ref[...])
        for q in range(HY_NQ - 1):
            mix(o, q)
            for i in range(NB):
                part = _bdot(csh_ref[:, chunk_rows(q)], ys_ref[chunk_rows(q), i * ct:(i + 1) * ct])
                if q == 0:
                    acc_ref[:, i * ct:(i + 1) * ct] = part
                else:
                    acc_ref[:, i * ct:(i + 1) * ct] += part
        q = HY_NQ - 1
        mix(o, q)
        for i in range(NB):
            rows = slice(i * P, (i + 1) * P)
            y_ny = sum(z_ny[j] * hny_ref[o, i - j + NB - 1:i - j + NB, :] for j in range(NB))
            conv = (acc_ref[:, i * ct:(i + 1) * ct]
                    + _bdot(csh_ref[:, chunk_rows(q)], ys_ref[chunk_rows(q), i * ct:(i + 1) * ct])
                    + sign * y_ny)
            z_ref[rows, :] = gate_ref[rows, :] * (conv + z_ref[rows, :] * skip_ref[o:o + 1, :])
    o_ref[...] = z_ref[...].astype(BF16)


def _hyena(proj3d, conv_w, skip, hre, him, hny, csv, csh):
    bsz = proj3d.shape[0]
    ct = HY_CT
    nct = HALF // ct
    base = 2 * HALF // ct

    def slab(part):
        return pl.BlockSpec((None, SEQ, ct), lambda c, b: (b, 0, base + part * nct + c))

    def cw(part):
        return pl.BlockSpec((3, ct), lambda c, b: (0, part * nct + c))

    hspec = _resident((2, HY_COEF, HY_P, ct), lambda c, b: (0, 0, 0, c))
    const = lambda c, b: (0, 0)
    return pl.pallas_call(
        _hyena_kernel,
        out_shape=jax.ShapeDtypeStruct((bsz, SEQ, HALF), BF16),
        grid=(nct, bsz),
        in_specs=[slab(0), slab(1), slab(2), cw(0), cw(1), cw(2),
                  pl.BlockSpec((2, ct), lambda c, b: (0, c)),
                  hspec, hspec,
                  pl.BlockSpec((2, HY_LAGS, ct), lambda c, b: (0, 0, c)),
                  pl.BlockSpec((2 * HY_P, HY_P), const), pl.BlockSpec((HY_P, 2 * HY_P), const)],
        out_specs=pl.BlockSpec((None, SEQ, ct), lambda c, b: (b, 0, c)),
        scratch_shapes=[pltpu.VMEM((SEQ, ct), F32), pltpu.VMEM((SEQ, ct), F32),
                        pltpu.VMEM((HY_P, HY_NB * ct), BF16),
                        pltpu.VMEM((2 * HY_P, HY_NB * ct), F32),
                        pltpu.VMEM((2 * HY_P, HY_NB * ct), BF16),
                        pltpu.VMEM((HY_P, HY_NB * ct), F32)],
        compiler_params=_params(2),
        name="hyena_long_conv",
    )(proj3d, proj3d, proj3d, conv_w, conv_w, conv_w, skip, hre, him, hny, csv, csh)


def _pool_conv_kernel(p_ref, b_ref, c_ref, h_ref, cw_ref, scale_ref, dw_ref, yc_ref, yd_ref,
                      xs_ref, chs_ref, pooled_ref):
    L, R, n = SEQ, PW_ROWS, PW_ROWS + 2 * HALO
    zeros = jnp.zeros((HALO, HALF), F32)
    for ref in (xs_ref, chs_ref):
        ref[0:HALO, :] = zeros
        ref[L + HALO:L + 2 * HALO, :] = zeros

    def fill(i, carry):
        r = pl.ds(pl.multiple_of(i * R, R), R)
        rp = pl.ds(pl.multiple_of(i * R + HALO, HALO), R)
        xs_ref[rp, :] = p_ref[r, :].astype(F32)
        chs_ref[rp, :] = c_ref[r, :].astype(F32) * h_ref[r, :].astype(F32)
        return carry
    lax.fori_loop(0, L // R, fill, 0)

    for g, w in enumerate(POOL_WINDOWS):
        cols = slice(g * GROUP_DIM, (g + 1) * GROUP_DIM)
        left = w // 2
        right = w - left - 1

        def body(i, carry):
            r0 = pl.multiple_of(i * R, R)
            a = xs_ref[pl.ds(r0, n), cols]
            s = a
            k = 1
            while k < w:
                s = s + pltpu.roll(s, k, axis=0)
                k *= 2
            if right:
                s = pltpu.roll(s, n - right, axis=0)
            t = r0 + lax.broadcasted_iota(jnp.int32, (R, GROUP_DIM), 0)
            cnt = (jnp.minimum(t + right, L - 1) - jnp.maximum(t - left, 0) + 1).astype(F32)
            pooled = s[HALO:HALO + R] / cnt - a[HALO:HALO + R]
            pooled_ref[pl.ds(r0, R), cols] = pooled.astype(BF16)

            e = chs_ref[pl.ds(r0, n), cols]
            dw = dw_ref[:, cols]
            conv = (pltpu.roll(e, 1, axis=0) * dw[0:1, :] + e * dw[1:2, :]
                    + pltpu.roll(e, n - 1, axis=0) * dw[2:3, :])
            yd = b_ref[pl.ds(r0, R), cols].astype(F32) * conv[HALO:HALO + R]
            yd_ref[pl.ds(r0, R), cols] = yd.astype(BF16)
            return carry
        lax.fori_loop(0, L // R, body, 0)

        y = _bdot(pooled_ref[:, cols], cw_ref[g]) * scale_ref[:, cols]
        yc_ref[:, cols] = y.astype(BF16)


def _pool_conv(proj3d, c_w, c_scale, d_conv_w):
    bsz = proj3d.shape[0]

    def part(k):
        return pl.BlockSpec((None, SEQ, HALF), lambda b: (b, 0, k))

    out = pl.BlockSpec((None, SEQ, HALF), lambda b: (b, 0, 0))
    return pl.pallas_call(
        _pool_conv_kernel,
        out_shape=(jax.ShapeDtypeStruct((bsz, SEQ, HALF), BF16),) * 2,
        grid=(bsz,),
        in_specs=[part(0), part(1), part(2), part(3),
                  pl.BlockSpec((len(POOL_WINDOWS), GROUP_DIM, GROUP_DIM), lambda b: (0, 0, 0)),
                  pl.BlockSpec((1, HALF), lambda b: (0, 0)),
                  pl.BlockSpec((3, HALF), lambda b: (0, 0))],
        out_specs=(out, out),
        scratch_shapes=[pltpu.VMEM((SEQ + 2 * HALO, HALF), F32),
                        pltpu.VMEM((SEQ + 2 * HALO, HALF), F32),
                        pltpu.VMEM((SEQ, HALF), BF16)],
        compiler_params=_params(1),
        name="pool_short_conv",
    )(proj3d, proj3d, proj3d, proj3d, c_w, c_scale, d_conv_w)


def _out_ffn_kernel(x_ref, y0_ref, y1_ref, wo_ref, g_ref, wgu_ref, wd_ref, o_ref, h_ref, a_ref):
    sub = ROW_TILE // FFN_SUBTILES
    rows = [slice(s * sub, (s + 1) * sub) for s in range(FFN_SUBTILES)]
    for r in rows:
        mix = _bdot(y0_ref[r, :], wo_ref[:HALF, :]) + _bdot(y1_ref[r, :], wo_ref[HALF:, :])
        o_ref[r, :] = x_ref[r, :] + _rms(mix, g_ref[1:2, :])
        h_ref[r, :] = _rms(o_ref[r, :], g_ref[2:3, :]).astype(BF16)
    for r in rows:
        for k in range(D_FF // FF_CHUNK):
            gate = _bdot(h_ref[r, :], wgu_ref[:, k * FF_CHUNK:(k + 1) * FF_CHUNK])
            up = _bdot(h_ref[r, :], wgu_ref[:, D_FF + k * FF_CHUNK:D_FF + (k + 1) * FF_CHUNK])
            a_ref[r, k * FF_CHUNK:(k + 1) * FF_CHUNK] = (
                gate * jax.nn.sigmoid(gate) * up).astype(BF16)
    for r in rows:
        f = _bdot(a_ref[r, :], wd_ref[...])
        o_ref[r, :] = o_ref[r, :] + _rms(f, g_ref[3:4, :])


def _out_ffn(x2d, y0, y1, w_out, g, w_gu, w_down):
    m = x2d.shape[0]
    tile = lambda n: pl.BlockSpec((ROW_TILE, n), lambda i: (i, 0))
    const = lambda i: (0, 0)
    return pl.pallas_call(
        _out_ffn_kernel,
        out_shape=jax.ShapeDtypeStruct((m, D_MODEL), F32),
        grid=(m // ROW_TILE,),
        in_specs=[tile(D_MODEL), tile(HALF), tile(HALF),
                  _resident((D_MODEL, D_MODEL), const),
                  pl.BlockSpec((4, D_MODEL), const),
                  _resident((D_MODEL, 2 * D_FF), const),
                  _resident((D_FF, D_MODEL), const)],
        out_specs=tile(D_MODEL),
        scratch_shapes=[pltpu.VMEM((ROW_TILE, D_MODEL), BF16),
                        pltpu.VMEM((ROW_TILE, D_FF), BF16)],
        compiler_params=_params(1),
        name="out_proj_ffn",
    )(x2d, y0, y1, w_out, g, w_gu, w_down)


def kernel(x, norm_g, ffn_w_gu, ffn_w_down, ab_w_in, ab_w_out, a_ln_g, a_w_s, a_b_s, b_conv_w, b_filt_w1, b_filt_b1, b_filt_freq, b_filt_w2, b_filt_b2, b_filt_w3, b_decay, b_skip, cd_w_in, cd_w_out, c_w, c_scale, d_conv_w):
    bsz, seq, d = x.shape
    assert (seq, d) == (SEQ, D_MODEL) and norm_g.shape[0] == DEPTH
    m = bsz * seq
    bands = jnp.linspace(1e-4, FILTER_BANDS - 1, FILTER_BANDS, dtype=F32)[None, :]
    csv, csh = _dft_tables()
    x2d = x.reshape(m, d)
    for i in range(DEPTH):
        j = i // 2
        g = norm_g[i]
        if i % 2 == 0:
            proj = _in_proj(x2d, g[0:1], ab_w_in[j].astype(BF16), 2 * HALF).reshape(bsz, seq, -1)
            y0 = _gmlp(proj, a_ln_g[j][None, :], a_w_s[j].astype(BF16), a_b_s[j][:, :, None])
            hidden = _filter_hidden(bands, b_filt_w1[j], b_filt_b1[j][None, :], b_filt_freq[j],
                                    b_filt_w2[j], b_filt_b2[j][None, :])
            hre, him, hny = _filter_spectra(hidden, b_filt_w3[j], b_decay[j][None, :], csv)
            y1 = _hyena(proj, b_conv_w[j], b_skip[j], hre, him, hny, csv, csh)
            w_out = ab_w_out[j]
        else:
            proj = _in_proj(x2d, g[0:1], cd_w_in[j].astype(BF16), 0).reshape(bsz, seq, -1)
            y0, y1 = _pool_conv(proj, c_w[j].astype(BF16), c_scale[j][None, :], d_conv_w[j])
            w_out = cd_w_out[j]
        x2d = _out_ffn(x2d, y0.reshape(m, HALF), y1.reshape(m, HALF), w_out.astype(BF16), g,
                       ffn_w_gu[i].astype(BF16), ffn_w_down[i].astype(BF16))
    return x2d.reshape(bsz, seq, d)
```

```python
import functools
import math

import jax
import jax.numpy as jnp
from jax import lax
from jax.experimental import pallas as pl
from jax.experimental.pallas import tpu as pltpu

F32 = jnp.float32
BF16 = jnp.bfloat16

D_MODEL = 1024
SEQ = 2048
DEPTH = 4
HALF = D_MODEL // 2
CHUNK = 128
N_CHUNKS = SEQ // CHUNK
A_GROUPS = 4
GROUP_DIM = HALF // A_GROUPS
FILTER_BANDS = 16
FILTER_HIDDEN = 64
POOL_WINDOWS = (2, 4, 8, 16)
D_FF = 2816
RMS_EPS = 1e-6
LN_EPS = 1e-5
HY_P = 512
HY_NB = SEQ // HY_P
HY_LAGS = 2 * HY_NB - 1
NFFT = 2 * HY_P
HY_NQ = 2
HY_FQ = HY_P // HY_NQ
HY_COEF = 9

ROW_TILE = 512
FFN_SUBTILES = 2
IN_TILE = 1024
IN_SUBTILES = 2
IN_CHUNK = 256
FF_CHUNK = 256
HY_CT = 256
MIX_ROWS = 16
MIX_LANES = 128
PW_ROWS = 128
HALO = 8
VMEM_LIMIT = 56 * 1024 * 1024


def _params(n_axes, vmem=VMEM_LIMIT, flags=None):
    return pltpu.CompilerParams(
        dimension_semantics=("arbitrary",) * n_axes, vmem_limit_bytes=vmem, flags=flags)


def _resident(shape, index_map):
    return pl.BlockSpec(shape, index_map, pipeline_mode=pl.Buffered(1))


def _rms(x, g):
    return x * lax.rsqrt(jnp.mean(x * x, axis=-1, keepdims=True) + RMS_EPS) * g


def _bdot(a, b):
    return jnp.dot(a, b, preferred_element_type=F32)


def _fdot(a, b):
    return jnp.dot(a, b, preferred_element_type=F32, precision=lax.Precision.HIGHEST)


def _shift_rows(x, k, row):
    n = x.shape[0]
    y = pltpu.roll(x, k % n, axis=0)
    if k > 0:
        return jnp.where(row >= k, y, 0.0)
    return jnp.where(row < n + k, y, 0.0)


def _conv3(x, w, row):
    return (_shift_rows(x, 1, row) * w[0:1, :] + x * w[1:2, :]
            + _shift_rows(x, -1, row) * w[2:3, :])


def _dft_kernel(csv_ref, csh_ref):
    P, FQ = HY_P, HY_FQ
    f = lax.broadcasted_iota(jnp.int32, (P, 128), 0)
    lane = lax.broadcasted_iota(jnp.int32, (P, 128), 1)
    scale = 2.0 * math.pi / NFFT
    ang_b = ((f * lane) & (NFFT - 1)).astype(F32) * scale
    ang_a = ((f * (lane * 128)) & (NFFT - 1)).astype(F32) * scale
    cb, sb = jnp.cos(ang_b), jnp.sin(ang_b)
    ca, sa = jnp.cos(ang_a), jnp.sin(ang_a)
    for a in range(P // 128):
        cols = slice(a * 128, (a + 1) * 128)
        ca_a = ca[:, a:a + 1]
        sa_a = sa[:, a:a + 1]
        c = (ca_a * cb - sa_a * sb).astype(BF16)
        s = sa_a * cb + ca_a * sb
        s_v = jnp.where(f == 0, jnp.where((lane & 1) == 0, 1.0, -1.0), s).astype(BF16)
        for q in range(HY_NQ):
            csv_ref[2 * q * FQ:(2 * q + 1) * FQ, cols] = c[q * FQ:(q + 1) * FQ]
            csv_ref[(2 * q + 1) * FQ:(2 * q + 2) * FQ, cols] = s_v[q * FQ:(q + 1) * FQ]
        s_h = s
        if a == 0:
            s_h = jnp.where(lane == 0, jnp.where((f & 1) == 0, 1.0, -1.0), s)
        q, off = divmod(a * 128, FQ)
        csh_ref[:, 2 * q * FQ + off:2 * q * FQ + off + 128] = c
        csh_ref[:, (2 * q + 1) * FQ + off:(2 * q + 1) * FQ + off + 128] = s_h.astype(BF16)


def _dft_tables():
    return pl.pallas_call(
        _dft_kernel,
        out_shape=(jax.ShapeDtypeStruct((2 * HY_P, HY_P), BF16),
                   jax.ShapeDtypeStruct((HY_P, 2 * HY_P), BF16)),
        compiler_params=_params(0),
        name="dft_tables",
    )()


def _filter_hidden_kernel(bands_ref, w1_ref, b1_ref, freq_ref, w2_ref, b2_ref, h_ref):
    L = SEQ
    r = lax.broadcasted_iota(jnp.int32, (L, 1), 0)
    t = jnp.where(pl.program_id(0) == 0, r, L - 1 - r).astype(F32)
    t01 = t / (L - 1)
    fw = (2 * math.pi * t / L) * bands_ref[...]
    pre = (t01 * w1_ref[0:1, :] + _fdot(jnp.cos(fw), w1_ref[1:1 + FILTER_BANDS, :])
           + _fdot(-jnp.sin(fw), w1_ref[1 + FILTER_BANDS:, :]) + b1_ref[...])
    h = jnp.sin(freq_ref[0:1, :] * pre)
    h_ref[...] = jnp.sin(freq_ref[1:2, :] * (_fdot(h, w2_ref[...]) + b2_ref[...]))


def _filter_hidden(bands, w1, b1, freq, w2, b2):
    const = lambda s: (0, 0)
    return pl.pallas_call(
        _filter_hidden_kernel,
        out_shape=jax.ShapeDtypeStruct((2, SEQ, FILTER_HIDDEN), F32),
        grid=(2,),
        in_specs=[pl.BlockSpec((1, FILTER_BANDS), const),
                  pl.BlockSpec((1 + 2 * FILTER_BANDS, FILTER_HIDDEN), const),
                  pl.BlockSpec((1, FILTER_HIDDEN), const),
                  pl.BlockSpec((2, FILTER_HIDDEN), const),
                  pl.BlockSpec((FILTER_HIDDEN, FILTER_HIDDEN), const),
                  pl.BlockSpec((1, FILTER_HIDDEN), const)],
        out_specs=pl.BlockSpec((None, SEQ, FILTER_HIDDEN), lambda s: (s, 0, 0)),
        compiler_params=_params(1),
        name="hyena_filter_hidden",
    )(bands, w1, b1, freq, w2, b2)


def _filter_kernel(hid_ref, w3f_ref, w3b_ref, decf_ref, decb_ref, csv_ref,
                   kre_ref, kim_ref, hny_ref):
    L, P, NB = SEQ, HY_P, HY_NB
    ct = kre_ref.shape[-1]
    row = lax.broadcasted_iota(jnp.int32, (L, ct), 0)
    t_asc = row.astype(F32) / (L - 1)
    t_dsc = (L - 1 - row).astype(F32) / (L - 1)
    dec_f = jnp.abs(decf_ref[...])
    dec_b = jnp.abs(decb_ref[...])
    h_asc = hid_ref[0]
    h_dsc = hid_ref[1]
    inner = (row & (P - 1)) != 0
    f_asc = _fdot(h_asc, w3f_ref[...]) * jnp.exp(-t_asc * dec_f)
    b_dsc = _fdot(h_dsc, w3b_ref[...]) * jnp.exp(-t_dsc * dec_b)
    f_dsc = _fdot(h_dsc, w3f_ref[...]) * jnp.exp(-t_dsc * dec_f)
    f_dsc = jnp.where(inner, _shift_rows(f_dsc, 1, row), 0.0)
    b_asc = _fdot(h_asc, w3b_ref[...]) * jnp.exp(-t_asc * dec_b)
    b_asc = jnp.where(inner, _shift_rows(b_asc, 1, row), 0.0)

    def blk(x, i):
        return x[i * P:(i + 1) * P, :]

    frow = lax.broadcasted_iota(jnp.int32, (P, ct), 0)
    wgt = jnp.where(frow == 0, 1.0 / NFFT, 2.0 / NFFT)
    sign = jnp.where((frow & 1) == 0, 1.0, -1.0)
    FQ = HY_FQ
    h = {}
    for idx in range(HY_LAGS):
        d = idx - (NB - 1)
        if d >= 1:
            kp, kn = blk(f_asc, d), blk(f_dsc, NB - d)
        elif d == 0:
            kp, kn = blk(f_asc, 0), blk(b_asc, 0)
        else:
            kp, kn = blk(b_dsc, NB + d), blk(b_asc, -d)
        ev = kp + kn
        od = kp - kn
        evb = ev.astype(BF16)
        odb = od.astype(BF16)
        re = jnp.concatenate([_bdot(csv_ref[2 * q * FQ:(2 * q + 1) * FQ, :], evb)
                              for q in range(HY_NQ)], axis=0)
        im = jnp.concatenate([_bdot(csv_ref[(2 * q + 1) * FQ:(2 * q + 2) * FQ, :], odb)
                              for q in range(HY_NQ)], axis=0)
        h[d] = (re * wgt, jnp.where(frow == 0, 0.0, -im * wgt))
        hny_ref[idx:idx + 1, :] = jnp.sum(ev * sign, axis=0, keepdims=True) * (1.0 / NFFT)

    def sub(a, b):
        return (a[0] - b[0], a[1] - b[1])

    def coef3(a0, a_dn, a_up):
        return [a0, sub(a_dn, a0), sub(a_up, a0)]

    coefs = (coef3(h[0], h[1], h[-1])
             + coef3(sub(h[2], h[0]), sub(h[3], h[1]), sub(h[1], h[-1]))
             + coef3(sub(h[-2], h[0]), sub(h[-1], h[1]), sub(h[-3], h[-1])))
    for c, (re, im) in enumerate(coefs):
        kre_ref[c] = re
        kim_ref[c] = im


def _filter_spectra(hidden, w3, decay, csv):
    ct = HY_CT
    nct = HALF // ct
    fcol = lambda o, c: (0, o * nct + c)
    bcol = lambda o, c: (0, (2 + o) * nct + c)
    hspec = pl.BlockSpec((None, HY_COEF, HY_P, ct), lambda o, c: (o, 0, 0, c))
    return pl.pallas_call(
        _filter_kernel,
        out_shape=(jax.ShapeDtypeStruct((2, HY_COEF, HY_P, HALF), F32),
                   jax.ShapeDtypeStruct((2, HY_COEF, HY_P, HALF), F32),
                   jax.ShapeDtypeStruct((2, HY_LAGS, HALF), F32)),
        grid=(2, nct),
        in_specs=[
            pl.BlockSpec((2, SEQ, FILTER_HIDDEN), lambda o, c: (0, 0, 0)),
            pl.BlockSpec((FILTER_HIDDEN, ct), fcol),
            pl.BlockSpec((FILTER_HIDDEN, ct), bcol),
            pl.BlockSpec((1, ct), fcol),
            pl.BlockSpec((1, ct), bcol),
            pl.BlockSpec((2 * HY_P, HY_P), lambda o, c: (0, 0)),
        ],
        out_specs=(hspec, hspec,
                   pl.BlockSpec((None, HY_LAGS, ct), lambda o, c: (o, 0, c))),
        compiler_params=_params(2),
        name="hyena_filter_spectra",
    )(hidden, w3, w3, decay, decay, csv)


def _in_proj_kernel(x_ref, g_ref, w_ref, o_ref, h_ref, *, gelu_cols):
    sub = IN_TILE // IN_SUBTILES
    rows = [slice(s * sub, (s + 1) * sub) for s in range(IN_SUBTILES)]
    for r in rows:
        h_ref[r, :] = _rms(x_ref[r, :], g_ref[...]).astype(BF16)
    n_out = o_ref.shape[1]
    for r in rows:
        for c0 in range(0, n_out, IN_CHUNK):
            p = _bdot(h_ref[r, :], w_ref[:, c0:c0 + IN_CHUNK])
            if c0 < gelu_cols:
                p = jax.nn.gelu(p)
            o_ref[r, c0:c0 + IN_CHUNK] = p.astype(BF16)


def _in_proj(x2d, g, w, gelu_cols):
    m = x2d.shape[0]
    n_out = w.shape[1]
    return pl.pallas_call(
        functools.partial(_in_proj_kernel, gelu_cols=gelu_cols),
        out_shape=jax.ShapeDtypeStruct((m, n_out), BF16),
        grid=(m // IN_TILE,),
        in_specs=[pl.BlockSpec((IN_TILE, D_MODEL), lambda i: (i, 0)),
                  pl.BlockSpec((1, D_MODEL), lambda i: (0, 0)),
                  _resident((D_MODEL, n_out), lambda i: (0, 0))],
        out_specs=pl.BlockSpec((IN_TILE, n_out), lambda i: (i, 0)),
        scratch_shapes=[pltpu.VMEM((IN_TILE, D_MODEL), BF16)],
        compiler_params=_params(1),
        name="rms_in_proj",
    )(x2d, g, w)


def _gmlp_kernel(za_ref, lng_ref, ws_ref, bs_ref, o_ref, vn_ref):
    def ln_body(i, carry):
        r = pl.ds(pl.multiple_of(i * PW_ROWS, PW_ROWS), PW_ROWS)
        v = za_ref[r, HALF:].astype(F32)
        xc = v - jnp.mean(v, axis=-1, keepdims=True)
        y = xc * lax.rsqrt(jnp.mean(xc * xc, axis=-1, keepdims=True) + LN_EPS) * lng_ref[...]
        vn_ref[r, :] = y.astype(BF16)
        return carry
    lax.fori_loop(0, SEQ // PW_ROWS, ln_body, 0, unroll=4)

    for g in range(A_GROUPS):
        cols = slice(g * GROUP_DIM, (g + 1) * GROUP_DIM)
        vg = jnp.concatenate(
            [vn_ref[n * CHUNK:(n + 1) * CHUNK, cols] for n in range(N_CHUNKS)], axis=1)
        s = _bdot(ws_ref[g], vg) + bs_ref[g]
        for n in range(N_CHUNKS):
            rows = slice(n * CHUNK, (n + 1) * CHUNK)
            u = za_ref[rows, cols].astype(F32)
            o_ref[rows, cols] = (u * s[:, n * GROUP_DIM:(n + 1) * GROUP_DIM]).astype(BF16)


def _gmlp(proj3d, ln_g, w_s, b_s):
    bsz = proj3d.shape[0]
    return pl.pallas_call(
        _gmlp_kernel,
        out_shape=jax.ShapeDtypeStruct((bsz, SEQ, HALF), BF16),
        grid=(bsz,),
        in_specs=[pl.BlockSpec((None, SEQ, 2 * HALF), lambda b: (b, 0, 0)),
                  pl.BlockSpec((1, HALF), lambda b: (0, 0)),
                  pl.BlockSpec((A_GROUPS, CHUNK, CHUNK), lambda b: (0, 0, 0)),
                  pl.BlockSpec((A_GROUPS, CHUNK, 1), lambda b: (0, 0, 0))],
        out_specs=pl.BlockSpec((None, SEQ, HALF), lambda b: (b, 0, 0)),
        scratch_shapes=[pltpu.VMEM((SEQ, HALF), BF16)],
        compiler_params=_params(1),
        name="gmlp_spatial_gating",
    )(proj3d, ln_g, w_s, b_s)


def _cmul(z, h):
    return z[0] * h[0] + z[1] * h[1], z[1] * h[0] - z[0] * h[1]


def _cadd(a, b):
    return a[0] + b[0], a[1] + b[1]


def _mix_tile(o, q, row0, lh, zs_ref, kre_ref, kim_ref, ys_ref):
    FQ = HY_FQ
    fr = slice(q * FQ + row0, q * FQ + row0 + MIX_ROWS)
    rr = slice(row0, row0 + MIX_ROWS)
    rm = slice(FQ + row0, FQ + row0 + MIX_ROWS)
    lanes = slice(lh * MIX_LANES, (lh + 1) * MIX_LANES)

    def coef(c):
        return kre_ref[o, c, fr, lanes], kim_ref[o, c, fr, lanes]

    def toep2(c, v0, v1):
        q1 = _cmul(_cadd(v0, v1), coef(c))
        q2 = _cmul(v0, coef(c + 1))
        q3 = _cmul(v1, coef(c + 2))
        return _cadd(q1, q3), _cadd(q1, q2)

    z = [(zs_ref[j, rr, lanes], zs_ref[j, rm, lanes]) for j in range(HY_NB)]
    p1 = toep2(0, _cadd(z[0], z[2]), _cadd(z[1], z[3]))
    p2 = toep2(3, z[0], z[1])
    p3 = toep2(6, z[2], z[3])
    y = [_cadd(p1[0], p3[0]), _cadd(p1[1], p3[1]), _cadd(p1[0], p2[0]), _cadd(p1[1], p2[1])]
    for i in range(HY_NB):
        ys_ref[i, rr, lanes] = y[i][0].astype(BF16)
        ys_ref[i, rm, lanes] = y[i][1].astype(BF16)


def _conv3_rows(src_ref, w, start):
    L, R, T = SEQ, PW_ROWS, 16
    if isinstance(start, int):
        at = lambda s, n: slice(s, s + n)
        lo, hi = max(start - T, 0), min(start + R, L - T)
    else:
        at = lambda s, n: pl.ds(pl.multiple_of(s, T), n)
        lo, hi = jnp.maximum(start - T, 0), jnp.minimum(start + R, L - T)
    x = src_ref[at(start, R), :].astype(F32)
    prev = jnp.where(start > 0, src_ref[at(lo, T), :].astype(F32), 0.0)
    nxt = jnp.where(start < L - R, src_ref[at(hi, T), :].astype(F32), 0.0)
    a = jnp.concatenate([prev, x, nxt], axis=0)
    n = R + 2 * T
    y = pltpu.roll(a, 1, axis=0) * w[0:1, :] + a * w[1:2, :] + pltpu.roll(a, n - 1, axis=0) * w[2:3, :]
    return y[T:T + R]


def _hyena_kernel(v_ref, g1_ref, g2_ref, wv_ref, wg1_ref, wg2_ref, skip_ref,
                  kre_ref, kim_ref, hny_ref, csv_ref, csh_ref, o_ref,
                  z_ref, gate_ref, zcat_ref, zs0_ref, zs1_ref, ys0_ref, ys1_ref, acc_ref):
    P, NB, FQ, R = HY_P, HY_NB, HY_FQ, PW_ROWS
    ct = o_ref.shape[1]
    steps = FQ // MIX_ROWS // NB

    def chunk(q):
        return slice(2 * q * FQ, 2 * (q + 1) * FQ)

    def mix_share(o, q, k, zs_ref, ys_ref):
        for s in range(steps):
            for lh in range(ct // MIX_LANES):
                _mix_tile(o, q, (k * steps + s) * MIX_ROWS, lh, zs_ref, kre_ref, kim_ref, ys_ref)

    for j in range(NB):
        for r in range(P // R):
            zr = _conv3_rows(v_ref, wv_ref[...], j * P + r * R)
            z_ref[j * P + r * R:j * P + (r + 1) * R, :] = zr
            zcat_ref[j, r * R:(r + 1) * R, :] = zr.astype(BF16)

    def always(k):
        return pl.program_id(1) + k >= 0

    def phase_a(o, g_ref, wg_ref):
        for j in range(NB):
            zs0_ref[j] = _bdot(csv_ref[chunk(0), :], zcat_ref[j])
            for r in range(P // R):
                start = j * P + r * R
                gate_ref[start:start + R, :] = _conv3_rows(g_ref, wg_ref[...], start)

    def phase_b(o):
        for j in range(NB):
            zs1_ref[j] = _bdot(csv_ref[chunk(1), :], zcat_ref[j])
            mix_share(o, 0, j, zs0_ref, ys0_ref)
        tile = slice(FQ, FQ + MIX_ROWS)
        first = lax.broadcasted_iota(jnp.int32, (MIX_ROWS, ct), 0) == 0
        z_ny = [zs0_ref[j, tile, :] for j in range(NB)]
        for i in range(NB):
            y_ny = sum(z_ny[j] * hny_ref[o, i - j + NB - 1:i - j + NB, :] for j in range(NB))
            ys0_ref[i, tile, :] = jnp.where(
                first, y_ny, ys0_ref[i, tile, :].astype(F32)).astype(BF16)

    def phase_c(o):
        for i in range(NB):
            acc_ref[i] = _bdot(csh_ref[:, chunk(0)], ys0_ref[i])
            mix_share(o, 1, i, zs1_ref, ys1_ref)

    def phase_d(o):
        for i in range(NB):
            rows = slice(i * P, (i + 1) * P)
            conv = acc_ref[i] + _bdot(csh_ref[:, chunk(1)], ys1_ref[i])
            z_new = gate_ref[rows, :] * (conv + z_ref[rows, :] * skip_ref[o:o + 1, :])
            if o == 0:
                z_ref[rows, :] = z_new
                zcat_ref[i] = z_new.astype(BF16)
            else:
                o_ref[rows, :] = z_new.astype(BF16)

    for o, (g_ref, wg_ref) in enumerate(((g1_ref, wg1_ref), (g2_ref, wg2_ref))):
        phases = (functools.partial(phase_a, o, g_ref, wg_ref), functools.partial(phase_b, o),
                  functools.partial(phase_c, o), functools.partial(phase_d, o))
        for k, phase in enumerate(phases):
            pl.when(always(len(phases) * o + k))(phase)


def _hyena(proj3d, conv_w, skip, hre, him, hny, csv, csh):
    bsz = proj3d.shape[0]
    ct = HY_CT
    nct = HALF // ct
    base = 2 * HALF // ct

    def slab(part):
        return pl.BlockSpec((None, SEQ, ct), lambda c, b: (b, 0, base + part * nct + c))

    def cw(part):
        return pl.BlockSpec((3, ct), lambda c, b: (0, part * nct + c))

    assert HY_NQ == 2 and HY_FQ % (MIX_ROWS * HY_NB) == 0 and ct % MIX_LANES == 0
    hspec = _resident((2, HY_COEF, HY_P, ct), lambda c, b: (0, 0, 0, c))
    const = lambda c, b: (0, 0)
    return pl.pallas_call(
        _hyena_kernel,
        out_shape=jax.ShapeDtypeStruct((bsz, SEQ, HALF), BF16),
        grid=(nct, bsz),
        in_specs=[slab(0), slab(1), slab(2), cw(0), cw(1), cw(2),
                  pl.BlockSpec((2, ct), lambda c, b: (0, c)),
                  hspec, hspec,
                  pl.BlockSpec((2, HY_LAGS, ct), lambda c, b: (0, 0, c)),
                  pl.BlockSpec((2 * HY_P, HY_P), const), pl.BlockSpec((HY_P, 2 * HY_P), const)],
        out_specs=pl.BlockSpec((None, SEQ, ct), lambda c, b: (b, 0, c)),
        scratch_shapes=[pltpu.VMEM((SEQ, ct), F32), pltpu.VMEM((SEQ, ct), F32),
                        pltpu.VMEM((HY_NB, HY_P, ct), BF16),
                        pltpu.VMEM((HY_NB, 2 * HY_FQ, ct), F32),
                        pltpu.VMEM((HY_NB, 2 * HY_FQ, ct), F32),
                        pltpu.VMEM((HY_NB, 2 * HY_FQ, ct), BF16),
                        pltpu.VMEM((HY_NB, 2 * HY_FQ, ct), BF16),
                        pltpu.VMEM((HY_NB, HY_P, ct), F32)],
        compiler_params=_params(2),
        name="hyena_long_conv",
    )(proj3d, proj3d, proj3d, conv_w, conv_w, conv_w, skip, hre, him, hny, csv, csh)


def _pool_conv_kernel(p_ref, b_ref, c_ref, h_ref, cw_ref, scale_ref, dw_ref, yc_ref, yd_ref,
                      xs_ref, chs_ref, pooled_ref):
    L, R, n = SEQ, PW_ROWS, PW_ROWS + 2 * HALO
    zeros = jnp.zeros((HALO, HALF), F32)
    for ref in (xs_ref, chs_ref):
        ref[0:HALO, :] = zeros
        ref[L + HALO:L + 2 * HALO, :] = zeros

    def fill(i, carry):
        r = pl.ds(pl.multiple_of(i * R, R), R)
        rp = pl.ds(pl.multiple_of(i * R + HALO, HALO), R)
        xs_ref[rp, :] = p_ref[r, :].astype(F32)
        chs_ref[rp, :] = c_ref[r, :].astype(F32) * h_ref[r, :].astype(F32)
        return carry
    lax.fori_loop(0, L // R, fill, 0)

    for g, w in enumerate(POOL_WINDOWS):
        cols = slice(g * GROUP_DIM, (g + 1) * GROUP_DIM)
        left = w // 2
        right = w - left - 1

        def body(i, carry):
            r0 = pl.multiple_of(i * R, R)
            a = xs_ref[pl.ds(r0, n), cols]
            s = a
            k = 1
            while k < w:
                s = s + pltpu.roll(s, k, axis=0)
                k *= 2
            if right:
                s = pltpu.roll(s, n - right, axis=0)
            t = r0 + lax.broadcasted_iota(jnp.int32, (R, GROUP_DIM), 0)
            cnt = (jnp.minimum(t + right, L - 1) - jnp.maximum(t - left, 0) + 1).astype(F32)
            pooled = s[HALO:HALO + R] / cnt - a[HALO:HALO + R]
            pooled_ref[pl.ds(r0, R), cols] = pooled.astype(BF16)

            e = chs_ref[pl.ds(r0, n), cols]
            dw = dw_ref[:, cols]
            conv = (pltpu.roll(e, 1, axis=0) * dw[0:1, :] + e * dw[1:2, :]
                    + pltpu.roll(e, n - 1, axis=0) * dw[2:3, :])
            yd = b_ref[pl.ds(r0, R), cols].astype(F32) * conv[HALO:HALO + R]
            yd_ref[pl.ds(r0, R), cols] = yd.astype(BF16)
            return carry
        lax.fori_loop(0, L // R, body, 0)

        y = _bdot(pooled_ref[:, cols], cw_ref[g]) * scale_ref[:, cols]
        yc_ref[:, cols] = y.astype(BF16)


def _pool_conv(proj3d, c_w, c_scale, d_conv_w):
    bsz = proj3d.shape[0]

    def part(k):
        return pl.BlockSpec((None, SEQ, HALF), lambda b: (b, 0, k))

    out = pl.BlockSpec((None, SEQ, HALF), lambda b: (b, 0, 0))
    return pl.pallas_call(
        _pool_conv_kernel,
        out_shape=(jax.ShapeDtypeStruct((bsz, SEQ, HALF), BF16),) * 2,
        grid=(bsz,),
        in_specs=[part(0), part(1), part(2), part(3),
                  pl.BlockSpec((len(POOL_WINDOWS), GROUP_DIM, GROUP_DIM), lambda b: (0, 0, 0)),
                  pl.BlockSpec((1, HALF), lambda b: (0, 0)),
                  pl.BlockSpec((3, HALF), lambda b: (0, 0))],
        out_specs=(out, out),
        scratch_shapes=[pltpu.VMEM((SEQ + 2 * HALO, HALF), F32),
                        pltpu.VMEM((SEQ + 2 * HALO, HALF), F32),
                        pltpu.VMEM((SEQ, HALF), BF16)],
        compiler_params=_params(1),
        name="pool_short_conv",
    )(proj3d, proj3d, proj3d, proj3d, c_w, c_scale, d_conv_w)


def _out_ffn_kernel(x_ref, y0_ref, y1_ref, wo_ref, g_ref, wgu_ref, wd_ref, o_ref, h_ref, a_ref):
    sub = ROW_TILE // FFN_SUBTILES
    rows = [slice(s * sub, (s + 1) * sub) for s in range(FFN_SUBTILES)]
    for r in rows:
        mix = _bdot(y0_ref[r, :], wo_ref[:HALF, :]) + _bdot(y1_ref[r, :], wo_ref[HALF:, :])
        o_ref[r, :] = x_ref[r, :] + _rms(mix, g_ref[1:2, :])
        h_ref[r, :] = _rms(o_ref[r, :], g_ref[2:3, :]).astype(BF16)
    for r in rows:
        for k in range(D_FF // FF_CHUNK):
            gate = _bdot(h_ref[r, :], wgu_ref[:, k * FF_CHUNK:(k + 1) * FF_CHUNK])
            up = _bdot(h_ref[r, :], wgu_ref[:, D_FF + k * FF_CHUNK:D_FF + (k + 1) * FF_CHUNK])
            a_ref[r, k * FF_CHUNK:(k + 1) * FF_CHUNK] = (
                gate * jax.nn.sigmoid(gate) * up).astype(BF16)
    for r in rows:
        f = _bdot(a_ref[r, :], wd_ref[...])
        o_ref[r, :] = o_ref[r, :] + _rms(f, g_ref[3:4, :])


def _out_ffn(x2d, y0, y1, w_out, g, w_gu, w_down):
    m = x2d.shape[0]
    tile = lambda n: pl.BlockSpec((ROW_TILE, n), lambda i: (i, 0))
    const = lambda i: (0, 0)
    return pl.pallas_call(
        _out_ffn_kernel,
        out_shape=jax.ShapeDtypeStruct((m, D_MODEL), F32),
        grid=(m // ROW_TILE,),
        in_specs=[tile(D_MODEL), tile(HALF), tile(HALF),
                  _resident((D_MODEL, D_MODEL), const),
                  pl.BlockSpec((4, D_MODEL), const),
                  _resident((D_MODEL, 2 * D_FF), const),
                  _resident((D_FF, D_MODEL), const)],
        out_specs=tile(D_MODEL),
        scratch_shapes=[pltpu.VMEM((ROW_TILE, D_MODEL), BF16),
                        pltpu.VMEM((ROW_TILE, D_FF), BF16)],
        compiler_params=_params(1),
        name="out_proj_ffn",
    )(x2d, y0, y1, w_out, g, w_gu, w_down)


def kernel(x, norm_g, ffn_w_gu, ffn_w_down, ab_w_in, ab_w_out, a_ln_g, a_w_s, a_b_s, b_conv_w, b_filt_w1, b_filt_b1, b_filt_freq, b_filt_w2, b_filt_b2, b_filt_w3, b_decay, b_skip, cd_w_in, cd_w_out, c_w, c_scale, d_conv_w):
    bsz, seq, d = x.shape
    assert (seq, d) == (SEQ, D_MODEL) and norm_g.shape[0] == DEPTH
    m = bsz * seq
    bands = jnp.linspace(1e-4, FILTER_BANDS - 1, FILTER_BANDS, dtype=F32)[None, :]
    csv, csh = _dft_tables()
    x2d = x.reshape(m, d)
    for i in range(DEPTH):
        j = i // 2
        g = norm_g[i]
        if i % 2 == 0:
            proj = _in_proj(x2d, g[0:1], ab_w_in[j].astype(BF16), 2 * HALF).reshape(bsz, seq, -1)
            y0 = _gmlp(proj, a_ln_g[j][None, :], a_w_s[j].astype(BF16), a_b_s[j][:, :, None])
            hidden = _filter_hidden(bands, b_filt_w1[j], b_filt_b1[j][None, :], b_filt_freq[j],
                                    b_filt_w2[j], b_filt_b2[j][None, :])
            hre, him, hny = _filter_spectra(hidden, b_filt_w3[j], b_decay[j][None, :], csv)
            y1 = _hyena(proj, b_conv_w[j], b_skip[j], hre, him, hny, csv, csh)
            w_out = ab_w_out[j]
        else:
            proj = _in_proj(x2d, g[0:1], cd_w_in[j].astype(BF16), 0).reshape(bsz, seq, -1)
            y0, y1 = _pool_conv(proj, c_w[j].astype(BF16), c_scale[j][None, :], d_conv_w[j])
            w_out = cd_w_out[j]
        x2d = _out_ffn(x2d, y0.reshape(m, HALF), y1.reshape(m, HALF), w_out.astype(BF16), g,
                       ffn_w_gu[i].astype(BF16), ffn_w_down[i].astype(BF16))
    return x2d.reshape(bsz, seq, d)
```

```python
import functools
import math

import jax
import jax.numpy as jnp
from jax import lax
from jax.experimental import pallas as pl
from jax.experimental.pallas import tpu as pltpu

F32 = jnp.float32
BF16 = jnp.bfloat16

D_MODEL = 1024
SEQ = 2048
DEPTH = 4
HALF = D_MODEL // 2
CHUNK = 128
N_CHUNKS = SEQ // CHUNK
A_GROUPS = 4
GROUP_DIM = HALF // A_GROUPS
FILTER_BANDS = 16
FILTER_HIDDEN = 64
POOL_WINDOWS = (2, 4, 8, 16)
D_FF = 2816
RMS_EPS = 1e-6
LN_EPS = 1e-5
HY_P = 512
HY_NB = SEQ // HY_P
HY_LAGS = 2 * HY_NB - 1
NFFT = 2 * HY_P
HY_NQ = 2
HY_FQ = HY_P // HY_NQ
HY_COEF = 9

ROW_TILE = 1024
FFN_SUBTILES = 4
FFN_VMEM_LIMIT = 60 * 1024 * 1024
IN_TILE = 2048
IN_SUBTILES = 4
IN_CHUNK = 256
FF_CHUNK = 256
HY_CT = 256
MIX_ROWS = 16
MIX_LANES = 128
PW_ROWS = 128
HALO = 8
VMEM_LIMIT = 56 * 1024 * 1024


def _params(n_axes, vmem=VMEM_LIMIT, flags=None):
    return pltpu.CompilerParams(
        dimension_semantics=("arbitrary",) * n_axes, vmem_limit_bytes=vmem, flags=flags)


def _resident(shape, index_map):
    return pl.BlockSpec(shape, index_map, pipeline_mode=pl.Buffered(1))


def _rms(x, g):
    return x * lax.rsqrt(jnp.mean(x * x, axis=-1, keepdims=True) + RMS_EPS) * g


def _bdot(a, b):
    return jnp.dot(a, b, preferred_element_type=F32)


def _fdot(a, b):
    return jnp.dot(a, b, preferred_element_type=F32, precision=lax.Precision.HIGHEST)


def _shift_rows(x, k, row):
    n = x.shape[0]
    y = pltpu.roll(x, k % n, axis=0)
    if k > 0:
        return jnp.where(row >= k, y, 0.0)
    return jnp.where(row < n + k, y, 0.0)


def _conv3(x, w, row):
    return (_shift_rows(x, 1, row) * w[0:1, :] + x * w[1:2, :]
            + _shift_rows(x, -1, row) * w[2:3, :])


def _dft_kernel(csv_ref, csh_ref):
    P, FQ = HY_P, HY_FQ
    f = lax.broadcasted_iota(jnp.int32, (P, 128), 0)
    lane = lax.broadcasted_iota(jnp.int32, (P, 128), 1)
    scale = 2.0 * math.pi / NFFT
    ang_b = ((f * lane) & (NFFT - 1)).astype(F32) * scale
    ang_a = ((f * (lane * 128)) & (NFFT - 1)).astype(F32) * scale
    cb, sb = jnp.cos(ang_b), jnp.sin(ang_b)
    ca, sa = jnp.cos(ang_a), jnp.sin(ang_a)
    for a in range(P // 128):
        cols = slice(a * 128, (a + 1) * 128)
        ca_a = ca[:, a:a + 1]
        sa_a = sa[:, a:a + 1]
        c = (ca_a * cb - sa_a * sb).astype(BF16)
        s = sa_a * cb + ca_a * sb
        s_v = jnp.where(f == 0, jnp.where((lane & 1) == 0, 1.0, -1.0), s).astype(BF16)
        for q in range(HY_NQ):
            csv_ref[2 * q * FQ:(2 * q + 1) * FQ, cols] = c[q * FQ:(q + 1) * FQ]
            csv_ref[(2 * q + 1) * FQ:(2 * q + 2) * FQ, cols] = s_v[q * FQ:(q + 1) * FQ]
        s_h = s
        if a == 0:
            s_h = jnp.where(lane == 0, jnp.where((f & 1) == 0, 1.0, -1.0), s)
        q, off = divmod(a * 128, FQ)
        csh_ref[:, 2 * q * FQ + off:2 * q * FQ + off + 128] = c
        csh_ref[:, (2 * q + 1) * FQ + off:(2 * q + 1) * FQ + off + 128] = s_h.astype(BF16)


def _dft_tables():
    return pl.pallas_call(
        _dft_kernel,
        out_shape=(jax.ShapeDtypeStruct((2 * HY_P, HY_P), BF16),
                   jax.ShapeDtypeStruct((HY_P, 2 * HY_P), BF16)),
        compiler_params=_params(0),
        name="dft_tables",
    )()


def _filter_hidden_kernel(bands_ref, w1_ref, b1_ref, freq_ref, w2_ref, b2_ref, h_ref):
    L = SEQ
    r = lax.broadcasted_iota(jnp.int32, (L, 1), 0)
    t = jnp.where(pl.program_id(0) == 0, r, L - 1 - r).astype(F32)
    t01 = t / (L - 1)
    fw = (2 * math.pi * t / L) * bands_ref[...]
    pre = (t01 * w1_ref[0:1, :] + _fdot(jnp.cos(fw), w1_ref[1:1 + FILTER_BANDS, :])
           + _fdot(-jnp.sin(fw), w1_ref[1 + FILTER_BANDS:, :]) + b1_ref[...])
    h = jnp.sin(freq_ref[0:1, :] * pre)
    h_ref[...] = jnp.sin(freq_ref[1:2, :] * (_fdot(h, w2_ref[...]) + b2_ref[...]))


def _filter_hidden(bands, w1, b1, freq, w2, b2):
    const = lambda s: (0, 0)
    return pl.pallas_call(
        _filter_hidden_kernel,
        out_shape=jax.ShapeDtypeStruct((2, SEQ, FILTER_HIDDEN), F32),
        grid=(2,),
        in_specs=[pl.BlockSpec((1, FILTER_BANDS), const),
                  pl.BlockSpec((1 + 2 * FILTER_BANDS, FILTER_HIDDEN), const),
                  pl.BlockSpec((1, FILTER_HIDDEN), const),
                  pl.BlockSpec((2, FILTER_HIDDEN), const),
                  pl.BlockSpec((FILTER_HIDDEN, FILTER_HIDDEN), const),
                  pl.BlockSpec((1, FILTER_HIDDEN), const)],
        out_specs=pl.BlockSpec((None, SEQ, FILTER_HIDDEN), lambda s: (s, 0, 0)),
        compiler_params=_params(1),
        name="hyena_filter_hidden",
    )(bands, w1, b1, freq, w2, b2)


def _filter_kernel(hid_ref, w3f_ref, w3b_ref, decf_ref, decb_ref, csv_ref,
                   kre_ref, kim_ref, hny_ref):
    L, P, NB = SEQ, HY_P, HY_NB
    ct = kre_ref.shape[-1]
    row = lax.broadcasted_iota(jnp.int32, (L, ct), 0)
    t_asc = row.astype(F32) / (L - 1)
    t_dsc = (L - 1 - row).astype(F32) / (L - 1)
    dec_f = jnp.abs(decf_ref[...])
    dec_b = jnp.abs(decb_ref[...])
    h_asc = hid_ref[0]
    h_dsc = hid_ref[1]
    inner = (row & (P - 1)) != 0
    f_asc = _fdot(h_asc, w3f_ref[...]) * jnp.exp(-t_asc * dec_f)
    b_dsc = _fdot(h_dsc, w3b_ref[...]) * jnp.exp(-t_dsc * dec_b)
    f_dsc = _fdot(h_dsc, w3f_ref[...]) * jnp.exp(-t_dsc * dec_f)
    f_dsc = jnp.where(inner, _shift_rows(f_dsc, 1, row), 0.0)
    b_asc = _fdot(h_asc, w3b_ref[...]) * jnp.exp(-t_asc * dec_b)
    b_asc = jnp.where(inner, _shift_rows(b_asc, 1, row), 0.0)

    def blk(x, i):
        return x[i * P:(i + 1) * P, :]

    frow = lax.broadcasted_iota(jnp.int32, (P, ct), 0)
    wgt = jnp.where(frow == 0, 1.0 / NFFT, 2.0 / NFFT)
    sign = jnp.where((frow & 1) == 0, 1.0, -1.0)
    FQ = HY_FQ
    h = {}
    for idx in range(HY_LAGS):
        d = idx - (NB - 1)
        if d >= 1:
            kp, kn = blk(f_asc, d), blk(f_dsc, NB - d)
        elif d == 0:
            kp, kn = blk(f_asc, 0), blk(b_asc, 0)
        else:
            kp, kn = blk(b_dsc, NB + d), blk(b_asc, -d)
        ev = kp + kn
        od = kp - kn
        evb = ev.astype(BF16)
        odb = od.astype(BF16)
        re = jnp.concatenate([_bdot(csv_ref[2 * q * FQ:(2 * q + 1) * FQ, :], evb)
                              for q in range(HY_NQ)], axis=0)
        im = jnp.concatenate([_bdot(csv_ref[(2 * q + 1) * FQ:(2 * q + 2) * FQ, :], odb)
                              for q in range(HY_NQ)], axis=0)
        h[d] = (re * wgt, jnp.where(frow == 0, 0.0, -im * wgt))
        hny_ref[idx:idx + 1, :] = jnp.sum(ev * sign, axis=0, keepdims=True) * (1.0 / NFFT)

    def sub(a, b):
        return (a[0] - b[0], a[1] - b[1])

    def coef3(a0, a_dn, a_up):
        return [a0, sub(a_dn, a0), sub(a_up, a0)]

    coefs = (coef3(h[0], h[1], h[-1])
             + coef3(sub(h[2], h[0]), sub(h[3], h[1]), sub(h[1], h[-1]))
             + coef3(sub(h[-2], h[0]), sub(h[-1], h[1]), sub(h[-3], h[-1])))
    for c, (re, im) in enumerate(coefs):
        kre_ref[c] = re
        kim_ref[c] = im


def _filter_spectra(hidden, w3, decay, csv):
    ct = HY_CT
    nct = HALF // ct
    fcol = lambda o, c: (0, o * nct + c)
    bcol = lambda o, c: (0, (2 + o) * nct + c)
    hspec = pl.BlockSpec((None, HY_COEF, HY_P, ct), lambda o, c: (o, 0, 0, c))
    return pl.pallas_call(
        _filter_kernel,
        out_shape=(jax.ShapeDtypeStruct((2, HY_COEF, HY_P, HALF), F32),
                   jax.ShapeDtypeStruct((2, HY_COEF, HY_P, HALF), F32),
                   jax.ShapeDtypeStruct((2, HY_LAGS, HALF), F32)),
        grid=(2, nct),
        in_specs=[
            pl.BlockSpec((2, SEQ, FILTER_HIDDEN), lambda o, c: (0, 0, 0)),
            pl.BlockSpec((FILTER_HIDDEN, ct), fcol),
            pl.BlockSpec((FILTER_HIDDEN, ct), bcol),
            pl.BlockSpec((1, ct), fcol),
            pl.BlockSpec((1, ct), bcol),
            pl.BlockSpec((2 * HY_P, HY_P), lambda o, c: (0, 0)),
        ],
        out_specs=(hspec, hspec,
                   pl.BlockSpec((None, HY_LAGS, ct), lambda o, c: (o, 0, c))),
        compiler_params=_params(2),
        name="hyena_filter_spectra",
    )(hidden, w3, w3, decay, decay, csv)


def _in_proj_kernel(x_ref, g_ref, w_ref, o_ref, h_ref, *, gelu_cols):
    sub = IN_TILE // IN_SUBTILES
    rows = [slice(s * sub, (s + 1) * sub) for s in range(IN_SUBTILES)]
    for r in rows:
        h_ref[r, :] = _rms(x_ref[r, :], g_ref[...]).astype(BF16)
    n_out = o_ref.shape[1]
    for r in rows:
        for c0 in range(0, n_out, IN_CHUNK):
            p = _bdot(h_ref[r, :], w_ref[:, c0:c0 + IN_CHUNK])
            if c0 < gelu_cols:
                p = jax.nn.gelu(p)
            o_ref[r, c0:c0 + IN_CHUNK] = p.astype(BF16)


def _in_proj(x2d, g, w, gelu_cols):
    m = x2d.shape[0]
    n_out = w.shape[1]
    return pl.pallas_call(
        functools.partial(_in_proj_kernel, gelu_cols=gelu_cols),
        out_shape=jax.ShapeDtypeStruct((m, n_out), BF16),
        grid=(m // IN_TILE,),
        in_specs=[pl.BlockSpec((IN_TILE, D_MODEL), lambda i: (i, 0)),
                  pl.BlockSpec((1, D_MODEL), lambda i: (0, 0)),
                  _resident((D_MODEL, n_out), lambda i: (0, 0))],
        out_specs=pl.BlockSpec((IN_TILE, n_out), lambda i: (i, 0)),
        scratch_shapes=[pltpu.VMEM((IN_TILE, D_MODEL), BF16)],
        compiler_params=_params(1),
        name="rms_in_proj",
    )(x2d, g, w)


def _gmlp_kernel(za_ref, lng_ref, ws_ref, bs_ref, o_ref, vn_ref):
    def ln_body(i, carry):
        r = pl.ds(pl.multiple_of(i * PW_ROWS, PW_ROWS), PW_ROWS)
        v = za_ref[r, HALF:].astype(F32)
        xc = v - jnp.mean(v, axis=-1, keepdims=True)
        y = xc * lax.rsqrt(jnp.mean(xc * xc, axis=-1, keepdims=True) + LN_EPS) * lng_ref[...]
        vn_ref[r, :] = y.astype(BF16)
        return carry
    lax.fori_loop(0, SEQ // PW_ROWS, ln_body, 0, unroll=4)

    for g in range(A_GROUPS):
        cols = slice(g * GROUP_DIM, (g + 1) * GROUP_DIM)
        vg = jnp.concatenate(
            [vn_ref[n * CHUNK:(n + 1) * CHUNK, cols] for n in range(N_CHUNKS)], axis=1)
        s = _bdot(ws_ref[g], vg) + bs_ref[g]
        for n in range(N_CHUNKS):
            rows = slice(n * CHUNK, (n + 1) * CHUNK)
            u = za_ref[rows, cols].astype(F32)
            o_ref[rows, cols] = (u * s[:, n * GROUP_DIM:(n + 1) * GROUP_DIM]).astype(BF16)


def _gmlp(proj3d, ln_g, w_s, b_s):
    bsz = proj3d.shape[0]
    return pl.pallas_call(
        _gmlp_kernel,
        out_shape=jax.ShapeDtypeStruct((bsz, SEQ, HALF), BF16),
        grid=(bsz,),
        in_specs=[pl.BlockSpec((None, SEQ, 2 * HALF), lambda b: (b, 0, 0)),
                  pl.BlockSpec((1, HALF), lambda b: (0, 0)),
                  pl.BlockSpec((A_GROUPS, CHUNK, CHUNK), lambda b: (0, 0, 0)),
                  pl.BlockSpec((A_GROUPS, CHUNK, 1), lambda b: (0, 0, 0))],
        out_specs=pl.BlockSpec((None, SEQ, HALF), lambda b: (b, 0, 0)),
        scratch_shapes=[pltpu.VMEM((SEQ, HALF), BF16)],
        compiler_params=_params(1),
        name="gmlp_spatial_gating",
    )(proj3d, ln_g, w_s, b_s)


def _cmul(z, h):
    return z[0] * h[0] + z[1] * h[1], z[1] * h[0] - z[0] * h[1]


def _cadd(a, b):
    return a[0] + b[0], a[1] + b[1]


def _mix_tile(o, q, row0, lh, zs_ref, kre_ref, kim_ref, ys_ref):
    FQ = HY_FQ
    fr = slice(q * FQ + row0, q * FQ + row0 + MIX_ROWS)
    rr = slice(row0, row0 + MIX_ROWS)
    rm = slice(FQ + row0, FQ + row0 + MIX_ROWS)
    lanes = slice(lh * MIX_LANES, (lh + 1) * MIX_LANES)

    def coef(c):
        return kre_ref[o, c, fr, lanes], kim_ref[o, c, fr, lanes]

    def toep2(c, v0, v1):
        q1 = _cmul(_cadd(v0, v1), coef(c))
        q2 = _cmul(v0, coef(c + 1))
        q3 = _cmul(v1, coef(c + 2))
        return _cadd(q1, q3), _cadd(q1, q2)

    z = [(zs_ref[j, rr, lanes], zs_ref[j, rm, lanes]) for j in range(HY_NB)]
    p1 = toep2(0, _cadd(z[0], z[2]), _cadd(z[1], z[3]))
    p2 = toep2(3, z[0], z[1])
    p3 = toep2(6, z[2], z[3])
    y = [_cadd(p1[0], p3[0]), _cadd(p1[1], p3[1]), _cadd(p1[0], p2[0]), _cadd(p1[1], p2[1])]
    for i in range(HY_NB):
        ys_ref[i, rr, lanes] = y[i][0].astype(BF16)
        ys_ref[i, rm, lanes] = y[i][1].astype(BF16)


def _conv3_rows(src_ref, w, start):
    L, R, T = SEQ, PW_ROWS, 16
    if isinstance(start, int):
        at = lambda s, n: slice(s, s + n)
        lo, hi = max(start - T, 0), min(start + R, L - T)
    else:
        at = lambda s, n: pl.ds(pl.multiple_of(s, T), n)
        lo, hi = jnp.maximum(start - T, 0), jnp.minimum(start + R, L - T)
    x = src_ref[at(start, R), :].astype(F32)
    prev = jnp.where(start > 0, src_ref[at(lo, T), :].astype(F32), 0.0)
    nxt = jnp.where(start < L - R, src_ref[at(hi, T), :].astype(F32), 0.0)
    a = jnp.concatenate([prev, x, nxt], axis=0)
    n = R + 2 * T
    y = pltpu.roll(a, 1, axis=0) * w[0:1, :] + a * w[1:2, :] + pltpu.roll(a, n - 1, axis=0) * w[2:3, :]
    return y[T:T + R]


def _hyena_kernel(v_ref, g1_ref, g2_ref, wv_ref, wg1_ref, wg2_ref, skip_ref,
                  kre_ref, kim_ref, hny_ref, csv_ref, csh_ref, o_ref,
                  z_ref, gate_ref, zcat_ref, zs0_ref, zs1_ref, ys0_ref, ys1_ref, acc_ref):
    P, NB, FQ, R = HY_P, HY_NB, HY_FQ, PW_ROWS
    ct = o_ref.shape[1]
    steps = FQ // MIX_ROWS // NB

    def chunk(q):
        return slice(2 * q * FQ, 2 * (q + 1) * FQ)

    def mix_share(o, q, k, zs_ref, ys_ref):
        for s in range(steps):
            for lh in range(ct // MIX_LANES):
                _mix_tile(o, q, (k * steps + s) * MIX_ROWS, lh, zs_ref, kre_ref, kim_ref, ys_ref)

    for j in range(NB):
        for r in range(P // R):
            zr = _conv3_rows(v_ref, wv_ref[...], j * P + r * R)
            z_ref[j * P + r * R:j * P + (r + 1) * R, :] = zr
            zcat_ref[j, r * R:(r + 1) * R, :] = zr.astype(BF16)

    def always(k):
        return pl.program_id(1) + k >= 0

    def phase_a(o, g_ref, wg_ref):
        for j in range(NB):
            zs0_ref[j] = _bdot(csv_ref[chunk(0), :], zcat_ref[j])
            for r in range(P // R):
                start = j * P + r * R
                gate_ref[start:start + R, :] = _conv3_rows(g_ref, wg_ref[...], start)

    def phase_b(o):
        for j in range(NB):
            zs1_ref[j] = _bdot(csv_ref[chunk(1), :], zcat_ref[j])
            mix_share(o, 0, j, zs0_ref, ys0_ref)
        tile = slice(FQ, FQ + MIX_ROWS)
        first = lax.broadcasted_iota(jnp.int32, (MIX_ROWS, ct), 0) == 0
        z_ny = [zs0_ref[j, tile, :] for j in range(NB)]
        for i in range(NB):
            y_ny = sum(z_ny[j] * hny_ref[o, i - j + NB - 1:i - j + NB, :] for j in range(NB))
            ys0_ref[i, tile, :] = jnp.where(
                first, y_ny, ys0_ref[i, tile, :].astype(F32)).astype(BF16)

    def phase_c(o):
        for i in range(NB):
            acc_ref[i] = _bdot(csh_ref[:, chunk(0)], ys0_ref[i])
            mix_share(o, 1, i, zs1_ref, ys1_ref)

    def phase_d(o):
        for i in range(NB):
            rows = slice(i * P, (i + 1) * P)
            conv = acc_ref[i] + _bdot(csh_ref[:, chunk(1)], ys1_ref[i])
            z_new = gate_ref[rows, :] * (conv + z_ref[rows, :] * skip_ref[o:o + 1, :])
            if o == 0:
                z_ref[rows, :] = z_new
                zcat_ref[i] = z_new.astype(BF16)
            else:
                o_ref[rows, :] = z_new.astype(BF16)

    for o, (g_ref, wg_ref) in enumerate(((g1_ref, wg1_ref), (g2_ref, wg2_ref))):
        phases = (functools.partial(phase_a, o, g_ref, wg_ref), functools.partial(phase_b, o),
                  functools.partial(phase_c, o), functools.partial(phase_d, o))
        for k, phase in enumerate(phases):
            pl.when(always(len(phases) * o + k))(phase)


def _hyena(proj3d, conv_w, skip, hre, him, hny, csv, csh):
    bsz = proj3d.shape[0]
    ct = HY_CT
    nct = HALF // ct
    base = 2 * HALF // ct

    def slab(part):
        return pl.BlockSpec((None, SEQ, ct), lambda c, b: (b, 0, base + part * nct + c))

    def cw(part):
        return pl.BlockSpec((3, ct), lambda c, b: (0, part * nct + c))

    assert HY_NQ == 2 and HY_FQ % (MIX_ROWS * HY_NB) == 0 and ct % MIX_LANES == 0
    hspec = _resident((2, HY_COEF, HY_P, ct), lambda c, b: (0, 0, 0, c))
    const = lambda c, b: (0, 0)
    return pl.pallas_call(
        _hyena_kernel,
        out_shape=jax.ShapeDtypeStruct((bsz, SEQ, HALF), BF16),
        grid=(nct, bsz),
        in_specs=[slab(0), slab(1), slab(2), cw(0), cw(1), cw(2),
                  pl.BlockSpec((2, ct), lambda c, b: (0, c)),
                  hspec, hspec,
                  pl.BlockSpec((2, HY_LAGS, ct), lambda c, b: (0, 0, c)),
                  pl.BlockSpec((2 * HY_P, HY_P), const), pl.BlockSpec((HY_P, 2 * HY_P), const)],
        out_specs=pl.BlockSpec((None, SEQ, ct), lambda c, b: (b, 0, c)),
        scratch_shapes=[pltpu.VMEM((SEQ, ct), F32), pltpu.VMEM((SEQ, ct), F32),
                        pltpu.VMEM((HY_NB, HY_P, ct), BF16),
                        pltpu.VMEM((HY_NB, 2 * HY_FQ, ct), F32),
                        pltpu.VMEM((HY_NB, 2 * HY_FQ, ct), F32),
                        pltpu.VMEM((HY_NB, 2 * HY_FQ, ct), BF16),
                        pltpu.VMEM((HY_NB, 2 * HY_FQ, ct), BF16),
                        pltpu.VMEM((HY_NB, HY_P, ct), F32)],
        compiler_params=_params(2),
        name="hyena_long_conv",
    )(proj3d, proj3d, proj3d, conv_w, conv_w, conv_w, skip, hre, him, hny, csv, csh)


def _pool_conv_kernel(p_ref, b_ref, c_ref, h_ref, cw_ref, scale_ref, dw_ref, yc_ref, yd_ref,
                      xs_ref, chs_ref, pooled_ref):
    L, R, n = SEQ, PW_ROWS, PW_ROWS + 2 * HALO
    zeros = jnp.zeros((HALO, HALF), F32)
    for ref in (xs_ref, chs_ref):
        ref[0:HALO, :] = zeros
        ref[L + HALO:L + 2 * HALO, :] = zeros

    def fill(i, carry):
        r = pl.ds(pl.multiple_of(i * R, R), R)
        rp = pl.ds(pl.multiple_of(i * R + HALO, HALO), R)
        xs_ref[rp, :] = p_ref[r, :].astype(F32)
        chs_ref[rp, :] = c_ref[r, :].astype(F32) * h_ref[r, :].astype(F32)
        return carry
    lax.fori_loop(0, L // R, fill, 0)

    for g, w in enumerate(POOL_WINDOWS):
        cols = slice(g * GROUP_DIM, (g + 1) * GROUP_DIM)
        left = w // 2
        right = w - left - 1

        def body(i, carry):
            r0 = pl.multiple_of(i * R, R)
            a = xs_ref[pl.ds(r0, n), cols]
            s = a
            k = 1
            while k < w:
                s = s + pltpu.roll(s, k, axis=0)
                k *= 2
            if right:
                s = pltpu.roll(s, n - right, axis=0)
            t = r0 + lax.broadcasted_iota(jnp.int32, (R, GROUP_DIM), 0)
            cnt = (jnp.minimum(t + right, L - 1) - jnp.maximum(t - left, 0) + 1).astype(F32)
            pooled = s[HALO:HALO + R] / cnt - a[HALO:HALO + R]
            pooled_ref[pl.ds(r0, R), cols] = pooled.astype(BF16)

            e = chs_ref[pl.ds(r0, n), cols]
            dw = dw_ref[:, cols]
            conv = (pltpu.roll(e, 1, axis=0) * dw[0:1, :] + e * dw[1:2, :]
                    + pltpu.roll(e, n - 1, axis=0) * dw[2:3, :])
            yd = b_ref[pl.ds(r0, R), cols].astype(F32) * conv[HALO:HALO + R]
            yd_ref[pl.ds(r0, R), cols] = yd.astype(BF16)
            return carry
        lax.fori_loop(0, L // R, body, 0)

        y = _bdot(pooled_ref[:, cols], cw_ref[g]) * scale_ref[:, cols]
        yc_ref[:, cols] = y.astype(BF16)


def _pool_conv(proj3d, c_w, c_scale, d_conv_w):
    bsz = proj3d.shape[0]

    def part(k):
        return pl.BlockSpec((None, SEQ, HALF), lambda b: (b, 0, k))

    out = pl.BlockSpec((None, SEQ, HALF), lambda b: (b, 0, 0))
    return pl.pallas_call(
        _pool_conv_kernel,
        out_shape=(jax.ShapeDtypeStruct((bsz, SEQ, HALF), BF16),) * 2,
        grid=(bsz,),
        in_specs=[part(0), part(1), part(2), part(3),
                  pl.BlockSpec((len(POOL_WINDOWS), GROUP_DIM, GROUP_DIM), lambda b: (0, 0, 0)),
                  pl.BlockSpec((1, HALF), lambda b: (0, 0)),
                  pl.BlockSpec((3, HALF), lambda b: (0, 0))],
        out_specs=(out, out),
        scratch_shapes=[pltpu.VMEM((SEQ + 2 * HALO, HALF), F32),
                        pltpu.VMEM((SEQ + 2 * HALO, HALF), F32),
                        pltpu.VMEM((SEQ, HALF), BF16)],
        compiler_params=_params(1),
        name="pool_short_conv",
    )(proj3d, proj3d, proj3d, proj3d, c_w, c_scale, d_conv_w)


def _out_ffn_kernel(x_ref, y0_ref, y1_ref, wo_ref, g_ref, wgu_ref, wd_ref, o_ref, h_ref, a_ref):
    sub = ROW_TILE // FFN_SUBTILES
    rows = [slice(s * sub, (s + 1) * sub) for s in range(FFN_SUBTILES)]
    for r in rows:
        mix = _bdot(y0_ref[r, :], wo_ref[:HALF, :]) + _bdot(y1_ref[r, :], wo_ref[HALF:, :])
        o_ref[r, :] = x_ref[r, :] + _rms(mix, g_ref[1:2, :])
        h_ref[r, :] = _rms(o_ref[r, :], g_ref[2:3, :]).astype(BF16)
    for r in rows:
        for k in range(D_FF // FF_CHUNK):
            gate = _bdot(h_ref[r, :], wgu_ref[:, k * FF_CHUNK:(k + 1) * FF_CHUNK])
            up = _bdot(h_ref[r, :], wgu_ref[:, D_FF + k * FF_CHUNK:D_FF + (k + 1) * FF_CHUNK])
            a_ref[r, k * FF_CHUNK:(k + 1) * FF_CHUNK] = (
                gate * jax.nn.sigmoid(gate) * up).astype(BF16)
    for r in rows:
        f = _bdot(a_ref[r, :], wd_ref[...])
        o_ref[r, :] = o_ref[r, :] + _rms(f, g_ref[3:4, :])


def _out_ffn(x2d, y0, y1, w_out, g, w_gu, w_down):
    m = x2d.shape[0]
    tile = lambda n: pl.BlockSpec((ROW_TILE, n), lambda i: (i, 0))
    const = lambda i: (0, 0)
    return pl.pallas_call(
        _out_ffn_kernel,
        out_shape=jax.ShapeDtypeStruct((m, D_MODEL), F32),
        grid=(m // ROW_TILE,),
        in_specs=[tile(D_MODEL), tile(HALF), tile(HALF),
                  _resident((D_MODEL, D_MODEL), const),
                  pl.BlockSpec((4, D_MODEL), const),
                  _resident((D_MODEL, 2 * D_FF), const),
                  _resident((D_FF, D_MODEL), const)],
        out_specs=tile(D_MODEL),
        scratch_shapes=[pltpu.VMEM((ROW_TILE, D_MODEL), BF16),
                        pltpu.VMEM((ROW_TILE, D_FF), BF16)],
        compiler_params=_params(1, FFN_VMEM_LIMIT),
        name="out_proj_ffn",
    )(x2d, y0, y1, w_out, g, w_gu, w_down)


def kernel(x, norm_g, ffn_w_gu, ffn_w_down, ab_w_in, ab_w_out, a_ln_g, a_w_s, a_b_s, b_conv_w, b_filt_w1, b_filt_b1, b_filt_freq, b_filt_w2, b_filt_b2, b_filt_w3, b_decay, b_skip, cd_w_in, cd_w_out, c_w, c_scale, d_conv_w):
    bsz, seq, d = x.shape
    assert (seq, d) == (SEQ, D_MODEL) and norm_g.shape[0] == DEPTH
    m = bsz * seq
    bands = jnp.linspace(1e-4, FILTER_BANDS - 1, FILTER_BANDS, dtype=F32)[None, :]
    csv, csh = _dft_tables()
    x2d = x.reshape(m, d)
    for i in range(DEPTH):
        j = i // 2
        g = norm_g[i]
        if i % 2 == 0:
            proj = _in_proj(x2d, g[0:1], ab_w_in[j].astype(BF16), 2 * HALF).reshape(bsz, seq, -1)
            y0 = _gmlp(proj, a_ln_g[j][None, :], a_w_s[j].astype(BF16), a_b_s[j][:, :, None])
            hidden = _filter_hidden(bands, b_filt_w1[j], b_filt_b1[j][None, :], b_filt_freq[j],
                                    b_filt_w2[j], b_filt_b2[j][None, :])
            hre, him, hny = _filter_spectra(hidden, b_filt_w3[j], b_decay[j][None, :], csv)
            y1 = _hyena(proj, b_conv_w[j], b_skip[j], hre, him, hny, csv, csh)
            w_out = ab_w_out[j]
        else:
            proj = _in_proj(x2d, g[0:1], cd_w_in[j].astype(BF16), 0).reshape(bsz, seq, -1)
            y0, y1 = _pool_conv(proj, c_w[j].astype(BF16), c_scale[j][None, :], d_conv_w[j])
            w_out = cd_w_out[j]
        x2d = _out_ffn(x2d, y0.reshape(m, HALF), y1.reshape(m, HALF), w_out.astype(BF16), g,
                       ffn_w_gu[i].astype(BF16), ffn_w_down[i].astype(BF16))
    return x2d.reshape(bsz, seq, d)
```

```python
import functools
import math

import jax
import jax.numpy as jnp
from jax import lax
from jax.experimental import pallas as pl
from jax.experimental.pallas import tpu as pltpu

F32 = jnp.float32
BF16 = jnp.bfloat16

D_MODEL = 1024
SEQ = 2048
DEPTH = 4
HALF = D_MODEL // 2
CHUNK = 128
N_CHUNKS = SEQ // CHUNK
A_GROUPS = 4
GROUP_DIM = HALF // A_GROUPS
FILTER_BANDS = 16
FILTER_HIDDEN = 64
POOL_WINDOWS = (2, 4, 8, 16)
D_FF = 2816
RMS_EPS = 1e-6
LN_EPS = 1e-5
HY_P = 512
HY_NB = SEQ // HY_P
HY_LAGS = 2 * HY_NB - 1
NFFT = 2 * HY_P
HY_NQ = 2
HY_FQ = HY_P // HY_NQ
HY_COEF = 9

ROW_TILE = 1024
FFN_SUBTILES = 4
FFN_VMEM_LIMIT = 60 * 1024 * 1024
IN_TILE = 2048
IN_SUBTILES = 4
IN_CHUNK = 256
FF_CHUNK = 256
HY_CT = 256
MIX_ROWS = 16
MIX_LANES = 128
PW_ROWS = 128
HALO = 8
POOL_TILE = 256
POOL_K = 512
POOL_EDGE = 16
VMEM_LIMIT = 56 * 1024 * 1024


def _params(n_axes, vmem=VMEM_LIMIT, flags=None):
    return pltpu.CompilerParams(
        dimension_semantics=("arbitrary",) * n_axes, vmem_limit_bytes=vmem, flags=flags)


def _resident(shape, index_map):
    return pl.BlockSpec(shape, index_map, pipeline_mode=pl.Buffered(1))


def _rms(x, g):
    return x * lax.rsqrt(jnp.mean(x * x, axis=-1, keepdims=True) + RMS_EPS) * g


def _bdot(a, b):
    return jnp.dot(a, b, preferred_element_type=F32)


def _fdot(a, b):
    return jnp.dot(a, b, preferred_element_type=F32, precision=lax.Precision.HIGHEST)


def _shift_rows(x, k, row):
    n = x.shape[0]
    y = pltpu.roll(x, k % n, axis=0)
    if k > 0:
        return jnp.where(row >= k, y, 0.0)
    return jnp.where(row < n + k, y, 0.0)


def _conv3(x, w, row):
    return (_shift_rows(x, 1, row) * w[0:1, :] + x * w[1:2, :]
            + _shift_rows(x, -1, row) * w[2:3, :])


def _dft_kernel(csv_ref, csh_ref):
    P, FQ = HY_P, HY_FQ
    f = lax.broadcasted_iota(jnp.int32, (P, 128), 0)
    lane = lax.broadcasted_iota(jnp.int32, (P, 128), 1)
    scale = 2.0 * math.pi / NFFT
    ang_b = ((f * lane) & (NFFT - 1)).astype(F32) * scale
    ang_a = ((f * (lane * 128)) & (NFFT - 1)).astype(F32) * scale
    cb, sb = jnp.cos(ang_b), jnp.sin(ang_b)
    ca, sa = jnp.cos(ang_a), jnp.sin(ang_a)
    for a in range(P // 128):
        cols = slice(a * 128, (a + 1) * 128)
        ca_a = ca[:, a:a + 1]
        sa_a = sa[:, a:a + 1]
        c = (ca_a * cb - sa_a * sb).astype(BF16)
        s = sa_a * cb + ca_a * sb
        s_v = jnp.where(f == 0, jnp.where((lane & 1) == 0, 1.0, -1.0), s).astype(BF16)
        for q in range(HY_NQ):
            csv_ref[2 * q * FQ:(2 * q + 1) * FQ, cols] = c[q * FQ:(q + 1) * FQ]
            csv_ref[(2 * q + 1) * FQ:(2 * q + 2) * FQ, cols] = s_v[q * FQ:(q + 1) * FQ]
        s_h = s
        if a == 0:
            s_h = jnp.where(lane == 0, jnp.where((f & 1) == 0, 1.0, -1.0), s)
        q, off = divmod(a * 128, FQ)
        csh_ref[:, 2 * q * FQ + off:2 * q * FQ + off + 128] = c
        csh_ref[:, (2 * q + 1) * FQ + off:(2 * q + 1) * FQ + off + 128] = s_h.astype(BF16)


def _dft_tables():
    return pl.pallas_call(
        _dft_kernel,
        out_shape=(jax.ShapeDtypeStruct((2 * HY_P, HY_P), BF16),
                   jax.ShapeDtypeStruct((HY_P, 2 * HY_P), BF16)),
        compiler_params=_params(0),
        name="dft_tables",
    )()


def _filter_hidden_kernel(bands_ref, w1_ref, b1_ref, freq_ref, w2_ref, b2_ref, h_ref):
    L = SEQ
    r = lax.broadcasted_iota(jnp.int32, (L, 1), 0)
    t = jnp.where(pl.program_id(0) == 0, r, L - 1 - r).astype(F32)
    t01 = t / (L - 1)
    fw = (2 * math.pi * t / L) * bands_ref[...]
    pre = (t01 * w1_ref[0:1, :] + _fdot(jnp.cos(fw), w1_ref[1:1 + FILTER_BANDS, :])
           + _fdot(-jnp.sin(fw), w1_ref[1 + FILTER_BANDS:, :]) + b1_ref[...])
    h = jnp.sin(freq_ref[0:1, :] * pre)
    h_ref[...] = jnp.sin(freq_ref[1:2, :] * (_fdot(h, w2_ref[...]) + b2_ref[...]))


def _filter_hidden(bands, w1, b1, freq, w2, b2):
    const = lambda s: (0, 0)
    return pl.pallas_call(
        _filter_hidden_kernel,
        out_shape=jax.ShapeDtypeStruct((2, SEQ, FILTER_HIDDEN), F32),
        grid=(2,),
        in_specs=[pl.BlockSpec((1, FILTER_BANDS), const),
                  pl.BlockSpec((1 + 2 * FILTER_BANDS, FILTER_HIDDEN), const),
                  pl.BlockSpec((1, FILTER_HIDDEN), const),
                  pl.BlockSpec((2, FILTER_HIDDEN), const),
                  pl.BlockSpec((FILTER_HIDDEN, FILTER_HIDDEN), const),
                  pl.BlockSpec((1, FILTER_HIDDEN), const)],
        out_specs=pl.BlockSpec((None, SEQ, FILTER_HIDDEN), lambda s: (s, 0, 0)),
        compiler_params=_params(1),
        name="hyena_filter_hidden",
    )(bands, w1, b1, freq, w2, b2)


def _filter_kernel(hid_ref, w3f_ref, w3b_ref, decf_ref, decb_ref, csv_ref,
                   kre_ref, kim_ref, hny_ref):
    L, P, NB = SEQ, HY_P, HY_NB
    ct = kre_ref.shape[-1]
    row = lax.broadcasted_iota(jnp.int32, (L, ct), 0)
    t_asc = row.astype(F32) / (L - 1)
    t_dsc = (L - 1 - row).astype(F32) / (L - 1)
    dec_f = jnp.abs(decf_ref[...])
    dec_b = jnp.abs(decb_ref[...])
    h_asc = hid_ref[0]
    h_dsc = hid_ref[1]
    inner = (row & (P - 1)) != 0
    f_asc = _fdot(h_asc, w3f_ref[...]) * jnp.exp(-t_asc * dec_f)
    b_dsc = _fdot(h_dsc, w3b_ref[...]) * jnp.exp(-t_dsc * dec_b)
    f_dsc = _fdot(h_dsc, w3f_ref[...]) * jnp.exp(-t_dsc * dec_f)
    f_dsc = jnp.where(inner, _shift_rows(f_dsc, 1, row), 0.0)
    b_asc = _fdot(h_asc, w3b_ref[...]) * jnp.exp(-t_asc * dec_b)
    b_asc = jnp.where(inner, _shift_rows(b_asc, 1, row), 0.0)

    def blk(x, i):
        return x[i * P:(i + 1) * P, :]

    frow = lax.broadcasted_iota(jnp.int32, (P, ct), 0)
    wgt = jnp.where(frow == 0, 1.0 / NFFT, 2.0 / NFFT)
    sign = jnp.where((frow & 1) == 0, 1.0, -1.0)
    FQ = HY_FQ
    h = {}
    for idx in range(HY_LAGS):
        d = idx - (NB - 1)
        if d >= 1:
            kp, kn = blk(f_asc, d), blk(f_dsc, NB - d)
        elif d == 0:
            kp, kn = blk(f_asc, 0), blk(b_asc, 0)
        else:
            kp, kn = blk(b_dsc, NB + d), blk(b_asc, -d)
        ev = kp + kn
        od = kp - kn
        evb = ev.astype(BF16)
        odb = od.astype(BF16)
        re = jnp.concatenate([_bdot(csv_ref[2 * q * FQ:(2 * q + 1) * FQ, :], evb)
                              for q in range(HY_NQ)], axis=0)
        im = jnp.concatenate([_bdot(csv_ref[(2 * q + 1) * FQ:(2 * q + 2) * FQ, :], odb)
                              for q in range(HY_NQ)], axis=0)
        h[d] = (re * wgt, jnp.where(frow == 0, 0.0, -im * wgt))
        hny_ref[idx:idx + 1, :] = jnp.sum(ev * sign, axis=0, keepdims=True) * (1.0 / NFFT)

    def sub(a, b):
        return (a[0] - b[0], a[1] - b[1])

    def coef3(a0, a_dn, a_up):
        return [a0, sub(a_dn, a0), sub(a_up, a0)]

    coefs = (coef3(h[0], h[1], h[-1])
             + coef3(sub(h[2], h[0]), sub(h[3], h[1]), sub(h[1], h[-1]))
             + coef3(sub(h[-2], h[0]), sub(h[-1], h[1]), sub(h[-3], h[-1])))
    for c, (re, im) in enumerate(coefs):
        kre_ref[c] = re
        kim_ref[c] = im


def _filter_spectra(hidden, w3, decay, csv):
    ct = HY_CT
    nct = HALF // ct
    fcol = lambda o, c: (0, o * nct + c)
    bcol = lambda o, c: (0, (2 + o) * nct + c)
    hspec = pl.BlockSpec((None, HY_COEF, HY_P, ct), lambda o, c: (o, 0, 0, c))
    return pl.pallas_call(
        _filter_kernel,
        out_shape=(jax.ShapeDtypeStruct((2, HY_COEF, HY_P, HALF), F32),
                   jax.ShapeDtypeStruct((2, HY_COEF, HY_P, HALF), F32),
                   jax.ShapeDtypeStruct((2, HY_LAGS, HALF), F32)),
        grid=(2, nct),
        in_specs=[
            pl.BlockSpec((2, SEQ, FILTER_HIDDEN), lambda o, c: (0, 0, 0)),
            pl.BlockSpec((FILTER_HIDDEN, ct), fcol),
            pl.BlockSpec((FILTER_HIDDEN, ct), bcol),
            pl.BlockSpec((1, ct), fcol),
            pl.BlockSpec((1, ct), bcol),
            pl.BlockSpec((2 * HY_P, HY_P), lambda o, c: (0, 0)),
        ],
        out_specs=(hspec, hspec,
                   pl.BlockSpec((None, HY_LAGS, ct), lambda o, c: (o, 0, c))),
        compiler_params=_params(2),
        name="hyena_filter_spectra",
    )(hidden, w3, w3, decay, decay, csv)


def _in_proj_kernel(x_ref, g_ref, w_ref, o_ref, h_ref, *, gelu_cols):
    sub = IN_TILE // IN_SUBTILES
    rows = [slice(s * sub, (s + 1) * sub) for s in range(IN_SUBTILES)]
    for r in rows:
        h_ref[r, :] = _rms(x_ref[r, :], g_ref[...]).astype(BF16)
    n_out = o_ref.shape[1]
    gelu_chunks = list(range(0, gelu_cols, IN_CHUNK))
    plain_chunks = list(range(gelu_cols, n_out, IN_CHUNK))
    order = [c for pair in zip(gelu_chunks, plain_chunks) for c in pair]
    order += gelu_chunks[len(plain_chunks):] + plain_chunks[len(gelu_chunks):]
    for r in rows:
        for c0 in order:
            p = _bdot(h_ref[r, :], w_ref[:, c0:c0 + IN_CHUNK])
            if c0 < gelu_cols:
                p = jax.nn.gelu(p)
            o_ref[r, c0:c0 + IN_CHUNK] = p.astype(BF16)


def _in_proj(x2d, g, w, gelu_cols):
    m = x2d.shape[0]
    n_out = w.shape[1]
    return pl.pallas_call(
        functools.partial(_in_proj_kernel, gelu_cols=gelu_cols),
        out_shape=jax.ShapeDtypeStruct((m, n_out), BF16),
        grid=(m // IN_TILE,),
        in_specs=[pl.BlockSpec((IN_TILE, D_MODEL), lambda i: (i, 0)),
                  pl.BlockSpec((1, D_MODEL), lambda i: (0, 0)),
                  _resident((D_MODEL, n_out), lambda i: (0, 0))],
        out_specs=pl.BlockSpec((IN_TILE, n_out), lambda i: (i, 0)),
        scratch_shapes=[pltpu.VMEM((IN_TILE, D_MODEL), BF16)],
        compiler_params=_params(1),
        name="rms_in_proj",
    )(x2d, g, w)


def _gmlp_kernel(za_ref, lng_ref, ws_ref, bs_ref, o_ref, vn_ref):
    def ln_body(i, carry):
        r = pl.ds(pl.multiple_of(i * PW_ROWS, PW_ROWS), PW_ROWS)
        v = za_ref[r, HALF:].astype(F32)
        xc = v - jnp.mean(v, axis=-1, keepdims=True)
        y = xc * lax.rsqrt(jnp.mean(xc * xc, axis=-1, keepdims=True) + LN_EPS) * lng_ref[...]
        vn_ref[r, :] = y.astype(BF16)
        return carry
    lax.fori_loop(0, SEQ // PW_ROWS, ln_body, 0, unroll=4)

    for g in range(A_GROUPS):
        cols = slice(g * GROUP_DIM, (g + 1) * GROUP_DIM)
        vg = jnp.concatenate(
            [vn_ref[n * CHUNK:(n + 1) * CHUNK, cols] for n in range(N_CHUNKS)], axis=1)
        s = _bdot(ws_ref[g], vg) + bs_ref[g]
        for n in range(N_CHUNKS):
            rows = slice(n * CHUNK, (n + 1) * CHUNK)
            u = za_ref[rows, cols].astype(F32)
            o_ref[rows, cols] = (u * s[:, n * GROUP_DIM:(n + 1) * GROUP_DIM]).astype(BF16)


def _gmlp(proj3d, ln_g, w_s, b_s):
    bsz = proj3d.shape[0]
    return pl.pallas_call(
        _gmlp_kernel,
        out_shape=jax.ShapeDtypeStruct((bsz, SEQ, HALF), BF16),
        grid=(bsz,),
        in_specs=[pl.BlockSpec((None, SEQ, 2 * HALF), lambda b: (b, 0, 0)),
                  pl.BlockSpec((1, HALF), lambda b: (0, 0)),
                  pl.BlockSpec((A_GROUPS, CHUNK, CHUNK), lambda b: (0, 0, 0)),
                  pl.BlockSpec((A_GROUPS, CHUNK, 1), lambda b: (0, 0, 0))],
        out_specs=pl.BlockSpec((None, SEQ, HALF), lambda b: (b, 0, 0)),
        scratch_shapes=[pltpu.VMEM((SEQ, HALF), BF16)],
        compiler_params=_params(1),
        name="gmlp_spatial_gating",
    )(proj3d, ln_g, w_s, b_s)


def _cmul(z, h):
    return z[0] * h[0] + z[1] * h[1], z[1] * h[0] - z[0] * h[1]


def _cadd(a, b):
    return a[0] + b[0], a[1] + b[1]


def _mix_tile(o, q, row0, lh, zs_ref, kre_ref, kim_ref, ys_ref):
    FQ = HY_FQ
    fr = slice(q * FQ + row0, q * FQ + row0 + MIX_ROWS)
    rr = slice(row0, row0 + MIX_ROWS)
    rm = slice(FQ + row0, FQ + row0 + MIX_ROWS)
    lanes = slice(lh * MIX_LANES, (lh + 1) * MIX_LANES)

    def coef(c):
        return kre_ref[o, c, fr, lanes], kim_ref[o, c, fr, lanes]

    def toep2(c, v0, v1):
        q1 = _cmul(_cadd(v0, v1), coef(c))
        q2 = _cmul(v0, coef(c + 1))
        q3 = _cmul(v1, coef(c + 2))
        return _cadd(q1, q3), _cadd(q1, q2)

    z = [(zs_ref[j, rr, lanes], zs_ref[j, rm, lanes]) for j in range(HY_NB)]
    p1 = toep2(0, _cadd(z[0], z[2]), _cadd(z[1], z[3]))
    p2 = toep2(3, z[0], z[1])
    p3 = toep2(6, z[2], z[3])
    y = [_cadd(p1[0], p3[0]), _cadd(p1[1], p3[1]), _cadd(p1[0], p2[0]), _cadd(p1[1], p2[1])]
    for i in range(HY_NB):
        ys_ref[i, rr, lanes] = y[i][0].astype(BF16)
        ys_ref[i, rm, lanes] = y[i][1].astype(BF16)


def _conv3_rows(src_ref, w, start):
    L, R, T = SEQ, PW_ROWS, 16
    if isinstance(start, int):
        at = lambda s, n: slice(s, s + n)
        lo, hi = max(start - T, 0), min(start + R, L - T)
    else:
        at = lambda s, n: pl.ds(pl.multiple_of(s, T), n)
        lo, hi = jnp.maximum(start - T, 0), jnp.minimum(start + R, L - T)
    x = src_ref[at(start, R), :].astype(F32)
    prev = jnp.where(start > 0, src_ref[at(lo, T), :].astype(F32), 0.0)
    nxt = jnp.where(start < L - R, src_ref[at(hi, T), :].astype(F32), 0.0)
    a = jnp.concatenate([prev, x, nxt], axis=0)
    n = R + 2 * T
    y = pltpu.roll(a, 1, axis=0) * w[0:1, :] + a * w[1:2, :] + pltpu.roll(a, n - 1, axis=0) * w[2:3, :]
    return y[T:T + R]


def _hyena_kernel(v_ref, g1_ref, g2_ref, wv_ref, wg1_ref, wg2_ref, skip_ref,
                  kre_ref, kim_ref, hny_ref, csv_ref, csh_ref, o_ref,
                  z_ref, gate_ref, zcat_ref, zs0_ref, zs1_ref, ys0_ref, ys1_ref, acc_ref):
    P, NB, FQ, R = HY_P, HY_NB, HY_FQ, PW_ROWS
    ct = o_ref.shape[1]
    steps = FQ // MIX_ROWS // NB

    def chunk(q):
        return slice(2 * q * FQ, 2 * (q + 1) * FQ)

    def mix_share(o, q, k, zs_ref, ys_ref):
        for s in range(steps):
            for lh in range(ct // MIX_LANES):
                _mix_tile(o, q, (k * steps + s) * MIX_ROWS, lh, zs_ref, kre_ref, kim_ref, ys_ref)

    for j in range(NB):
        for r in range(P // R):
            zr = _conv3_rows(v_ref, wv_ref[...], j * P + r * R)
            z_ref[j * P + r * R:j * P + (r + 1) * R, :] = zr
            zcat_ref[j, r * R:(r + 1) * R, :] = zr.astype(BF16)

    def always(k):
        return pl.program_id(1) + k >= 0

    def phase_a(o, g_ref, wg_ref):
        for j in range(NB):
            zs0_ref[j] = _bdot(csv_ref[chunk(0), :], zcat_ref[j])
            for r in range(P // R):
                start = j * P + r * R
                gate_ref[start:start + R, :] = _conv3_rows(g_ref, wg_ref[...], start)

    def phase_b(o):
        for j in range(NB):
            zs1_ref[j] = _bdot(csv_ref[chunk(1), :], zcat_ref[j])
            mix_share(o, 0, j, zs0_ref, ys0_ref)
        tile = slice(FQ, FQ + MIX_ROWS)
        first = lax.broadcasted_iota(jnp.int32, (MIX_ROWS, ct), 0) == 0
        z_ny = [zs0_ref[j, tile, :] for j in range(NB)]
        for i in range(NB):
            y_ny = sum(z_ny[j] * hny_ref[o, i - j + NB - 1:i - j + NB, :] for j in range(NB))
            ys0_ref[i, tile, :] = jnp.where(
                first, y_ny, ys0_ref[i, tile, :].astype(F32)).astype(BF16)

    def phase_c(o):
        for i in range(NB):
            acc_ref[i] = _bdot(csh_ref[:, chunk(0)], ys0_ref[i])
            mix_share(o, 1, i, zs1_ref, ys1_ref)

    def phase_d(o):
        for i in range(NB):
            rows = slice(i * P, (i + 1) * P)
            conv = acc_ref[i] + _bdot(csh_ref[:, chunk(1)], ys1_ref[i])
            z_new = gate_ref[rows, :] * (conv + z_ref[rows, :] * skip_ref[o:o + 1, :])
            if o == 0:
                z_ref[rows, :] = z_new
                zcat_ref[i] = z_new.astype(BF16)
            else:
                o_ref[rows, :] = z_new.astype(BF16)

    for o, (g_ref, wg_ref) in enumerate(((g1_ref, wg1_ref), (g2_ref, wg2_ref))):
        phases = (functools.partial(phase_a, o, g_ref, wg_ref), functools.partial(phase_b, o),
                  functools.partial(phase_c, o), functools.partial(phase_d, o))
        for k, phase in enumerate(phases):
            pl.when(always(len(phases) * o + k))(phase)


def _hyena(proj3d, conv_w, skip, hre, him, hny, csv, csh):
    bsz = proj3d.shape[0]
    ct = HY_CT
    nct = HALF // ct
    base = 2 * HALF // ct

    def slab(part):
        return pl.BlockSpec((None, SEQ, ct), lambda c, b: (b, 0, base + part * nct + c))

    def cw(part):
        return pl.BlockSpec((3, ct), lambda c, b: (0, part * nct + c))

    assert HY_NQ == 2 and HY_FQ % (MIX_ROWS * HY_NB) == 0 and ct % MIX_LANES == 0
    hspec = _resident((2, HY_COEF, HY_P, ct), lambda c, b: (0, 0, 0, c))
    const = lambda c, b: (0, 0)
    return pl.pallas_call(
        _hyena_kernel,
        out_shape=jax.ShapeDtypeStruct((bsz, SEQ, HALF), BF16),
        grid=(nct, bsz),
        in_specs=[slab(0), slab(1), slab(2), cw(0), cw(1), cw(2),
                  pl.BlockSpec((2, ct), lambda c, b: (0, c)),
                  hspec, hspec,
                  pl.BlockSpec((2, HY_LAGS, ct), lambda c, b: (0, 0, c)),
                  pl.BlockSpec((2 * HY_P, HY_P), const), pl.BlockSpec((HY_P, 2 * HY_P), const)],
        out_specs=pl.BlockSpec((None, SEQ, ct), lambda c, b: (b, 0, c)),
        scratch_shapes=[pltpu.VMEM((SEQ, ct), F32), pltpu.VMEM((SEQ, ct), F32),
                        pltpu.VMEM((HY_NB, HY_P, ct), BF16),
                        pltpu.VMEM((HY_NB, 2 * HY_FQ, ct), F32),
                        pltpu.VMEM((HY_NB, 2 * HY_FQ, ct), F32),
                        pltpu.VMEM((HY_NB, 2 * HY_FQ, ct), BF16),
                        pltpu.VMEM((HY_NB, 2 * HY_FQ, ct), BF16),
                        pltpu.VMEM((HY_NB, HY_P, ct), F32)],
        compiler_params=_params(2),
        name="hyena_long_conv",
    )(proj3d, proj3d, proj3d, conv_w, conv_w, conv_w, skip, hre, him, hny, csv, csh)


def _window_sum(a, w):
    n = a.shape[0]
    right = w - w // 2 - 1
    s = a
    k = 1
    while k < w:
        s = s + pltpu.roll(s, k, axis=0)
        k *= 2
    return pltpu.roll(s, n - right, axis=0) if right else s


def _pool_conv_kernel(p_ref, b_ref, c_ref, h_ref, cw_ref, scale_ref, dw_ref, yc_ref, yd_ref,
                      chs_ref, pooled_ref, band_ref):
    L, R, n = SEQ, PW_ROWS, PW_ROWS + 2 * HALO
    offs = (0, (POOL_K - POOL_TILE) // 2, POOL_K - POOL_TILE)

    @pl.when(pl.program_id(0) == 0)
    def _():
        r = lax.broadcasted_iota(jnp.int32, (POOL_TILE, POOL_K), 0)
        c = lax.broadcasted_iota(jnp.int32, (POOL_TILE, POOL_K), 1)
        for g, w in enumerate(POOL_WINDOWS):
            left = w // 2
            for kind, off in enumerate(offs):
                d = c - off - r + left
                m = jnp.where((d >= 0) & (d < w), 1.0 / w, 0.0) - jnp.where(d == left, 1.0, 0.0)
                band_ref[g, kind] = m.astype(BF16)

    zeros = jnp.zeros((HALO, HALF), F32)
    chs_ref[0:HALO, :] = zeros
    chs_ref[L + HALO:L + 2 * HALO, :] = zeros

    for i in range(L // R):
        r = slice(i * R, (i + 1) * R)
        chs_ref[HALO + i * R:HALO + (i + 1) * R, :] = (
            c_ref[r, :].astype(F32) * h_ref[r, :].astype(F32))

    n_tiles = L // POOL_TILE
    for g, w in enumerate(POOL_WINDOWS):
        cols = slice(g * GROUP_DIM, (g + 1) * GROUP_DIM)
        left = w // 2
        right = w - left - 1

        for k in range(n_tiles):
            kind = 0 if k == 0 else (2 if k == n_tiles - 1 else 1)
            t0 = k * POOL_TILE
            ws = t0 - offs[kind]
            pooled_ref[t0:t0 + POOL_TILE, cols] = _bdot(
                band_ref[g, kind], p_ref[ws:ws + POOL_K, cols]).astype(BF16)

        for top in (True, False):
            pad = jnp.zeros((POOL_EDGE, GROUP_DIM), F32)
            if top:
                a = jnp.concatenate([pad, p_ref[0:2 * POOL_EDGE, cols].astype(F32)], axis=0)
                first = 0
            else:
                a = jnp.concatenate([p_ref[L - 2 * POOL_EDGE:L, cols].astype(F32), pad], axis=0)
                first = L - POOL_EDGE
            keep = slice(POOL_EDGE, 2 * POOL_EDGE)
            t = first + lax.broadcasted_iota(jnp.int32, (POOL_EDGE, GROUP_DIM), 0)
            cnt = (jnp.minimum(t + right, L - 1) - jnp.maximum(t - left, 0) + 1).astype(F32)
            pooled = _window_sum(a, w)[keep] / cnt - a[keep]
            pooled_ref[first:first + POOL_EDGE, cols] = pooled.astype(BF16)

        dw = dw_ref[:, cols]
        for i in range(L // R):
            r0 = i * R
            e = chs_ref[r0:r0 + n, cols]
            conv = (pltpu.roll(e, 1, axis=0) * dw[0:1, :] + e * dw[1:2, :]
                    + pltpu.roll(e, n - 1, axis=0) * dw[2:3, :])
            yd = b_ref[r0:r0 + R, cols].astype(F32) * conv[HALO:HALO + R]
            yd_ref[r0:r0 + R, cols] = yd.astype(BF16)

        y = _bdot(pooled_ref[:, cols], cw_ref[g]) * scale_ref[:, cols]
        yc_ref[:, cols] = y.astype(BF16)


def _pool_conv(proj3d, c_w, c_scale, d_conv_w):
    bsz = proj3d.shape[0]

    def part(k):
        return pl.BlockSpec((None, SEQ, HALF), lambda b: (b, 0, k))

    out = pl.BlockSpec((None, SEQ, HALF), lambda b: (b, 0, 0))
    return pl.pallas_call(
        _pool_conv_kernel,
        out_shape=(jax.ShapeDtypeStruct((bsz, SEQ, HALF), BF16),) * 2,
        grid=(bsz,),
        in_specs=[part(0), part(1), part(2), part(3),
                  pl.BlockSpec((len(POOL_WINDOWS), GROUP_DIM, GROUP_DIM), lambda b: (0, 0, 0)),
                  pl.BlockSpec((1, HALF), lambda b: (0, 0)),
                  pl.BlockSpec((3, HALF), lambda b: (0, 0))],
        out_specs=(out, out),
        scratch_shapes=[pltpu.VMEM((SEQ + 2 * HALO, HALF), F32),
                        pltpu.VMEM((SEQ, HALF), BF16),
                        pltpu.VMEM((len(POOL_WINDOWS), 3, POOL_TILE, POOL_K), BF16)],
        compiler_params=_params(1),
        name="pool_short_conv",
    )(proj3d, proj3d, proj3d, proj3d, c_w, c_scale, d_conv_w)


def _out_ffn_kernel(x_ref, y0_ref, y1_ref, wo_ref, g_ref, wgu_ref, wd_ref, o_ref, h_ref, a_ref):
    sub = ROW_TILE // FFN_SUBTILES
    rows = [slice(s * sub, (s + 1) * sub) for s in range(FFN_SUBTILES)]
    for r in rows:
        mix = _bdot(y0_ref[r, :], wo_ref[:HALF, :]) + _bdot(y1_ref[r, :], wo_ref[HALF:, :])
        o_ref[r, :] = x_ref[r, :] + _rms(mix, g_ref[1:2, :])
        h_ref[r, :] = _rms(o_ref[r, :], g_ref[2:3, :]).astype(BF16)
    for r in rows:
        for k in range(D_FF // FF_CHUNK):
            gate = _bdot(h_ref[r, :], wgu_ref[:, k * FF_CHUNK:(k + 1) * FF_CHUNK])
            up = _bdot(h_ref[r, :], wgu_ref[:, D_FF + k * FF_CHUNK:D_FF + (k + 1) * FF_CHUNK])
            a_ref[r, k * FF_CHUNK:(k + 1) * FF_CHUNK] = (
                gate * jax.nn.sigmoid(gate) * up).astype(BF16)
    for r in rows:
        f = _bdot(a_ref[r, :], wd_ref[...])
        o_ref[r, :] = o_ref[r, :] + _rms(f, g_ref[3:4, :])


def _out_ffn(x2d, y0, y1, w_out, g, w_gu, w_down):
    m = x2d.shape[0]
    tile = lambda n: pl.BlockSpec((ROW_TILE, n), lambda i: (i, 0))
    const = lambda i: (0, 0)
    return pl.pallas_call(
        _out_ffn_kernel,
        out_shape=jax.ShapeDtypeStruct((m, D_MODEL), F32),
        grid=(m // ROW_TILE,),
        in_specs=[tile(D_MODEL), tile(HALF), tile(HALF),
                  _resident((D_MODEL, D_MODEL), const),
                  pl.BlockSpec((4, D_MODEL), const),
                  _resident((D_MODEL, 2 * D_FF), const),
                  _resident((D_FF, D_MODEL), const)],
        out_specs=tile(D_MODEL),
        scratch_shapes=[pltpu.VMEM((ROW_TILE, D_MODEL), BF16),
                        pltpu.VMEM((ROW_TILE, D_FF), BF16)],
        compiler_params=_params(1, FFN_VMEM_LIMIT),
        name="out_proj_ffn",
    )(x2d, y0, y1, w_out, g, w_gu, w_down)


def kernel(x, norm_g, ffn_w_gu, ffn_w_down, ab_w_in, ab_w_out, a_ln_g, a_w_s, a_b_s, b_conv_w, b_filt_w1, b_filt_b1, b_filt_freq, b_filt_w2, b_filt_b2, b_filt_w3, b_decay, b_skip, cd_w_in, cd_w_out, c_w, c_scale, d_conv_w):
    bsz, seq, d = x.shape
    assert (seq, d) == (SEQ, D_MODEL) and norm_g.shape[0] == DEPTH
    m = bsz * seq
    bands = jnp.linspace(1e-4, FILTER_BANDS - 1, FILTER_BANDS, dtype=F32)[None, :]
    csv, csh = _dft_tables()
    x2d = x.reshape(m, d)
    for i in range(DEPTH):
        j = i // 2
        g = norm_g[i]
        if i % 2 == 0:
            proj = _in_proj(x2d, g[0:1], ab_w_in[j].astype(BF16), 2 * HALF).reshape(bsz, seq, -1)
            y0 = _gmlp(proj, a_ln_g[j][None, :], a_w_s[j].astype(BF16), a_b_s[j][:, :, None])
            hidden = _filter_hidden(bands, b_filt_w1[j], b_filt_b1[j][None, :], b_filt_freq[j],
                                    b_filt_w2[j], b_filt_b2[j][None, :])
            hre, him, hny = _filter_spectra(hidden, b_filt_w3[j], b_decay[j][None, :], csv)
            y1 = _hyena(proj, b_conv_w[j], b_skip[j], hre, him, hny, csv, csh)
            w_out = ab_w_out[j]
        else:
            proj = _in_proj(x2d, g[0:1], cd_w_in[j].astype(BF16), 0).reshape(bsz, seq, -1)
            y0, y1 = _pool_conv(proj, c_w[j].astype(BF16), c_scale[j][None, :], d_conv_w[j])
            w_out = cd_w_out[j]
        x2d = _out_ffn(x2d, y0.reshape(m, HALF), y1.reshape(m, HALF), w_out.astype(BF16), g,
                       ffn_w_gu[i].astype(BF16), ffn_w_down[i].astype(BF16))
    return x2d.reshape(bsz, seq, d)
```

```python
import functools
import math

import jax
import jax.numpy as jnp
from jax import lax
from jax.experimental import pallas as pl
from jax.experimental.pallas import tpu as pltpu

F32 = jnp.float32
BF16 = jnp.bfloat16

D_MODEL = 1024
SEQ = 2048
DEPTH = 4
HALF = D_MODEL // 2
CHUNK = 128
N_CHUNKS = SEQ // CHUNK
A_GROUPS = 4
GROUP_DIM = HALF // A_GROUPS
FILTER_BANDS = 16
FILTER_HIDDEN = 64
POOL_WINDOWS = (2, 4, 8, 16)
D_FF = 2816
RMS_EPS = 1e-6
LN_EPS = 1e-5
HY_P = 512
HY_NB = SEQ // HY_P
HY_LAGS = 2 * HY_NB - 1
NFFT = 2 * HY_P
HY_NQ = 2
HY_FQ = HY_P // HY_NQ
HY_COEF = 9

ROW_TILE = 1024
FFN_SUBTILES = 4
FFN_VMEM_LIMIT = 60 * 1024 * 1024
IN_TILE = 2048
IN_SUBTILES = 4
IN_CHUNK = 256
FF_CHUNK = 256
HY_CT = 256
MIX_ROWS = 16
MIX_LANES = 128
PW_ROWS = 128
HALO = 8
POOL_TILE = 256
POOL_K = 512
POOL_EDGE = 16
VMEM_LIMIT = 56 * 1024 * 1024


def _params(n_axes, vmem=VMEM_LIMIT, flags=None):
    return pltpu.CompilerParams(
        dimension_semantics=("arbitrary",) * n_axes, vmem_limit_bytes=vmem, flags=flags)


def _resident(shape, index_map):
    return pl.BlockSpec(shape, index_map, pipeline_mode=pl.Buffered(1))


def _rms(x, g):
    return x * lax.rsqrt(jnp.mean(x * x, axis=-1, keepdims=True) + RMS_EPS) * g


def _bdot(a, b):
    return jnp.dot(a, b, preferred_element_type=F32)


def _fdot(a, b):
    return jnp.dot(a, b, preferred_element_type=F32, precision=lax.Precision.HIGHEST)


def _shift_rows(x, k, row):
    n = x.shape[0]
    y = pltpu.roll(x, k % n, axis=0)
    if k > 0:
        return jnp.where(row >= k, y, 0.0)
    return jnp.where(row < n + k, y, 0.0)


def _conv3(x, w, row):
    return (_shift_rows(x, 1, row) * w[0:1, :] + x * w[1:2, :]
            + _shift_rows(x, -1, row) * w[2:3, :])


def _dft_kernel(csv_ref, csh_ref):
    P, FQ = HY_P, HY_FQ
    f = lax.broadcasted_iota(jnp.int32, (P, 128), 0)
    lane = lax.broadcasted_iota(jnp.int32, (P, 128), 1)
    scale = 2.0 * math.pi / NFFT
    ang_b = ((f * lane) & (NFFT - 1)).astype(F32) * scale
    ang_a = ((f * (lane * 128)) & (NFFT - 1)).astype(F32) * scale
    cb, sb = jnp.cos(ang_b), jnp.sin(ang_b)
    ca, sa = jnp.cos(ang_a), jnp.sin(ang_a)
    for a in range(P // 128):
        cols = slice(a * 128, (a + 1) * 128)
        ca_a = ca[:, a:a + 1]
        sa_a = sa[:, a:a + 1]
        c = (ca_a * cb - sa_a * sb).astype(BF16)
        s = sa_a * cb + ca_a * sb
        s_v = jnp.where(f == 0, jnp.where((lane & 1) == 0, 1.0, -1.0), s).astype(BF16)
        for q in range(HY_NQ):
            csv_ref[2 * q * FQ:(2 * q + 1) * FQ, cols] = c[q * FQ:(q + 1) * FQ]
            csv_ref[(2 * q + 1) * FQ:(2 * q + 2) * FQ, cols] = s_v[q * FQ:(q + 1) * FQ]
        s_h = s
        if a == 0:
            s_h = jnp.where(lane == 0, jnp.where((f & 1) == 0, 1.0, -1.0), s)
        q, off = divmod(a * 128, FQ)
        csh_ref[:, 2 * q * FQ + off:2 * q * FQ + off + 128] = c
        csh_ref[:, (2 * q + 1) * FQ + off:(2 * q + 1) * FQ + off + 128] = s_h.astype(BF16)


def _dft_tables():
    return pl.pallas_call(
        _dft_kernel,
        out_shape=(jax.ShapeDtypeStruct((2 * HY_P, HY_P), BF16),
                   jax.ShapeDtypeStruct((HY_P, 2 * HY_P), BF16)),
        compiler_params=_params(0),
        name="dft_tables",
    )()


def _filter_hidden_kernel(bands_ref, w1_ref, b1_ref, freq_ref, w2_ref, b2_ref, h_ref):
    L = SEQ
    r = lax.broadcasted_iota(jnp.int32, (L, 1), 0)
    t = jnp.where(pl.program_id(0) == 0, r, L - 1 - r).astype(F32)
    t01 = t / (L - 1)
    fw = (2 * math.pi * t / L) * bands_ref[...]
    pre = (t01 * w1_ref[0:1, :] + _fdot(jnp.cos(fw), w1_ref[1:1 + FILTER_BANDS, :])
           + _fdot(-jnp.sin(fw), w1_ref[1 + FILTER_BANDS:, :]) + b1_ref[...])
    h = jnp.sin(freq_ref[0:1, :] * pre)
    h_ref[...] = jnp.sin(freq_ref[1:2, :] * (_fdot(h, w2_ref[...]) + b2_ref[...]))


def _filter_hidden(bands, w1, b1, freq, w2, b2):
    const = lambda s: (0, 0)
    return pl.pallas_call(
        _filter_hidden_kernel,
        out_shape=jax.ShapeDtypeStruct((2, SEQ, FILTER_HIDDEN), F32),
        grid=(2,),
        in_specs=[pl.BlockSpec((1, FILTER_BANDS), const),
                  pl.BlockSpec((1 + 2 * FILTER_BANDS, FILTER_HIDDEN), const),
                  pl.BlockSpec((1, FILTER_HIDDEN), const),
                  pl.BlockSpec((2, FILTER_HIDDEN), const),
                  pl.BlockSpec((FILTER_HIDDEN, FILTER_HIDDEN), const),
                  pl.BlockSpec((1, FILTER_HIDDEN), const)],
        out_specs=pl.BlockSpec((None, SEQ, FILTER_HIDDEN), lambda s: (s, 0, 0)),
        compiler_params=_params(1),
        name="hyena_filter_hidden",
    )(bands, w1, b1, freq, w2, b2)


def _filter_kernel(hid_ref, w3f_ref, w3b_ref, decf_ref, decb_ref, csv_ref,
                   kre_ref, kim_ref, hny_ref):
    L, P, NB = SEQ, HY_P, HY_NB
    ct = kre_ref.shape[-1]
    row = lax.broadcasted_iota(jnp.int32, (L, ct), 0)
    t_asc = row.astype(F32) / (L - 1)
    t_dsc = (L - 1 - row).astype(F32) / (L - 1)
    dec_f = jnp.abs(decf_ref[...])
    dec_b = jnp.abs(decb_ref[...])
    h_asc = hid_ref[0]
    h_dsc = hid_ref[1]
    inner = (row & (P - 1)) != 0
    f_asc = _fdot(h_asc, w3f_ref[...]) * jnp.exp(-t_asc * dec_f)
    b_dsc = _fdot(h_dsc, w3b_ref[...]) * jnp.exp(-t_dsc * dec_b)
    f_dsc = _fdot(h_dsc, w3f_ref[...]) * jnp.exp(-t_dsc * dec_f)
    f_dsc = jnp.where(inner, _shift_rows(f_dsc, 1, row), 0.0)
    b_asc = _fdot(h_asc, w3b_ref[...]) * jnp.exp(-t_asc * dec_b)
    b_asc = jnp.where(inner, _shift_rows(b_asc, 1, row), 0.0)

    def blk(x, i):
        return x[i * P:(i + 1) * P, :]

    frow = lax.broadcasted_iota(jnp.int32, (P, ct), 0)
    wgt = jnp.where(frow == 0, 1.0 / NFFT, 2.0 / NFFT)
    sign = jnp.where((frow & 1) == 0, 1.0, -1.0)
    FQ = HY_FQ
    h = {}
    for idx in range(HY_LAGS):
        d = idx - (NB - 1)
        if d >= 1:
            kp, kn = blk(f_asc, d), blk(f_dsc, NB - d)
        elif d == 0:
            kp, kn = blk(f_asc, 0), blk(b_asc, 0)
        else:
            kp, kn = blk(b_dsc, NB + d), blk(b_asc, -d)
        ev = kp + kn
        od = kp - kn
        evb = ev.astype(BF16)
        odb = od.astype(BF16)
        re = jnp.concatenate([_bdot(csv_ref[2 * q * FQ:(2 * q + 1) * FQ, :], evb)
                              for q in range(HY_NQ)], axis=0)
        im = jnp.concatenate([_bdot(csv_ref[(2 * q + 1) * FQ:(2 * q + 2) * FQ, :], odb)
                              for q in range(HY_NQ)], axis=0)
        h[d] = (re * wgt, jnp.where(frow == 0, 0.0, -im * wgt))
        hny_ref[idx:idx + 1, :] = jnp.sum(ev * sign, axis=0, keepdims=True) * (1.0 / NFFT)

    def sub(a, b):
        return (a[0] - b[0], a[1] - b[1])

    def coef3(a0, a_dn, a_up):
        return [a0, sub(a_dn, a0), sub(a_up, a0)]

    coefs = (coef3(h[0], h[1], h[-1])
             + coef3(sub(h[2], h[0]), sub(h[3], h[1]), sub(h[1], h[-1]))
             + coef3(sub(h[-2], h[0]), sub(h[-1], h[1]), sub(h[-3], h[-1])))
    for c, (re, im) in enumerate(coefs):
        kre_ref[c] = re
        kim_ref[c] = im


def _filter_spectra(hidden, w3, decay, csv):
    ct = HY_CT
    nct = HALF // ct
    fcol = lambda o, c: (0, o * nct + c)
    bcol = lambda o, c: (0, (2 + o) * nct + c)
    hspec = pl.BlockSpec((None, HY_COEF, HY_P, ct), lambda o, c: (o, 0, 0, c))
    return pl.pallas_call(
        _filter_kernel,
        out_shape=(jax.ShapeDtypeStruct((2, HY_COEF, HY_P, HALF), F32),
                   jax.ShapeDtypeStruct((2, HY_COEF, HY_P, HALF), F32),
                   jax.ShapeDtypeStruct((2, HY_LAGS, HALF), F32)),
        grid=(2, nct),
        in_specs=[
            pl.BlockSpec((2, SEQ, FILTER_HIDDEN), lambda o, c: (0, 0, 0)),
            pl.BlockSpec((FILTER_HIDDEN, ct), fcol),
            pl.BlockSpec((FILTER_HIDDEN, ct), bcol),
            pl.BlockSpec((1, ct), fcol),
            pl.BlockSpec((1, ct), bcol),
            pl.BlockSpec((2 * HY_P, HY_P), lambda o, c: (0, 0)),
        ],
        out_specs=(hspec, hspec,
                   pl.BlockSpec((None, HY_LAGS, ct), lambda o, c: (o, 0, c))),
        compiler_params=_params(2),
        name="hyena_filter_spectra",
    )(hidden, w3, w3, decay, decay, csv)


def _in_proj_kernel(x_ref, g_ref, w_ref, o_ref, h_ref, *, gelu_cols):
    sub = IN_TILE // IN_SUBTILES
    rows = [slice(s * sub, (s + 1) * sub) for s in range(IN_SUBTILES)]
    for r in rows:
        h_ref[r, :] = _rms(x_ref[r, :], g_ref[...]).astype(BF16)
    n_out = o_ref.shape[1]
    gelu_chunks = list(range(0, gelu_cols, IN_CHUNK))
    plain_chunks = list(range(gelu_cols, n_out, IN_CHUNK))
    order = [c for pair in zip(gelu_chunks, plain_chunks) for c in pair]
    order += gelu_chunks[len(plain_chunks):] + plain_chunks[len(gelu_chunks):]
    for r in rows:
        for c0 in order:
            p = _bdot(h_ref[r, :], w_ref[:, c0:c0 + IN_CHUNK])
            if c0 < gelu_cols:
                p = jax.nn.gelu(p)
            o_ref[r, c0:c0 + IN_CHUNK] = p.astype(BF16)


def _in_proj(x2d, g, w, layer, gelu_cols):
    m = x2d.shape[0]
    n_out = w.shape[2]
    return pl.pallas_call(
        functools.partial(_in_proj_kernel, gelu_cols=gelu_cols),
        out_shape=jax.ShapeDtypeStruct((m, n_out), BF16),
        grid=(m // IN_TILE,),
        in_specs=[pl.BlockSpec((IN_TILE, D_MODEL), lambda i: (i, 0)),
                  pl.BlockSpec((1, D_MODEL), lambda i: (0, 0)),
                  _resident((None, D_MODEL, n_out), lambda i: (layer, 0, 0))],
        out_specs=pl.BlockSpec((IN_TILE, n_out), lambda i: (i, 0)),
        scratch_shapes=[pltpu.VMEM((IN_TILE, D_MODEL), BF16)],
        compiler_params=_params(1),
        name="rms_in_proj",
    )(x2d, g, w)


def _gmlp_kernel(za_ref, lng_ref, ws_ref, bs_ref, o_ref, vn_ref):
    def ln_body(i, carry):
        r = pl.ds(pl.multiple_of(i * PW_ROWS, PW_ROWS), PW_ROWS)
        v = za_ref[r, HALF:].astype(F32)
        xc = v - jnp.mean(v, axis=-1, keepdims=True)
        y = xc * lax.rsqrt(jnp.mean(xc * xc, axis=-1, keepdims=True) + LN_EPS) * lng_ref[...]
        vn_ref[r, :] = y.astype(BF16)
        return carry
    lax.fori_loop(0, SEQ // PW_ROWS, ln_body, 0, unroll=4)

    for g in range(A_GROUPS):
        cols = slice(g * GROUP_DIM, (g + 1) * GROUP_DIM)
        vg = jnp.concatenate(
            [vn_ref[n * CHUNK:(n + 1) * CHUNK, cols] for n in range(N_CHUNKS)], axis=1)
        s = _bdot(ws_ref[g], vg) + bs_ref[g]
        for n in range(N_CHUNKS):
            rows = slice(n * CHUNK, (n + 1) * CHUNK)
            u = za_ref[rows, cols].astype(F32)
            o_ref[rows, cols] = (u * s[:, n * GROUP_DIM:(n + 1) * GROUP_DIM]).astype(BF16)


def _gmlp(proj3d, ln_g, w_s, b_s):
    bsz = proj3d.shape[0]
    return pl.pallas_call(
        _gmlp_kernel,
        out_shape=jax.ShapeDtypeStruct((bsz, SEQ, HALF), BF16),
        grid=(bsz,),
        in_specs=[pl.BlockSpec((None, SEQ, 2 * HALF), lambda b: (b, 0, 0)),
                  pl.BlockSpec((1, HALF), lambda b: (0, 0)),
                  pl.BlockSpec((A_GROUPS, CHUNK, CHUNK), lambda b: (0, 0, 0)),
                  pl.BlockSpec((A_GROUPS, CHUNK, 1), lambda b: (0, 0, 0))],
        out_specs=pl.BlockSpec((None, SEQ, HALF), lambda b: (b, 0, 0)),
        scratch_shapes=[pltpu.VMEM((SEQ, HALF), BF16)],
        compiler_params=_params(1),
        name="gmlp_spatial_gating",
    )(proj3d, ln_g, w_s, b_s)


def _cmul(z, h):
    return z[0] * h[0] + z[1] * h[1], z[1] * h[0] - z[0] * h[1]


def _cadd(a, b):
    return a[0] + b[0], a[1] + b[1]


def _mix_tile(o, q, row0, lh, zs_ref, kre_ref, kim_ref, ys_ref):
    FQ = HY_FQ
    fr = slice(q * FQ + row0, q * FQ + row0 + MIX_ROWS)
    rr = slice(row0, row0 + MIX_ROWS)
    rm = slice(FQ + row0, FQ + row0 + MIX_ROWS)
    lanes = slice(lh * MIX_LANES, (lh + 1) * MIX_LANES)

    def coef(c):
        return kre_ref[o, c, fr, lanes], kim_ref[o, c, fr, lanes]

    def toep2(c, v0, v1):
        q1 = _cmul(_cadd(v0, v1), coef(c))
        q2 = _cmul(v0, coef(c + 1))
        q3 = _cmul(v1, coef(c + 2))
        return _cadd(q1, q3), _cadd(q1, q2)

    z = [(zs_ref[j, rr, lanes], zs_ref[j, rm, lanes]) for j in range(HY_NB)]
    p1 = toep2(0, _cadd(z[0], z[2]), _cadd(z[1], z[3]))
    p2 = toep2(3, z[0], z[1])
    p3 = toep2(6, z[2], z[3])
    y = [_cadd(p1[0], p3[0]), _cadd(p1[1], p3[1]), _cadd(p1[0], p2[0]), _cadd(p1[1], p2[1])]
    for i in range(HY_NB):
        ys_ref[i, rr, lanes] = y[i][0].astype(BF16)
        ys_ref[i, rm, lanes] = y[i][1].astype(BF16)


def _conv3_rows(src_ref, w, start):
    L, R, T = SEQ, PW_ROWS, 16
    if isinstance(start, int):
        at = lambda s, n: slice(s, s + n)
        lo, hi = max(start - T, 0), min(start + R, L - T)
    else:
        at = lambda s, n: pl.ds(pl.multiple_of(s, T), n)
        lo, hi = jnp.maximum(start - T, 0), jnp.minimum(start + R, L - T)
    x = src_ref[at(start, R), :].astype(F32)
    prev = jnp.where(start > 0, src_ref[at(lo, T), :].astype(F32), 0.0)
    nxt = jnp.where(start < L - R, src_ref[at(hi, T), :].astype(F32), 0.0)
    a = jnp.concatenate([prev, x, nxt], axis=0)
    n = R + 2 * T
    y = pltpu.roll(a, 1, axis=0) * w[0:1, :] + a * w[1:2, :] + pltpu.roll(a, n - 1, axis=0) * w[2:3, :]
    return y[T:T + R]


def _hyena_kernel(v_ref, g1_ref, g2_ref, wv_ref, wg1_ref, wg2_ref, skip_ref,
                  kre_ref, kim_ref, hny_ref, csv_ref, csh_ref, o_ref,
                  z_ref, gate_ref, zcat_ref, zs0_ref, zs1_ref, ys0_ref, ys1_ref, acc_ref):
    P, NB, FQ, R = HY_P, HY_NB, HY_FQ, PW_ROWS
    ct = o_ref.shape[1]
    steps = FQ // MIX_ROWS // NB

    def chunk(q):
        return slice(2 * q * FQ, 2 * (q + 1) * FQ)

    def mix_share(o, q, k, zs_ref, ys_ref):
        for s in range(steps):
            for lh in range(ct // MIX_LANES):
                _mix_tile(o, q, (k * steps + s) * MIX_ROWS, lh, zs_ref, kre_ref, kim_ref, ys_ref)

    for j in range(NB):
        for r in range(P // R):
            zr = _conv3_rows(v_ref, wv_ref[...], j * P + r * R)
            z_ref[j * P + r * R:j * P + (r + 1) * R, :] = zr
            zcat_ref[j, r * R:(r + 1) * R, :] = zr.astype(BF16)

    def always(k):
        return pl.program_id(1) + k >= 0

    def phase_a(o, g_ref, wg_ref):
        for j in range(NB):
            zs0_ref[j] = _bdot(csv_ref[chunk(0), :], zcat_ref[j])
            for r in range(P // R):
                start = j * P + r * R
                gate_ref[start:start + R, :] = _conv3_rows(g_ref, wg_ref[...], start)

    def phase_b(o):
        for j in range(NB):
            zs1_ref[j] = _bdot(csv_ref[chunk(1), :], zcat_ref[j])
            mix_share(o, 0, j, zs0_ref, ys0_ref)
        tile = slice(FQ, FQ + MIX_ROWS)
        first = lax.broadcasted_iota(jnp.int32, (MIX_ROWS, ct), 0) == 0
        z_ny = [zs0_ref[j, tile, :] for j in range(NB)]
        for i in range(NB):
            y_ny = sum(z_ny[j] * hny_ref[o, i - j + NB - 1:i - j + NB, :] for j in range(NB))
            ys0_ref[i, tile, :] = jnp.where(
                first, y_ny, ys0_ref[i, tile, :].astype(F32)).astype(BF16)

    def phase_c(o):
        for i in range(NB):
            acc_ref[i] = _bdot(csh_ref[:, chunk(0)], ys0_ref[i])
            mix_share(o, 1, i, zs1_ref, ys1_ref)

    def phase_d(o):
        for i in range(NB):
            rows = slice(i * P, (i + 1) * P)
            conv = acc_ref[i] + _bdot(csh_ref[:, chunk(1)], ys1_ref[i])
            z_new = gate_ref[rows, :] * (conv + z_ref[rows, :] * skip_ref[o:o + 1, :])
            if o == 0:
                z_ref[rows, :] = z_new
                zcat_ref[i] = z_new.astype(BF16)
            else:
                o_ref[rows, :] = z_new.astype(BF16)

    for o, (g_ref, wg_ref) in enumerate(((g1_ref, wg1_ref), (g2_ref, wg2_ref))):
        phases = (functools.partial(phase_a, o, g_ref, wg_ref), functools.partial(phase_b, o),
                  functools.partial(phase_c, o), functools.partial(phase_d, o))
        for k, phase in enumerate(phases):
            pl.when(always(len(phases) * o + k))(phase)


def _hyena(proj3d, conv_w, skip, hre, him, hny, csv, csh):
    bsz = proj3d.shape[0]
    ct = HY_CT
    nct = HALF // ct
    base = 2 * HALF // ct

    def slab(part):
        return pl.BlockSpec((None, SEQ, ct), lambda c, b: (b, 0, base + part * nct + c))

    def cw(part):
        return pl.BlockSpec((3, ct), lambda c, b: (0, part * nct + c))

    assert HY_NQ == 2 and HY_FQ % (MIX_ROWS * HY_NB) == 0 and ct % MIX_LANES == 0
    hspec = _resident((2, HY_COEF, HY_P, ct), lambda c, b: (0, 0, 0, c))
    const = lambda c, b: (0, 0)
    return pl.pallas_call(
        _hyena_kernel,
        out_shape=jax.ShapeDtypeStruct((bsz, SEQ, HALF), BF16),
        grid=(nct, bsz),
        in_specs=[slab(0), slab(1), slab(2), cw(0), cw(1), cw(2),
                  pl.BlockSpec((2, ct), lambda c, b: (0, c)),
                  hspec, hspec,
                  pl.BlockSpec((2, HY_LAGS, ct), lambda c, b: (0, 0, c)),
                  pl.BlockSpec((2 * HY_P, HY_P), const), pl.BlockSpec((HY_P, 2 * HY_P), const)],
        out_specs=pl.BlockSpec((None, SEQ, ct), lambda c, b: (b, 0, c)),
        scratch_shapes=[pltpu.VMEM((SEQ, ct), F32), pltpu.VMEM((SEQ, ct), F32),
                        pltpu.VMEM((HY_NB, HY_P, ct), BF16),
                        pltpu.VMEM((HY_NB, 2 * HY_FQ, ct), F32),
                        pltpu.VMEM((HY_NB, 2 * HY_FQ, ct), F32),
                        pltpu.VMEM((HY_NB, 2 * HY_FQ, ct), BF16),
                        pltpu.VMEM((HY_NB, 2 * HY_FQ, ct), BF16),
                        pltpu.VMEM((HY_NB, HY_P, ct), F32)],
        compiler_params=_params(2),
        name="hyena_long_conv",
    )(proj3d, proj3d, proj3d, conv_w, conv_w, conv_w, skip, hre, him, hny, csv, csh)


def _window_sum(a, w):
    n = a.shape[0]
    right = w - w // 2 - 1
    s = a
    k = 1
    while k < w:
        s = s + pltpu.roll(s, k, axis=0)
        k *= 2
    return pltpu.roll(s, n - right, axis=0) if right else s


def _pool_conv_kernel(p_ref, b_ref, c_ref, h_ref, cw_ref, scale_ref, dw_ref, yc_ref, yd_ref,
                      chs_ref, pooled_ref, band_ref):
    L, R, n = SEQ, PW_ROWS, PW_ROWS + 2 * HALO
    offs = (0, (POOL_K - POOL_TILE) // 2, POOL_K - POOL_TILE)

    @pl.when(pl.program_id(0) == 0)
    def _():
        r = lax.broadcasted_iota(jnp.int32, (POOL_TILE, POOL_K), 0)
        c = lax.broadcasted_iota(jnp.int32, (POOL_TILE, POOL_K), 1)
        for g, w in enumerate(POOL_WINDOWS):
            left = w // 2
            for kind, off in enumerate(offs):
                d = c - off - r + left
                m = jnp.where((d >= 0) & (d < w), 1.0 / w, 0.0) - jnp.where(d == left, 1.0, 0.0)
                band_ref[g, kind] = m.astype(BF16)

    zeros = jnp.zeros((HALO, HALF), F32)
    chs_ref[0:HALO, :] = zeros
    chs_ref[L + HALO:L + 2 * HALO, :] = zeros

    for i in range(L // R):
        r = slice(i * R, (i + 1) * R)
        chs_ref[HALO + i * R:HALO + (i + 1) * R, :] = (
            c_ref[r, :].astype(F32) * h_ref[r, :].astype(F32))

    n_tiles = L // POOL_TILE
    for g, w in enumerate(POOL_WINDOWS):
        cols = slice(g * GROUP_DIM, (g + 1) * GROUP_DIM)
        left = w // 2
        right = w - left - 1

        for k in range(n_tiles):
            kind = 0 if k == 0 else (2 if k == n_tiles - 1 else 1)
            t0 = k * POOL_TILE
            ws = t0 - offs[kind]
            pooled_ref[t0:t0 + POOL_TILE, cols] = _bdot(
                band_ref[g, kind], p_ref[ws:ws + POOL_K, cols]).astype(BF16)

        for top in (True, False):
            pad = jnp.zeros((POOL_EDGE, GROUP_DIM), F32)
            if top:
                a = jnp.concatenate([pad, p_ref[0:2 * POOL_EDGE, cols].astype(F32)], axis=0)
                first = 0
            else:
                a = jnp.concatenate([p_ref[L - 2 * POOL_EDGE:L, cols].astype(F32), pad], axis=0)
                first = L - POOL_EDGE
            keep = slice(POOL_EDGE, 2 * POOL_EDGE)
            t = first + lax.broadcasted_iota(jnp.int32, (POOL_EDGE, GROUP_DIM), 0)
            cnt = (jnp.minimum(t + right, L - 1) - jnp.maximum(t - left, 0) + 1).astype(F32)
            pooled = _window_sum(a, w)[keep] / cnt - a[keep]
            pooled_ref[first:first + POOL_EDGE, cols] = pooled.astype(BF16)

        dw = dw_ref[:, cols]
        for i in range(L // R):
            r0 = i * R
            e = chs_ref[r0:r0 + n, cols]
            conv = (pltpu.roll(e, 1, axis=0) * dw[0:1, :] + e * dw[1:2, :]
                    + pltpu.roll(e, n - 1, axis=0) * dw[2:3, :])
            yd = b_ref[r0:r0 + R, cols].astype(F32) * conv[HALO:HALO + R]
            yd_ref[r0:r0 + R, cols] = yd.astype(BF16)

        y = _bdot(pooled_ref[:, cols], cw_ref[g]) * scale_ref[:, cols]
        yc_ref[:, cols] = y.astype(BF16)


def _pool_conv(proj3d, c_w, c_scale, d_conv_w):
    bsz = proj3d.shape[0]

    def part(k):
        return pl.BlockSpec((None, SEQ, HALF), lambda b: (b, 0, k))

    out = pl.BlockSpec((None, SEQ, HALF), lambda b: (b, 0, 0))
    return pl.pallas_call(
        _pool_conv_kernel,
        out_shape=(jax.ShapeDtypeStruct((bsz, SEQ, HALF), BF16),) * 2,
        grid=(bsz,),
        in_specs=[part(0), part(1), part(2), part(3),
                  pl.BlockSpec((len(POOL_WINDOWS), GROUP_DIM, GROUP_DIM), lambda b: (0, 0, 0)),
                  pl.BlockSpec((1, HALF), lambda b: (0, 0)),
                  pl.BlockSpec((3, HALF), lambda b: (0, 0))],
        out_specs=(out, out),
        scratch_shapes=[pltpu.VMEM((SEQ + 2 * HALO, HALF), F32),
                        pltpu.VMEM((SEQ, HALF), BF16),
                        pltpu.VMEM((len(POOL_WINDOWS), 3, POOL_TILE, POOL_K), BF16)],
        compiler_params=_params(1),
        name="pool_short_conv",
    )(proj3d, proj3d, proj3d, proj3d, c_w, c_scale, d_conv_w)


def _out_ffn_kernel(x_ref, y0_ref, y1_ref, wo_ref, g_ref, wgu_ref, wd_ref, o_ref, h_ref, a_ref):
    sub = ROW_TILE // FFN_SUBTILES
    rows = [slice(s * sub, (s + 1) * sub) for s in range(FFN_SUBTILES)]
    for r in rows:
        mix = _bdot(y0_ref[r, :], wo_ref[:HALF, :]) + _bdot(y1_ref[r, :], wo_ref[HALF:, :])
        o_ref[r, :] = x_ref[r, :] + _rms(mix, g_ref[1:2, :])
        h_ref[r, :] = _rms(o_ref[r, :], g_ref[2:3, :]).astype(BF16)
    for r in rows:
        for k in range(D_FF // FF_CHUNK):
            gate = _bdot(h_ref[r, :], wgu_ref[:, k * FF_CHUNK:(k + 1) * FF_CHUNK])
            up = _bdot(h_ref[r, :], wgu_ref[:, D_FF + k * FF_CHUNK:D_FF + (k + 1) * FF_CHUNK])
            a_ref[r, k * FF_CHUNK:(k + 1) * FF_CHUNK] = (
                gate * jax.nn.sigmoid(gate) * up).astype(BF16)
    for r in rows:
        f = _bdot(a_ref[r, :], wd_ref[...])
        o_ref[r, :] = o_ref[r, :] + _rms(f, g_ref[3:4, :])


def _out_ffn(x2d, y0, y1, w_out, mix_layer, g, w_gu, w_down, layer):
    m = x2d.shape[0]
    tile = lambda n: pl.BlockSpec((ROW_TILE, n), lambda i: (i, 0))
    return pl.pallas_call(
        _out_ffn_kernel,
        out_shape=jax.ShapeDtypeStruct((m, D_MODEL), F32),
        grid=(m // ROW_TILE,),
        in_specs=[tile(D_MODEL), tile(HALF), tile(HALF),
                  _resident((None, D_MODEL, D_MODEL), lambda i: (mix_layer, 0, 0)),
                  pl.BlockSpec((None, 4, D_MODEL), lambda i: (layer, 0, 0)),
                  _resident((None, D_MODEL, 2 * D_FF), lambda i: (layer, 0, 0)),
                  _resident((None, D_FF, D_MODEL), lambda i: (layer, 0, 0))],
        out_specs=tile(D_MODEL),
        scratch_shapes=[pltpu.VMEM((ROW_TILE, D_MODEL), BF16),
                        pltpu.VMEM((ROW_TILE, D_FF), BF16)],
        compiler_params=_params(1, FFN_VMEM_LIMIT),
        name="out_proj_ffn",
    )(x2d, y0, y1, w_out, g, w_gu, w_down)


def kernel(x, norm_g, ffn_w_gu, ffn_w_down, ab_w_in, ab_w_out, a_ln_g, a_w_s, a_b_s, b_conv_w, b_filt_w1, b_filt_b1, b_filt_freq, b_filt_w2, b_filt_b2, b_filt_w3, b_decay, b_skip, cd_w_in, cd_w_out, c_w, c_scale, d_conv_w):
    bsz, seq, d = x.shape
    assert (seq, d) == (SEQ, D_MODEL) and norm_g.shape[0] == DEPTH
    m = bsz * seq
    bands = jnp.linspace(1e-4, FILTER_BANDS - 1, FILTER_BANDS, dtype=F32)[None, :]
    csv, csh = _dft_tables()
    x2d = x.reshape(m, d)
    ab_w_in, ab_w_out, cd_w_in, cd_w_out, ffn_w_gu, ffn_w_down = (
        w.astype(BF16) for w in (ab_w_in, ab_w_out, cd_w_in, cd_w_out, ffn_w_gu, ffn_w_down))
    for i in range(DEPTH):
        j = i // 2
        g = norm_g[i]
        if i % 2 == 0:
            proj = _in_proj(x2d, g[0:1], ab_w_in, j, 2 * HALF).reshape(bsz, seq, -1)
            y0 = _gmlp(proj, a_ln_g[j][None, :], a_w_s[j].astype(BF16), a_b_s[j][:, :, None])
            hidden = _filter_hidden(bands, b_filt_w1[j], b_filt_b1[j][None, :], b_filt_freq[j],
                                    b_filt_w2[j], b_filt_b2[j][None, :])
            hre, him, hny = _filter_spectra(hidden, b_filt_w3[j], b_decay[j][None, :], csv)
            y1 = _hyena(proj, b_conv_w[j], b_skip[j], hre, him, hny, csv, csh)
            w_out = ab_w_out
        else:
            proj = _in_proj(x2d, g[0:1], cd_w_in, j, 0).reshape(bsz, seq, -1)
            y0, y1 = _pool_conv(proj, c_w[j].astype(BF16), c_scale[j][None, :], d_conv_w[j])
            w_out = cd_w_out
        x2d = _out_ffn(x2d, y0.reshape(m, HALF), y1.reshape(m, HALF), w_out, j, norm_g,
                       ffn_w_gu, ffn_w_down, i)
    return x2d.reshape(bsz, seq, d)
```

```python
import functools
import math

import jax
import jax.numpy as jnp
from jax import lax
from jax.experimental import pallas as pl
from jax.experimental.pallas import tpu as pltpu

F32 = jnp.float32
BF16 = jnp.bfloat16

D_MODEL = 1024
SEQ = 2048
DEPTH = 4
HALF = D_MODEL // 2
CHUNK = 128
N_CHUNKS = SEQ // CHUNK
A_GROUPS = 4
GROUP_DIM = HALF // A_GROUPS
FILTER_BANDS = 16
FILTER_HIDDEN = 64
POOL_WINDOWS = (2, 4, 8, 16)
D_FF = 2816
RMS_EPS = 1e-6
LN_EPS = 1e-5
HY_P = 512
HY_NB = SEQ // HY_P
HY_LAGS = 2 * HY_NB - 1
NFFT = 2 * HY_P
HY_NQ = 2
HY_FQ = HY_P // HY_NQ
HY_COEF = 9

ROW_TILE = 1024
FFN_SUBTILES = 4
FFN_VMEM_LIMIT = 60 * 1024 * 1024
IN_TILE = 2048
IN_SUBTILES = 4
IN_CHUNK = 256
FF_CHUNK = 256
HY_CT = 256
MIX_ROWS = 16
MIX_LANES = 128
PW_ROWS = 128
HALO = 8
POOL_TILE = 256
POOL_K = 512
POOL_EDGE = 16
VMEM_LIMIT = 56 * 1024 * 1024


def _params(n_axes, vmem=VMEM_LIMIT, flags=None):
    return pltpu.CompilerParams(
        dimension_semantics=("arbitrary",) * n_axes, vmem_limit_bytes=vmem, flags=flags)


def _resident(shape, index_map):
    return pl.BlockSpec(shape, index_map, pipeline_mode=pl.Buffered(1))


def _rms(x, g):
    return x * lax.rsqrt(jnp.mean(x * x, axis=-1, keepdims=True) + RMS_EPS) * g


def _bdot(a, b):
    return jnp.dot(a, b, preferred_element_type=F32)


def _fdot(a, b):
    return jnp.dot(a, b, preferred_element_type=F32, precision=lax.Precision.HIGHEST)


def _shift_rows(x, k, row):
    n = x.shape[0]
    y = pltpu.roll(x, k % n, axis=0)
    if k > 0:
        return jnp.where(row >= k, y, 0.0)
    return jnp.where(row < n + k, y, 0.0)


def _conv3(x, w, row):
    return (_shift_rows(x, 1, row) * w[0:1, :] + x * w[1:2, :]
            + _shift_rows(x, -1, row) * w[2:3, :])


def _dft_kernel(csv_ref, csh_ref):
    P, FQ = HY_P, HY_FQ
    f = lax.broadcasted_iota(jnp.int32, (P, 128), 0)
    lane = lax.broadcasted_iota(jnp.int32, (P, 128), 1)
    scale = 2.0 * math.pi / NFFT
    ang_b = ((f * lane) & (NFFT - 1)).astype(F32) * scale
    ang_a = ((f * (lane * 128)) & (NFFT - 1)).astype(F32) * scale
    cb, sb = jnp.cos(ang_b), jnp.sin(ang_b)
    ca, sa = jnp.cos(ang_a), jnp.sin(ang_a)
    for a in range(P // 128):
        cols = slice(a * 128, (a + 1) * 128)
        ca_a = ca[:, a:a + 1]
        sa_a = sa[:, a:a + 1]
        c = (ca_a * cb - sa_a * sb).astype(BF16)
        s = sa_a * cb + ca_a * sb
        s_v = jnp.where(f == 0, jnp.where((lane & 1) == 0, 1.0, -1.0), s).astype(BF16)
        for q in range(HY_NQ):
            csv_ref[2 * q * FQ:(2 * q + 1) * FQ, cols] = c[q * FQ:(q + 1) * FQ]
            csv_ref[(2 * q + 1) * FQ:(2 * q + 2) * FQ, cols] = s_v[q * FQ:(q + 1) * FQ]
        s_h = s
        if a == 0:
            s_h = jnp.where(lane == 0, jnp.where((f & 1) == 0, 1.0, -1.0), s)
        q, off = divmod(a * 128, FQ)
        csh_ref[q, :, off:off + 128] = c
        csh_ref[q, :, FQ + off:FQ + off + 128] = s_h.astype(BF16)


def _dft_tables():
    return pl.pallas_call(
        _dft_kernel,
        out_shape=(jax.ShapeDtypeStruct((2 * HY_P, HY_P), BF16),
                   jax.ShapeDtypeStruct((HY_NQ, HY_P, 2 * HY_FQ), BF16)),
        compiler_params=_params(0),
        name="dft_tables",
    )()


def _filter_hidden_kernel(bands_ref, w1_ref, b1_ref, freq_ref, w2_ref, b2_ref, h_ref):
    L = SEQ
    r = lax.broadcasted_iota(jnp.int32, (L, 1), 0)
    t = jnp.where(pl.program_id(0) == 0, r, L - 1 - r).astype(F32)
    t01 = t / (L - 1)
    fw = (2 * math.pi * t / L) * bands_ref[...]
    pre = (t01 * w1_ref[0:1, :] + _fdot(jnp.cos(fw), w1_ref[1:1 + FILTER_BANDS, :])
           + _fdot(-jnp.sin(fw), w1_ref[1 + FILTER_BANDS:, :]) + b1_ref[...])
    h = jnp.sin(freq_ref[0:1, :] * pre)
    h_ref[...] = jnp.sin(freq_ref[1:2, :] * (_fdot(h, w2_ref[...]) + b2_ref[...]))


def _filter_hidden(bands, w1, b1, freq, w2, b2):
    const = lambda s: (0, 0)
    return pl.pallas_call(
        _filter_hidden_kernel,
        out_shape=jax.ShapeDtypeStruct((2, SEQ, FILTER_HIDDEN), F32),
        grid=(2,),
        in_specs=[pl.BlockSpec((1, FILTER_BANDS), const),
                  pl.BlockSpec((1 + 2 * FILTER_BANDS, FILTER_HIDDEN), const),
                  pl.BlockSpec((1, FILTER_HIDDEN), const),
                  pl.BlockSpec((2, FILTER_HIDDEN), const),
                  pl.BlockSpec((FILTER_HIDDEN, FILTER_HIDDEN), const),
                  pl.BlockSpec((1, FILTER_HIDDEN), const)],
        out_specs=pl.BlockSpec((None, SEQ, FILTER_HIDDEN), lambda s: (s, 0, 0)),
        compiler_params=_params(1),
        name="hyena_filter_hidden",
    )(bands, w1, b1, freq, w2, b2)


def _filter_kernel(hid_ref, w3f_ref, w3b_ref, decf_ref, decb_ref, csv_ref,
                   kre_ref, kim_ref, hny_ref):
    L, P, NB = SEQ, HY_P, HY_NB
    ct = kre_ref.shape[-1]
    row = lax.broadcasted_iota(jnp.int32, (L, ct), 0)
    t_asc = row.astype(F32) / (L - 1)
    t_dsc = (L - 1 - row).astype(F32) / (L - 1)
    dec_f = jnp.abs(decf_ref[...])
    dec_b = jnp.abs(decb_ref[...])
    h_asc = hid_ref[0]
    h_dsc = hid_ref[1]
    inner = (row & (P - 1)) != 0
    f_asc = _fdot(h_asc, w3f_ref[...]) * jnp.exp(-t_asc * dec_f)
    b_dsc = _fdot(h_dsc, w3b_ref[...]) * jnp.exp(-t_dsc * dec_b)
    f_dsc = _fdot(h_dsc, w3f_ref[...]) * jnp.exp(-t_dsc * dec_f)
    f_dsc = jnp.where(inner, _shift_rows(f_dsc, 1, row), 0.0)
    b_asc = _fdot(h_asc, w3b_ref[...]) * jnp.exp(-t_asc * dec_b)
    b_asc = jnp.where(inner, _shift_rows(b_asc, 1, row), 0.0)

    def blk(x, i):
        return x[i * P:(i + 1) * P, :]

    frow = lax.broadcasted_iota(jnp.int32, (P, ct), 0)
    wgt = jnp.where(frow == 0, 1.0 / NFFT, 2.0 / NFFT)
    sign = jnp.where((frow & 1) == 0, 1.0, -1.0)
    FQ = HY_FQ
    h = {}
    for idx in range(HY_LAGS):
        d = idx - (NB - 1)
        if d >= 1:
            kp, kn = blk(f_asc, d), blk(f_dsc, NB - d)
        elif d == 0:
            kp, kn = blk(f_asc, 0), blk(b_asc, 0)
        else:
            kp, kn = blk(b_dsc, NB + d), blk(b_asc, -d)
        ev = kp + kn
        od = kp - kn
        evb = ev.astype(BF16)
        odb = od.astype(BF16)
        re = jnp.concatenate([_bdot(csv_ref[2 * q * FQ:(2 * q + 1) * FQ, :], evb)
                              for q in range(HY_NQ)], axis=0)
        im = jnp.concatenate([_bdot(csv_ref[(2 * q + 1) * FQ:(2 * q + 2) * FQ, :], odb)
                              for q in range(HY_NQ)], axis=0)
        h[d] = (re * wgt, jnp.where(frow == 0, 0.0, -im * wgt))
        hny_ref[idx:idx + 1, :] = jnp.sum(ev * sign, axis=0, keepdims=True) * (1.0 / NFFT)

    def sub(a, b):
        return (a[0] - b[0], a[1] - b[1])

    def coef3(a0, a_dn, a_up):
        return [a0, sub(a_dn, a0), sub(a_up, a0)]

    coefs = (coef3(h[0], h[1], h[-1])
             + coef3(sub(h[2], h[0]), sub(h[3], h[1]), sub(h[1], h[-1]))
             + coef3(sub(h[-2], h[0]), sub(h[-1], h[1]), sub(h[-3], h[-1])))
    for c, (re, im) in enumerate(coefs):
        kre_ref[c] = re
        kim_ref[c] = im


def _filter_spectra(hidden, w3, decay, csv):
    ct = HY_CT
    nct = HALF // ct
    fcol = lambda o, c: (0, o * nct + c)
    bcol = lambda o, c: (0, (2 + o) * nct + c)
    hspec = pl.BlockSpec((None, HY_COEF, HY_P, ct), lambda o, c: (o, 0, 0, c))
    return pl.pallas_call(
        _filter_kernel,
        out_shape=(jax.ShapeDtypeStruct((2, HY_COEF, HY_P, HALF), F32),
                   jax.ShapeDtypeStruct((2, HY_COEF, HY_P, HALF), F32),
                   jax.ShapeDtypeStruct((2, HY_LAGS, HALF), F32)),
        grid=(2, nct),
        in_specs=[
            pl.BlockSpec((2, SEQ, FILTER_HIDDEN), lambda o, c: (0, 0, 0)),
            pl.BlockSpec((FILTER_HIDDEN, ct), fcol),
            pl.BlockSpec((FILTER_HIDDEN, ct), bcol),
            pl.BlockSpec((1, ct), fcol),
            pl.BlockSpec((1, ct), bcol),
            pl.BlockSpec((2 * HY_P, HY_P), lambda o, c: (0, 0)),
        ],
        out_specs=(hspec, hspec,
                   pl.BlockSpec((None, HY_LAGS, ct), lambda o, c: (o, 0, c))),
        compiler_params=_params(2),
        name="hyena_filter_spectra",
    )(hidden, w3, w3, decay, decay, csv)


def _in_proj_kernel(x_ref, g_ref, w_ref, o_ref, h_ref, *, gelu_cols):
    sub = IN_TILE // IN_SUBTILES
    rows = [slice(s * sub, (s + 1) * sub) for s in range(IN_SUBTILES)]
    for r in rows:
        h_ref[r, :] = _rms(x_ref[r, :], g_ref[...]).astype(BF16)
    n_out = o_ref.shape[1]
    gelu_chunks = list(range(0, gelu_cols, IN_CHUNK))
    plain_chunks = list(range(gelu_cols, n_out, IN_CHUNK))
    order = [c for pair in zip(gelu_chunks, plain_chunks) for c in pair]
    order += gelu_chunks[len(plain_chunks):] + plain_chunks[len(gelu_chunks):]
    for r in rows:
        for c0 in order:
            p = _bdot(h_ref[r, :], w_ref[:, c0:c0 + IN_CHUNK])
            if c0 < gelu_cols:
                p = jax.nn.gelu(p)
            o_ref[r, c0:c0 + IN_CHUNK] = p.astype(BF16)


def _in_proj(x2d, g, w, layer, gelu_cols):
    m = x2d.shape[0]
    n_out = w.shape[2]
    return pl.pallas_call(
        functools.partial(_in_proj_kernel, gelu_cols=gelu_cols),
        out_shape=jax.ShapeDtypeStruct((m, n_out), BF16),
        grid=(m // IN_TILE,),
        in_specs=[pl.BlockSpec((IN_TILE, D_MODEL), lambda i: (i, 0)),
                  pl.BlockSpec((1, D_MODEL), lambda i: (0, 0)),
                  _resident((None, D_MODEL, n_out), lambda i: (layer, 0, 0))],
        out_specs=pl.BlockSpec((IN_TILE, n_out), lambda i: (i, 0)),
        scratch_shapes=[pltpu.VMEM((IN_TILE, D_MODEL), BF16)],
        compiler_params=_params(1),
        name="rms_in_proj",
    )(x2d, g, w)


def _gmlp_kernel(za_ref, lng_ref, ws_ref, bs_ref, o_ref, vn_ref):
    def ln_body(i, carry):
        r = pl.ds(pl.multiple_of(i * PW_ROWS, PW_ROWS), PW_ROWS)
        v = za_ref[r, HALF:].astype(F32)
        xc = v - jnp.mean(v, axis=-1, keepdims=True)
        y = xc * lax.rsqrt(jnp.mean(xc * xc, axis=-1, keepdims=True) + LN_EPS) * lng_ref[...]
        vn_ref[r, :] = y.astype(BF16)
        return carry
    lax.fori_loop(0, SEQ // PW_ROWS, ln_body, 0, unroll=4)

    for g in range(A_GROUPS):
        cols = slice(g * GROUP_DIM, (g + 1) * GROUP_DIM)
        vg = jnp.concatenate(
            [vn_ref[n * CHUNK:(n + 1) * CHUNK, cols] for n in range(N_CHUNKS)], axis=1)
        s = _bdot(ws_ref[g], vg) + bs_ref[g]
        for n in range(N_CHUNKS):
            rows = slice(n * CHUNK, (n + 1) * CHUNK)
            u = za_ref[rows, cols].astype(F32)
            o_ref[rows, cols] = (u * s[:, n * GROUP_DIM:(n + 1) * GROUP_DIM]).astype(BF16)


def _gmlp(proj3d, ln_g, w_s, b_s):
    bsz = proj3d.shape[0]
    return pl.pallas_call(
        _gmlp_kernel,
        out_shape=jax.ShapeDtypeStruct((bsz, SEQ, HALF), BF16),
        grid=(bsz,),
        in_specs=[pl.BlockSpec((None, SEQ, 2 * HALF), lambda b: (b, 0, 0)),
                  pl.BlockSpec((1, HALF), lambda b: (0, 0)),
                  pl.BlockSpec((A_GROUPS, CHUNK, CHUNK), lambda b: (0, 0, 0)),
                  pl.BlockSpec((A_GROUPS, CHUNK, 1), lambda b: (0, 0, 0))],
        out_specs=pl.BlockSpec((None, SEQ, HALF), lambda b: (b, 0, 0)),
        scratch_shapes=[pltpu.VMEM((SEQ, HALF), BF16)],
        compiler_params=_params(1),
        name="gmlp_spatial_gating",
    )(proj3d, ln_g, w_s, b_s)


def _cmul(z, h):
    return z[0] * h[0] + z[1] * h[1], z[1] * h[0] - z[0] * h[1]


def _cadd(a, b):
    return a[0] + b[0], a[1] + b[1]


def _mix_tile(o, q, row0, lh, zs_ref, kre_ref, kim_ref, ys_ref):
    FQ = HY_FQ
    fr = slice(q * FQ + row0, q * FQ + row0 + MIX_ROWS)
    rr = slice(row0, row0 + MIX_ROWS)
    rm = slice(FQ + row0, FQ + row0 + MIX_ROWS)
    lanes = slice(lh * MIX_LANES, (lh + 1) * MIX_LANES)

    def coef(c):
        return kre_ref[o, c, fr, lanes], kim_ref[o, c, fr, lanes]

    def toep2(c, v0, v1):
        q1 = _cmul(_cadd(v0, v1), coef(c))
        q2 = _cmul(v0, coef(c + 1))
        q3 = _cmul(v1, coef(c + 2))
        return _cadd(q1, q3), _cadd(q1, q2)

    z = [(zs_ref[j, rr, lanes], zs_ref[j, rm, lanes]) for j in range(HY_NB)]
    p1 = toep2(0, _cadd(z[0], z[2]), _cadd(z[1], z[3]))
    p2 = toep2(3, z[0], z[1])
    p3 = toep2(6, z[2], z[3])
    y = [_cadd(p1[0], p3[0]), _cadd(p1[1], p3[1]), _cadd(p1[0], p2[0]), _cadd(p1[1], p2[1])]
    for i in range(HY_NB):
        ys_ref[i, rr, lanes] = y[i][0].astype(BF16)
        ys_ref[i, rm, lanes] = y[i][1].astype(BF16)


def _conv3_rows(src_ref, w, start):
    L, R, T = SEQ, PW_ROWS, 16
    if isinstance(start, int):
        at = lambda s, n: slice(s, s + n)
        lo, hi = max(start - T, 0), min(start + R, L - T)
    else:
        at = lambda s, n: pl.ds(pl.multiple_of(s, T), n)
        lo, hi = jnp.maximum(start - T, 0), jnp.minimum(start + R, L - T)
    x = src_ref[at(start, R), :].astype(F32)
    prev = jnp.where(start > 0, src_ref[at(lo, T), :].astype(F32), 0.0)
    nxt = jnp.where(start < L - R, src_ref[at(hi, T), :].astype(F32), 0.0)
    a = jnp.concatenate([prev, x, nxt], axis=0)
    n = R + 2 * T
    y = pltpu.roll(a, 1, axis=0) * w[0:1, :] + a * w[1:2, :] + pltpu.roll(a, n - 1, axis=0) * w[2:3, :]
    return y[T:T + R]


def _hyena_kernel(v_ref, g1_ref, g2_ref, wv_ref, wg1_ref, wg2_ref, skip_ref,
                  kre_ref, kim_ref, hny_ref, csv_ref, csh_ref, o_ref,
                  z_ref, gate_ref, zcat_ref, zs0_ref, zs1_ref, ys0_ref, ys1_ref, acc_ref):
    P, NB, FQ, R = HY_P, HY_NB, HY_FQ, PW_ROWS
    ct = o_ref.shape[1]
    steps = FQ // MIX_ROWS // NB

    def chunk(q):
        return slice(2 * q * FQ, 2 * (q + 1) * FQ)

    def mix_share(o, q, k, zs_ref, ys_ref):
        for s in range(steps):
            for lh in range(ct // MIX_LANES):
                _mix_tile(o, q, (k * steps + s) * MIX_ROWS, lh, zs_ref, kre_ref, kim_ref, ys_ref)

    for j in range(NB):
        for r in range(P // R):
            zr = _conv3_rows(v_ref, wv_ref[...], j * P + r * R)
            z_ref[j * P + r * R:j * P + (r + 1) * R, :] = zr
            zcat_ref[j, r * R:(r + 1) * R, :] = zr.astype(BF16)

    def always(k):
        return pl.program_id(1) + k >= 0

    def phase_a(o, g_ref, wg_ref):
        for j in range(NB):
            zs0_ref[j] = _bdot(csv_ref[chunk(0), :], zcat_ref[j])
            for r in range(P // R):
                start = j * P + r * R
                gate_ref[start:start + R, :] = _conv3_rows(g_ref, wg_ref[...], start)

    def phase_b(o):
        for j in range(NB):
            zs1_ref[j] = _bdot(csv_ref[chunk(1), :], zcat_ref[j])
            mix_share(o, 0, j, zs0_ref, ys0_ref)
        tile = slice(FQ, FQ + MIX_ROWS)
        first = lax.broadcasted_iota(jnp.int32, (MIX_ROWS, ct), 0) == 0
        z_ny = [zs0_ref[j, tile, :] for j in range(NB)]
        for i in range(NB):
            y_ny = sum(z_ny[j] * hny_ref[o, i - j + NB - 1:i - j + NB, :] for j in range(NB))
            ys0_ref[i, tile, :] = jnp.where(
                first, y_ny, ys0_ref[i, tile, :].astype(F32)).astype(BF16)

    def phase_c(o):
        for i in range(NB):
            acc_ref[i] = _bdot(csh_ref[0], ys0_ref[i])
            mix_share(o, 1, i, zs1_ref, ys1_ref)

    def phase_d(o):
        for i in range(NB):
            rows = slice(i * P, (i + 1) * P)
            conv = acc_ref[i] + _bdot(csh_ref[1], ys1_ref[i])
            z_new = gate_ref[rows, :] * (conv + z_ref[rows, :] * skip_ref[o:o + 1, :])
            if o == 0:
                z_ref[rows, :] = z_new
                zcat_ref[i] = z_new.astype(BF16)
            else:
                o_ref[rows, :] = z_new.astype(BF16)

    for o, (g_ref, wg_ref) in enumerate(((g1_ref, wg1_ref), (g2_ref, wg2_ref))):
        phases = (functools.partial(phase_a, o, g_ref, wg_ref), functools.partial(phase_b, o),
                  functools.partial(phase_c, o), functools.partial(phase_d, o))
        for k, phase in enumerate(phases):
            pl.when(always(len(phases) * o + k))(phase)


def _hyena(proj3d, conv_w, skip, hre, him, hny, csv, csh):
    bsz = proj3d.shape[0]
    ct = HY_CT
    nct = HALF // ct
    base = 2 * HALF // ct

    def slab(part):
        return pl.BlockSpec((None, SEQ, ct), lambda c, b: (b, 0, base + part * nct + c))

    def cw(part):
        return pl.BlockSpec((3, ct), lambda c, b: (0, part * nct + c))

    assert HY_NQ == 2 and HY_FQ % (MIX_ROWS * HY_NB) == 0 and ct % MIX_LANES == 0
    hspec = _resident((2, HY_COEF, HY_P, ct), lambda c, b: (0, 0, 0, c))
    const = lambda c, b: (0, 0)
    return pl.pallas_call(
        _hyena_kernel,
        out_shape=jax.ShapeDtypeStruct((bsz, SEQ, HALF), BF16),
        grid=(nct, bsz),
        in_specs=[slab(0), slab(1), slab(2), cw(0), cw(1), cw(2),
                  pl.BlockSpec((2, ct), lambda c, b: (0, c)),
                  hspec, hspec,
                  pl.BlockSpec((2, HY_LAGS, ct), lambda c, b: (0, 0, c)),
                  pl.BlockSpec((2 * HY_P, HY_P), const), pl.BlockSpec((HY_NQ, HY_P, 2 * HY_FQ), lambda c, b: (0, 0, 0))],
        out_specs=pl.BlockSpec((None, SEQ, ct), lambda c, b: (b, 0, c)),
        scratch_shapes=[pltpu.VMEM((SEQ, ct), F32), pltpu.VMEM((SEQ, ct), F32),
                        pltpu.VMEM((HY_NB, HY_P, ct), BF16),
                        pltpu.VMEM((HY_NB, 2 * HY_FQ, ct), F32),
                        pltpu.VMEM((HY_NB, 2 * HY_FQ, ct), F32),
                        pltpu.VMEM((HY_NB, 2 * HY_FQ, ct), BF16),
                        pltpu.VMEM((HY_NB, 2 * HY_FQ, ct), BF16),
                        pltpu.VMEM((HY_NB, HY_P, ct), F32)],
        compiler_params=_params(2),
        name="hyena_long_conv",
    )(proj3d, proj3d, proj3d, conv_w, conv_w, conv_w, skip, hre, him, hny, csv, csh)


def _window_sum(a, w):
    n = a.shape[0]
    right = w - w // 2 - 1
    s = a
    k = 1
    while k < w:
        s = s + pltpu.roll(s, k, axis=0)
        k *= 2
    return pltpu.roll(s, n - right, axis=0) if right else s


def _pool_conv_kernel(p_ref, b_ref, c_ref, h_ref, cw_ref, scale_ref, dw_ref, yc_ref, yd_ref,
                      chs_ref, pooled_ref, band_ref):
    L, R, n = SEQ, PW_ROWS, PW_ROWS + 2 * HALO
    offs = (0, (POOL_K - POOL_TILE) // 2, POOL_K - POOL_TILE)

    @pl.when(pl.program_id(0) == 0)
    def _():
        r = lax.broadcasted_iota(jnp.int32, (POOL_TILE, POOL_K), 0)
        c = lax.broadcasted_iota(jnp.int32, (POOL_TILE, POOL_K), 1)
        for g, w in enumerate(POOL_WINDOWS):
            left = w // 2
            for kind, off in enumerate(offs):
                d = c - off - r + left
                m = jnp.where((d >= 0) & (d < w), 1.0 / w, 0.0) - jnp.where(d == left, 1.0, 0.0)
                band_ref[g, kind] = m.astype(BF16)

    zeros = jnp.zeros((HALO, HALF), F32)
    chs_ref[0:HALO, :] = zeros
    chs_ref[L + HALO:L + 2 * HALO, :] = zeros

    for i in range(L // R):
        r = slice(i * R, (i + 1) * R)
        chs_ref[HALO + i * R:HALO + (i + 1) * R, :] = (
            c_ref[r, :].astype(F32) * h_ref[r, :].astype(F32))

    n_tiles = L // POOL_TILE
    for g, w in enumerate(POOL_WINDOWS):
        cols = slice(g * GROUP_DIM, (g + 1) * GROUP_DIM)
        left = w // 2
        right = w - left - 1

        for k in range(n_tiles):
            kind = 0 if k == 0 else (2 if k == n_tiles - 1 else 1)
            t0 = k * POOL_TILE
            ws = t0 - offs[kind]
            pooled_ref[g, t0:t0 + POOL_TILE, :] = _bdot(
                band_ref[g, kind], p_ref[ws:ws + POOL_K, cols]).astype(BF16)

        for top in (True, False):
            pad = jnp.zeros((POOL_EDGE, GROUP_DIM), F32)
            if top:
                a = jnp.concatenate([pad, p_ref[0:2 * POOL_EDGE, cols].astype(F32)], axis=0)
                first = 0
            else:
                a = jnp.concatenate([p_ref[L - 2 * POOL_EDGE:L, cols].astype(F32), pad], axis=0)
                first = L - POOL_EDGE
            keep = slice(POOL_EDGE, 2 * POOL_EDGE)
            t = first + lax.broadcasted_iota(jnp.int32, (POOL_EDGE, GROUP_DIM), 0)
            cnt = (jnp.minimum(t + right, L - 1) - jnp.maximum(t - left, 0) + 1).astype(F32)
            pooled = _window_sum(a, w)[keep] / cnt - a[keep]
            pooled_ref[g, first:first + POOL_EDGE, :] = pooled.astype(BF16)

        dw = dw_ref[:, cols]
        for i in range(L // R):
            r0 = i * R
            e = chs_ref[r0:r0 + n, cols]
            conv = (pltpu.roll(e, 1, axis=0) * dw[0:1, :] + e * dw[1:2, :]
                    + pltpu.roll(e, n - 1, axis=0) * dw[2:3, :])
            yd = b_ref[r0:r0 + R, cols].astype(F32) * conv[HALO:HALO + R]
            yd_ref[r0:r0 + R, cols] = yd.astype(BF16)

        y = _bdot(pooled_ref[g], cw_ref[g]) * scale_ref[:, cols]
        yc_ref[:, cols] = y.astype(BF16)


def _pool_conv(proj3d, c_w, c_scale, d_conv_w):
    bsz = proj3d.shape[0]

    def part(k):
        return pl.BlockSpec((None, SEQ, HALF), lambda b: (b, 0, k))

    out = pl.BlockSpec((None, SEQ, HALF), lambda b: (b, 0, 0))
    return pl.pallas_call(
        _pool_conv_kernel,
        out_shape=(jax.ShapeDtypeStruct((bsz, SEQ, HALF), BF16),) * 2,
        grid=(bsz,),
        in_specs=[part(0), part(1), part(2), part(3),
                  pl.BlockSpec((len(POOL_WINDOWS), GROUP_DIM, GROUP_DIM), lambda b: (0, 0, 0)),
                  pl.BlockSpec((1, HALF), lambda b: (0, 0)),
                  pl.BlockSpec((3, HALF), lambda b: (0, 0))],
        out_specs=(out, out),
        scratch_shapes=[pltpu.VMEM((SEQ + 2 * HALO, HALF), F32),
                        pltpu.VMEM((len(POOL_WINDOWS), SEQ, GROUP_DIM), BF16),
                        pltpu.VMEM((len(POOL_WINDOWS), 3, POOL_TILE, POOL_K), BF16)],
        compiler_params=_params(1),
        name="pool_short_conv",
    )(proj3d, proj3d, proj3d, proj3d, c_w, c_scale, d_conv_w)


def _out_ffn_kernel(x_ref, y0_ref, y1_ref, wo_ref, g_ref, wgu_ref, wd_ref, o_ref, h_ref, a_ref):
    sub = ROW_TILE // FFN_SUBTILES
    rows = [slice(s * sub, (s + 1) * sub) for s in range(FFN_SUBTILES)]
    for r in rows:
        mix = _bdot(y0_ref[r, :], wo_ref[:HALF, :]) + _bdot(y1_ref[r, :], wo_ref[HALF:, :])
        o_ref[r, :] = x_ref[r, :] + _rms(mix, g_ref[1:2, :])
        h_ref[r, :] = _rms(o_ref[r, :], g_ref[2:3, :]).astype(BF16)
    for r in rows:
        for k in range(D_FF // FF_CHUNK):
            gate = _bdot(h_ref[r, :], wgu_ref[:, k * FF_CHUNK:(k + 1) * FF_CHUNK])
            up = _bdot(h_ref[r, :], wgu_ref[:, D_FF + k * FF_CHUNK:D_FF + (k + 1) * FF_CHUNK])
            a_ref[r, k * FF_CHUNK:(k + 1) * FF_CHUNK] = (
                gate * jax.nn.sigmoid(gate) * up).astype(BF16)
    for r in rows:
        f = _bdot(a_ref[r, :], wd_ref[...])
        o_ref[r, :] = o_ref[r, :] + _rms(f, g_ref[3:4, :])


def _out_ffn(x2d, y0, y1, w_out, mix_layer, g, w_gu, w_down, layer):
    m = x2d.shape[0]
    tile = lambda n: pl.BlockSpec((ROW_TILE, n), lambda i: (i, 0))
    return pl.pallas_call(
        _out_ffn_kernel,
        out_shape=jax.ShapeDtypeStruct((m, D_MODEL), F32),
        grid=(m // ROW_TILE,),
        in_specs=[tile(D_MODEL), tile(HALF), tile(HALF),
                  _resident((None, D_MODEL, D_MODEL), lambda i: (mix_layer, 0, 0)),
                  pl.BlockSpec((None, 4, D_MODEL), lambda i: (layer, 0, 0)),
                  _resident((None, D_MODEL, 2 * D_FF), lambda i: (layer, 0, 0)),
                  _resident((None, D_FF, D_MODEL), lambda i: (layer, 0, 0))],
        out_specs=tile(D_MODEL),
        scratch_shapes=[pltpu.VMEM((ROW_TILE, D_MODEL), BF16),
                        pltpu.VMEM((ROW_TILE, D_FF), BF16)],
        compiler_params=_params(1, FFN_VMEM_LIMIT),
        name="out_proj_ffn",
    )(x2d, y0, y1, w_out, g, w_gu, w_down)


def kernel(x, norm_g, ffn_w_gu, ffn_w_down, ab_w_in, ab_w_out, a_ln_g, a_w_s, a_b_s, b_conv_w, b_filt_w1, b_filt_b1, b_filt_freq, b_filt_w2, b_filt_b2, b_filt_w3, b_decay, b_skip, cd_w_in, cd_w_out, c_w, c_scale, d_conv_w):
    bsz, seq, d = x.shape
    assert (seq, d) == (SEQ, D_MODEL) and norm_g.shape[0] == DEPTH
    m = bsz * seq
    bands = jnp.linspace(1e-4, FILTER_BANDS - 1, FILTER_BANDS, dtype=F32)[None, :]
    csv, csh = _dft_tables()
    x2d = x.reshape(m, d)
    ab_w_in, ab_w_out, cd_w_in, cd_w_out, ffn_w_gu, ffn_w_down = (
        w.astype(BF16) for w in (ab_w_in, ab_w_out, cd_w_in, cd_w_out, ffn_w_gu, ffn_w_down))
    for i in range(DEPTH):
        j = i // 2
        g = norm_g[i]
        if i % 2 == 0:
            proj = _in_proj(x2d, g[0:1], ab_w_in, j, 2 * HALF).reshape(bsz, seq, -1)
            y0 = _gmlp(proj, a_ln_g[j][None, :], a_w_s[j].astype(BF16), a_b_s[j][:, :, None])
            hidden = _filter_hidden(bands, b_filt_w1[j], b_filt_b1[j][None, :], b_filt_freq[j],
                                    b_filt_w2[j], b_filt_b2[j][None, :])
            hre, him, hny = _filter_spectra(hidden, b_filt_w3[j], b_decay[j][None, :], csv)
            y1 = _hyena(proj, b_conv_w[j], b_skip[j], hre, him, hny, csv, csh)
            w_out = ab_w_out
        else:
            proj = _in_proj(x2d, g[0:1], cd_w_in, j, 0).reshape(bsz, seq, -1)
            y0, y1 = _pool_conv(proj, c_w[j].astype(BF16), c_scale[j][None, :], d_conv_w[j])
            w_out = cd_w_out
        x2d = _out_ffn(x2d, y0.reshape(m, HALF), y1.reshape(m, HALF), w_out, j, norm_g,
                       ffn_w_gu, ffn_w_down, i)
    return x2d.reshape(bsz, seq, d)
```

```python
import functools
import math

import jax
import jax.numpy as jnp
from jax import lax
from jax.experimental import pallas as pl
from jax.experimental.pallas import tpu as pltpu

F32 = jnp.float32
BF16 = jnp.bfloat16
LANES = 128
BF16_ROWS = 16

D_MODEL = 1024
SEQ = 2048
DEPTH = 4
HALF = D_MODEL // 2
CHUNK = 128
N_CHUNKS = SEQ // CHUNK
A_GROUPS = 4
GROUP_DIM = HALF // A_GROUPS
FILTER_BANDS = 16
FILTER_HIDDEN = 64
POOL_WINDOWS = (2, 4, 8, 16)
D_FF = 2816
RMS_EPS = 1e-6
LN_EPS = 1e-5
HY_P = 512
HY_NB = SEQ // HY_P
HY_LAGS = 2 * HY_NB - 1
NFFT = 2 * HY_P
HY_NQ = 2
HY_FQ = HY_P // HY_NQ
HY_COEF = 9

ROW_TILE = 1024
FFN_SUBTILES = 4
FFN_VMEM_LIMIT = 60 * 1024 * 1024
IN_TILE = 2048
IN_SUBTILES = 4
IN_CHUNK = 256
FF_CHUNK = 256
HY_CT = 256
HY_CONV_ROWS = 128
MIX_ROWS = 16
MIX_LANES = 128
PW_ROWS = 128
HALO = 8
POOL_TILE = 256
POOL_K = 512
POOL_EDGE = 16
VMEM_LIMIT = 56 * 1024 * 1024


def _params(n_axes, vmem=VMEM_LIMIT):
    return pltpu.CompilerParams(
        dimension_semantics=("arbitrary",) * n_axes, vmem_limit_bytes=vmem)


def _resident(shape, index_map):
    return pl.BlockSpec(shape, index_map, pipeline_mode=pl.Buffered(1))


def _rms(x, g):
    return x * lax.rsqrt(jnp.mean(x * x, axis=-1, keepdims=True) + RMS_EPS) * g


def _bdot(a, b):
    return jnp.dot(a, b, preferred_element_type=F32)


def _fdot(a, b):
    return jnp.dot(a, b, preferred_element_type=F32, precision=lax.Precision.HIGHEST)


def _shift_rows(x, k, row):
    n = x.shape[0]
    y = pltpu.roll(x, k % n, axis=0)
    if k > 0:
        return jnp.where(row >= k, y, 0.0)
    return jnp.where(row < n + k, y, 0.0)


def _conv3(x, w, row):
    return (_shift_rows(x, 1, row) * w[0:1, :] + x * w[1:2, :]
            + _shift_rows(x, -1, row) * w[2:3, :])


def _dft_kernel(csv_ref, csh_ref):
    P, FQ = HY_P, HY_FQ
    f = lax.broadcasted_iota(jnp.int32, (P, LANES), 0)
    lane = lax.broadcasted_iota(jnp.int32, (P, LANES), 1)
    scale = 2.0 * math.pi / NFFT
    ang_b = ((f * lane) & (NFFT - 1)).astype(F32) * scale
    ang_a = ((f * (lane * LANES)) & (NFFT - 1)).astype(F32) * scale
    cb, sb = jnp.cos(ang_b), jnp.sin(ang_b)
    ca, sa = jnp.cos(ang_a), jnp.sin(ang_a)
    for a in range(P // LANES):
        cols = slice(a * LANES, (a + 1) * LANES)
        ca_a = ca[:, a:a + 1]
        sa_a = sa[:, a:a + 1]
        c = (ca_a * cb - sa_a * sb).astype(BF16)
        s = sa_a * cb + ca_a * sb
        s_v = jnp.where(f == 0, jnp.where((lane & 1) == 0, 1.0, -1.0), s).astype(BF16)
        for q in range(HY_NQ):
            csv_ref[2 * q * FQ:(2 * q + 1) * FQ, cols] = c[q * FQ:(q + 1) * FQ]
            csv_ref[(2 * q + 1) * FQ:(2 * q + 2) * FQ, cols] = s_v[q * FQ:(q + 1) * FQ]
        s_h = s
        if a == 0:
            s_h = jnp.where(lane == 0, jnp.where((f & 1) == 0, 1.0, -1.0), s)
        q, off = divmod(a * LANES, FQ)
        csh_ref[q, :, off:off + LANES] = c
        csh_ref[q, :, FQ + off:FQ + off + LANES] = s_h.astype(BF16)


def _dft_tables():
    return pl.pallas_call(
        _dft_kernel,
        out_shape=(jax.ShapeDtypeStruct((2 * HY_P, HY_P), BF16),
                   jax.ShapeDtypeStruct((HY_NQ, HY_P, 2 * HY_FQ), BF16)),
        compiler_params=_params(0),
        name="dft_tables",
    )()


def _filter_hidden_kernel(bands_ref, w1_ref, b1_ref, freq_ref, w2_ref, b2_ref, h_ref):
    L = SEQ
    r = lax.broadcasted_iota(jnp.int32, (L, 1), 0)
    t = jnp.where(pl.program_id(0) == 0, r, L - 1 - r).astype(F32)
    t01 = t / (L - 1)
    fw = (2 * math.pi * t / L) * bands_ref[...]
    pre = (t01 * w1_ref[0:1, :] + _fdot(jnp.cos(fw), w1_ref[1:1 + FILTER_BANDS, :])
           + _fdot(-jnp.sin(fw), w1_ref[1 + FILTER_BANDS:, :]) + b1_ref[...])
    h = jnp.sin(freq_ref[0:1, :] * pre)
    h_ref[...] = jnp.sin(freq_ref[1:2, :] * (_fdot(h, w2_ref[...]) + b2_ref[...]))


def _filter_hidden(bands, w1, b1, freq, w2, b2):
    const = lambda s: (0, 0)
    return pl.pallas_call(
        _filter_hidden_kernel,
        out_shape=jax.ShapeDtypeStruct((2, SEQ, FILTER_HIDDEN), F32),
        grid=(2,),
        in_specs=[pl.BlockSpec((1, FILTER_BANDS), const),
                  pl.BlockSpec((1 + 2 * FILTER_BANDS, FILTER_HIDDEN), const),
                  pl.BlockSpec((1, FILTER_HIDDEN), const),
                  pl.BlockSpec((2, FILTER_HIDDEN), const),
                  pl.BlockSpec((FILTER_HIDDEN, FILTER_HIDDEN), const),
                  pl.BlockSpec((1, FILTER_HIDDEN), const)],
        out_specs=pl.BlockSpec((None, SEQ, FILTER_HIDDEN), lambda s: (s, 0, 0)),
        compiler_params=_params(1),
        name="hyena_filter_hidden",
    )(bands, w1, b1, freq, w2, b2)


def _filter_kernel(hid_ref, w3f_ref, w3b_ref, decf_ref, decb_ref, csv_ref,
                   kre_ref, kim_ref, hny_ref):
    L, P, NB = SEQ, HY_P, HY_NB
    ct = kre_ref.shape[-1]
    row = lax.broadcasted_iota(jnp.int32, (L, ct), 0)
    t_asc = row.astype(F32) / (L - 1)
    t_dsc = (L - 1 - row).astype(F32) / (L - 1)
    dec_f = jnp.abs(decf_ref[...])
    dec_b = jnp.abs(decb_ref[...])
    h_asc = hid_ref[0]
    h_dsc = hid_ref[1]
    inner = (row & (P - 1)) != 0
    f_asc = _fdot(h_asc, w3f_ref[...]) * jnp.exp(-t_asc * dec_f)
    b_dsc = _fdot(h_dsc, w3b_ref[...]) * jnp.exp(-t_dsc * dec_b)
    f_dsc = _fdot(h_dsc, w3f_ref[...]) * jnp.exp(-t_dsc * dec_f)
    f_dsc = jnp.where(inner, _shift_rows(f_dsc, 1, row), 0.0)
    b_asc = _fdot(h_asc, w3b_ref[...]) * jnp.exp(-t_asc * dec_b)
    b_asc = jnp.where(inner, _shift_rows(b_asc, 1, row), 0.0)

    def blk(x, i):
        return x[i * P:(i + 1) * P, :]

    frow = lax.broadcasted_iota(jnp.int32, (P, ct), 0)
    wgt = jnp.where(frow == 0, 1.0 / NFFT, 2.0 / NFFT)
    sign = jnp.where((frow & 1) == 0, 1.0, -1.0)
    FQ = HY_FQ
    h = {}
    for idx in range(HY_LAGS):
        d = idx - (NB - 1)
        if d >= 1:
            kp, kn = blk(f_asc, d), blk(f_dsc, NB - d)
        elif d == 0:
            kp, kn = blk(f_asc, 0), blk(b_asc, 0)
        else:
            kp, kn = blk(b_dsc, NB + d), blk(b_asc, -d)
        ev = kp + kn
        od = kp - kn
        evb = ev.astype(BF16)
        odb = od.astype(BF16)
        re = jnp.concatenate([_bdot(csv_ref[2 * q * FQ:(2 * q + 1) * FQ, :], evb)
                              for q in range(HY_NQ)], axis=0)
        im = jnp.concatenate([_bdot(csv_ref[(2 * q + 1) * FQ:(2 * q + 2) * FQ, :], odb)
                              for q in range(HY_NQ)], axis=0)
        h[d] = (re * wgt, jnp.where(frow == 0, 0.0, -im * wgt))
        hny_ref[idx:idx + 1, :] = jnp.sum(ev * sign, axis=0, keepdims=True) * (1.0 / NFFT)

    def sub(a, b):
        return (a[0] - b[0], a[1] - b[1])

    def coef3(a0, a_dn, a_up):
        return [a0, sub(a_dn, a0), sub(a_up, a0)]

    coefs = (coef3(h[0], h[1], h[-1])
             + coef3(sub(h[2], h[0]), sub(h[3], h[1]), sub(h[1], h[-1]))
             + coef3(sub(h[-2], h[0]), sub(h[-1], h[1]), sub(h[-3], h[-1])))
    for c, (re, im) in enumerate(coefs):
        kre_ref[c] = re
        kim_ref[c] = im


def _filter_spectra(hidden, w3, decay, csv):
    ct = HY_CT
    nct = HALF // ct
    fcol = lambda o, c: (0, o * nct + c)
    bcol = lambda o, c: (0, (2 + o) * nct + c)
    hspec = pl.BlockSpec((None, HY_COEF, HY_P, ct), lambda o, c: (o, 0, 0, c))
    return pl.pallas_call(
        _filter_kernel,
        out_shape=(jax.ShapeDtypeStruct((2, HY_COEF, HY_P, HALF), F32),
                   jax.ShapeDtypeStruct((2, HY_COEF, HY_P, HALF), F32),
                   jax.ShapeDtypeStruct((2, HY_LAGS, HALF), F32)),
        grid=(2, nct),
        in_specs=[
            pl.BlockSpec((2, SEQ, FILTER_HIDDEN), lambda o, c: (0, 0, 0)),
            pl.BlockSpec((FILTER_HIDDEN, ct), fcol),
            pl.BlockSpec((FILTER_HIDDEN, ct), bcol),
            pl.BlockSpec((1, ct), fcol),
            pl.BlockSpec((1, ct), bcol),
            pl.BlockSpec((2 * HY_P, HY_P), lambda o, c: (0, 0)),
        ],
        out_specs=(hspec, hspec,
                   pl.BlockSpec((None, HY_LAGS, ct), lambda o, c: (o, 0, c))),
        compiler_params=_params(2),
        name="hyena_filter_spectra",
    )(hidden, w3, w3, decay, decay, csv)


def _in_proj_kernel(x_ref, g_ref, w_ref, o_ref, h_ref, *, gelu_cols):
    sub = IN_TILE // IN_SUBTILES
    rows = [slice(s * sub, (s + 1) * sub) for s in range(IN_SUBTILES)]
    for r in rows:
        h_ref[r, :] = _rms(x_ref[r, :], g_ref[...]).astype(BF16)
    n_out = o_ref.shape[1]
    gelu_chunks = list(range(0, gelu_cols, IN_CHUNK))
    plain_chunks = list(range(gelu_cols, n_out, IN_CHUNK))
    order = [c for pair in zip(gelu_chunks, plain_chunks) for c in pair]
    order += gelu_chunks[len(plain_chunks):] + plain_chunks[len(gelu_chunks):]
    for r in rows:
        for c0 in order:
            p = _bdot(h_ref[r, :], w_ref[:, c0:c0 + IN_CHUNK])
            if c0 < gelu_cols:
                p = jax.nn.gelu(p)
            o_ref[r, c0:c0 + IN_CHUNK] = p.astype(BF16)


def _in_proj(x2d, g, w, layer, gelu_cols):
    m = x2d.shape[0]
    n_out = w.shape[2]
    return pl.pallas_call(
        functools.partial(_in_proj_kernel, gelu_cols=gelu_cols),
        out_shape=jax.ShapeDtypeStruct((m, n_out), BF16),
        grid=(m // IN_TILE,),
        in_specs=[pl.BlockSpec((IN_TILE, D_MODEL), lambda i: (i, 0)),
                  pl.BlockSpec((1, D_MODEL), lambda i: (0, 0)),
                  _resident((None, D_MODEL, n_out), lambda i: (layer, 0, 0))],
        out_specs=pl.BlockSpec((IN_TILE, n_out), lambda i: (i, 0)),
        scratch_shapes=[pltpu.VMEM((IN_TILE, D_MODEL), BF16)],
        compiler_params=_params(1),
        name="rms_in_proj",
    )(x2d, g, w)


def _gmlp_kernel(za_ref, lng_ref, ws_ref, bs_ref, o_ref, vn_ref):
    def ln_body(i, carry):
        r = pl.ds(pl.multiple_of(i * PW_ROWS, PW_ROWS), PW_ROWS)
        v = za_ref[r, HALF:].astype(F32)
        xc = v - jnp.mean(v, axis=-1, keepdims=True)
        y = xc * lax.rsqrt(jnp.mean(xc * xc, axis=-1, keepdims=True) + LN_EPS) * lng_ref[...]
        vn_ref[r, :] = y.astype(BF16)
        return carry
    lax.fori_loop(0, SEQ // PW_ROWS, ln_body, 0, unroll=4)

    for g in range(A_GROUPS):
        cols = slice(g * GROUP_DIM, (g + 1) * GROUP_DIM)
        vg = jnp.concatenate(
            [vn_ref[n * CHUNK:(n + 1) * CHUNK, cols] for n in range(N_CHUNKS)], axis=1)
        s = _bdot(ws_ref[g], vg) + bs_ref[g]
        for n in range(N_CHUNKS):
            rows = slice(n * CHUNK, (n + 1) * CHUNK)
            u = za_ref[rows, cols].astype(F32)
            o_ref[rows, cols] = (u * s[:, n * GROUP_DIM:(n + 1) * GROUP_DIM]).astype(BF16)


def _gmlp(proj3d, ln_g, w_s, b_s):
    bsz = proj3d.shape[0]
    return pl.pallas_call(
        _gmlp_kernel,
        out_shape=jax.ShapeDtypeStruct((bsz, SEQ, HALF), BF16),
        grid=(bsz,),
        in_specs=[pl.BlockSpec((None, SEQ, 2 * HALF), lambda b: (b, 0, 0)),
                  pl.BlockSpec((1, HALF), lambda b: (0, 0)),
                  pl.BlockSpec((A_GROUPS, CHUNK, CHUNK), lambda b: (0, 0, 0)),
                  pl.BlockSpec((A_GROUPS, CHUNK, 1), lambda b: (0, 0, 0))],
        out_specs=pl.BlockSpec((None, SEQ, HALF), lambda b: (b, 0, 0)),
        scratch_shapes=[pltpu.VMEM((SEQ, HALF), BF16)],
        compiler_params=_params(1),
        name="gmlp_spatial_gating",
    )(proj3d, ln_g, w_s, b_s)


def _cmul(z, h):
    return z[0] * h[0] + z[1] * h[1], z[1] * h[0] - z[0] * h[1]


def _cadd(a, b):
    return a[0] + b[0], a[1] + b[1]


def _mix_tile(o, q, row0, lh, zs_ref, kre_ref, kim_ref, ys_ref):
    FQ = HY_FQ
    fr = slice(q * FQ + row0, q * FQ + row0 + MIX_ROWS)
    rr = slice(row0, row0 + MIX_ROWS)
    rm = slice(FQ + row0, FQ + row0 + MIX_ROWS)
    lanes = slice(lh * MIX_LANES, (lh + 1) * MIX_LANES)

    def coef(c):
        return kre_ref[o, c, fr, lanes], kim_ref[o, c, fr, lanes]

    def toep2(c, v0, v1):
        q1 = _cmul(_cadd(v0, v1), coef(c))
        q2 = _cmul(v0, coef(c + 1))
        q3 = _cmul(v1, coef(c + 2))
        return _cadd(q1, q3), _cadd(q1, q2)

    z = [(zs_ref[j, rr, lanes], zs_ref[j, rm, lanes]) for j in range(HY_NB)]
    p1 = toep2(0, _cadd(z[0], z[2]), _cadd(z[1], z[3]))
    p2 = toep2(3, z[0], z[1])
    p3 = toep2(6, z[2], z[3])
    y = [_cadd(p1[0], p3[0]), _cadd(p1[1], p3[1]), _cadd(p1[0], p2[0]), _cadd(p1[1], p2[1])]
    for i in range(HY_NB):
        ys_ref[i, rr, lanes] = y[i][0].astype(BF16)
        ys_ref[i, rm, lanes] = y[i][1].astype(BF16)


def _conv3_rows(src_ref, w, start):
    L, R, T, H = SEQ, HY_CONV_ROWS, BF16_ROWS, HALO
    x = src_ref[start:start + R, :].astype(F32)
    zeros = jnp.zeros((H, x.shape[1]), F32)
    prev = src_ref[start - T:start, :].astype(F32)[T - H:] if start > 0 else zeros
    nxt = src_ref[start + R:start + R + T, :].astype(F32)[:H] if start + R < L else zeros
    a = jnp.concatenate([prev, x, nxt], axis=0)
    n = R + 2 * H
    y = pltpu.roll(a, 1, axis=0) * w[0:1, :] + a * w[1:2, :] + pltpu.roll(a, n - 1, axis=0) * w[2:3, :]
    return y[H:H + R]


def _hyena_kernel(v_ref, g1_ref, g2_ref, wv_ref, wg1_ref, wg2_ref, skip_ref,
                  kre_ref, kim_ref, hny_ref, csv_ref, csh_ref, o_ref,
                  z_ref, gate_ref, zcat_ref, zs0_ref, zs1_ref, ys0_ref, ys1_ref, acc_ref):
    P, NB, FQ, R = HY_P, HY_NB, HY_FQ, HY_CONV_ROWS
    ct = o_ref.shape[1]
    steps = FQ // MIX_ROWS // NB

    def chunk(q):
        return slice(2 * q * FQ, 2 * (q + 1) * FQ)

    def mix_share(o, q, k, zs_ref, ys_ref):
        for s in range(steps):
            for lh in range(ct // MIX_LANES):
                _mix_tile(o, q, (k * steps + s) * MIX_ROWS, lh, zs_ref, kre_ref, kim_ref, ys_ref)

    for j in range(NB):
        for r in range(P // R):
            zr = _conv3_rows(v_ref, wv_ref[...], j * P + r * R)
            z_ref[j * P + r * R:j * P + (r + 1) * R, :] = zr
            zcat_ref[j, r * R:(r + 1) * R, :] = zr.astype(BF16)

    def always(k):
        return pl.program_id(1) + k >= 0

    def phase_a(o, g_ref, wg_ref):
        for j in range(NB):
            zs0_ref[j] = _bdot(csv_ref[chunk(0), :], zcat_ref[j])
            for r in range(P // R):
                start = j * P + r * R
                gate_ref[start:start + R, :] = _conv3_rows(g_ref, wg_ref[...], start)

    def phase_b(o):
        for j in range(NB):
            zs1_ref[j] = _bdot(csv_ref[chunk(1), :], zcat_ref[j])
            mix_share(o, 0, j, zs0_ref, ys0_ref)
        tile = slice(FQ, FQ + MIX_ROWS)
        first = lax.broadcasted_iota(jnp.int32, (MIX_ROWS, ct), 0) == 0
        z_ny = [zs0_ref[j, tile, :] for j in range(NB)]
        for i in range(NB):
            y_ny = sum(z_ny[j] * hny_ref[o, i - j + NB - 1:i - j + NB, :] for j in range(NB))
            ys0_ref[i, tile, :] = jnp.where(
                first, y_ny, ys0_ref[i, tile, :].astype(F32)).astype(BF16)

    def phase_c(o):
        for i in range(NB):
            acc_ref[i] = _bdot(csh_ref[0], ys0_ref[i])
            mix_share(o, 1, i, zs1_ref, ys1_ref)

    def phase_d(o):
        for i in range(NB):
            rows = slice(i * P, (i + 1) * P)
            conv = acc_ref[i] + _bdot(csh_ref[1], ys1_ref[i])
            z_new = gate_ref[rows, :] * (conv + z_ref[rows, :] * skip_ref[o:o + 1, :])
            if o == 0:
                z_ref[rows, :] = z_new
                zcat_ref[i] = z_new.astype(BF16)
            else:
                o_ref[rows, :] = z_new.astype(BF16)

    for o, (g_ref, wg_ref) in enumerate(((g1_ref, wg1_ref), (g2_ref, wg2_ref))):
        phases = (functools.partial(phase_a, o, g_ref, wg_ref), functools.partial(phase_b, o),
                  functools.partial(phase_c, o), functools.partial(phase_d, o))
        for k, phase in enumerate(phases):
            pl.when(always(len(phases) * o + k))(phase)


def _hyena(proj3d, conv_w, skip, kre, kim, hny, csv, csh):
    bsz = proj3d.shape[0]
    ct = HY_CT
    nct = HALF // ct
    base = 2 * HALF // ct

    def slab(part):
        return pl.BlockSpec((None, SEQ, ct), lambda c, b: (b, 0, base + part * nct + c))

    def cw(part):
        return pl.BlockSpec((3, ct), lambda c, b: (0, part * nct + c))

    assert HY_NB == 4 and HY_NQ == 2 and HY_FQ % (MIX_ROWS * HY_NB) == 0 and ct % MIX_LANES == 0
    hspec = _resident((2, HY_COEF, HY_P, ct), lambda c, b: (0, 0, 0, c))
    const = lambda c, b: (0, 0)
    return pl.pallas_call(
        _hyena_kernel,
        out_shape=jax.ShapeDtypeStruct((bsz, SEQ, HALF), BF16),
        grid=(nct, bsz),
        in_specs=[slab(0), slab(1), slab(2), cw(0), cw(1), cw(2),
                  pl.BlockSpec((2, ct), lambda c, b: (0, c)),
                  hspec, hspec,
                  pl.BlockSpec((2, HY_LAGS, ct), lambda c, b: (0, 0, c)),
                  pl.BlockSpec((2 * HY_P, HY_P), const), pl.BlockSpec((HY_NQ, HY_P, 2 * HY_FQ), lambda c, b: (0, 0, 0))],
        out_specs=pl.BlockSpec((None, SEQ, ct), lambda c, b: (b, 0, c)),
        scratch_shapes=[pltpu.VMEM((SEQ, ct), F32), pltpu.VMEM((SEQ, ct), F32),
                        pltpu.VMEM((HY_NB, HY_P, ct), BF16),
                        pltpu.VMEM((HY_NB, 2 * HY_FQ, ct), F32),
                        pltpu.VMEM((HY_NB, 2 * HY_FQ, ct), F32),
                        pltpu.VMEM((HY_NB, 2 * HY_FQ, ct), BF16),
                        pltpu.VMEM((HY_NB, 2 * HY_FQ, ct), BF16),
                        pltpu.VMEM((HY_NB, HY_P, ct), F32)],
        compiler_params=_params(2),
        name="hyena_long_conv",
    )(proj3d, proj3d, proj3d, conv_w, conv_w, conv_w, skip, kre, kim, hny, csv, csh)


def _window_sum(a, w):
    n = a.shape[0]
    right = w - w // 2 - 1
    s = a
    k = 1
    while k < w:
        s = s + pltpu.roll(s, k, axis=0)
        k *= 2
    return pltpu.roll(s, n - right, axis=0) if right else s


def _pool_conv_kernel(p_ref, b_ref, c_ref, h_ref, cw_ref, scale_ref, dw_ref, yc_ref, yd_ref,
                      chs_ref, pooled_ref, band_ref):
    L, R, n = SEQ, PW_ROWS, PW_ROWS + 2 * HALO
    offs = (0, (POOL_K - POOL_TILE) // 2, POOL_K - POOL_TILE)

    @pl.when(pl.program_id(0) == 0)
    def _():
        r = lax.broadcasted_iota(jnp.int32, (POOL_TILE, POOL_K), 0)
        c = lax.broadcasted_iota(jnp.int32, (POOL_TILE, POOL_K), 1)
        for g, w in enumerate(POOL_WINDOWS):
            left = w // 2
            for kind, off in enumerate(offs):
                d = c - off - r + left
                m = jnp.where((d >= 0) & (d < w), 1.0 / w, 0.0) - jnp.where(d == left, 1.0, 0.0)
                band_ref[g, kind] = m.astype(BF16)

    zeros = jnp.zeros((HALO, HALF), F32)
    chs_ref[0:HALO, :] = zeros
    chs_ref[L + HALO:L + 2 * HALO, :] = zeros

    for i in range(L // R):
        r = slice(i * R, (i + 1) * R)
        chs_ref[HALO + i * R:HALO + (i + 1) * R, :] = (
            c_ref[r, :].astype(F32) * h_ref[r, :].astype(F32))

    n_tiles = L // POOL_TILE
    for g, w in enumerate(POOL_WINDOWS):
        cols = slice(g * GROUP_DIM, (g + 1) * GROUP_DIM)
        left = w // 2
        right = w - left - 1

        for k in range(n_tiles):
            kind = 0 if k == 0 else (2 if k == n_tiles - 1 else 1)
            t0 = k * POOL_TILE
            ws = t0 - offs[kind]
            pooled_ref[g, t0:t0 + POOL_TILE, :] = _bdot(
                band_ref[g, kind], p_ref[ws:ws + POOL_K, cols]).astype(BF16)

        for top in (True, False):
            pad = jnp.zeros((POOL_EDGE, GROUP_DIM), F32)
            if top:
                a = jnp.concatenate([pad, p_ref[0:2 * POOL_EDGE, cols].astype(F32)], axis=0)
                first = 0
            else:
                a = jnp.concatenate([p_ref[L - 2 * POOL_EDGE:L, cols].astype(F32), pad], axis=0)
                first = L - POOL_EDGE
            keep = slice(POOL_EDGE, 2 * POOL_EDGE)
            t = first + lax.broadcasted_iota(jnp.int32, (POOL_EDGE, GROUP_DIM), 0)
            cnt = (jnp.minimum(t + right, L - 1) - jnp.maximum(t - left, 0) + 1).astype(F32)
            pooled = _window_sum(a, w)[keep] / cnt - a[keep]
            pooled_ref[g, first:first + POOL_EDGE, :] = pooled.astype(BF16)

        dw = dw_ref[:, cols]
        for i in range(L // R):
            r0 = i * R
            e = chs_ref[r0:r0 + n, cols]
            conv = (pltpu.roll(e, 1, axis=0) * dw[0:1, :] + e * dw[1:2, :]
                    + pltpu.roll(e, n - 1, axis=0) * dw[2:3, :])
            yd = b_ref[r0:r0 + R, cols].astype(F32) * conv[HALO:HALO + R]
            yd_ref[r0:r0 + R, cols] = yd.astype(BF16)

        y = _bdot(pooled_ref[g], cw_ref[g]) * scale_ref[:, cols]
        yc_ref[:, cols] = y.astype(BF16)


def _pool_conv(proj3d, c_w, c_scale, d_conv_w):
    bsz = proj3d.shape[0]

    def part(k):
        return pl.BlockSpec((None, SEQ, HALF), lambda b: (b, 0, k))

    out = pl.BlockSpec((None, SEQ, HALF), lambda b: (b, 0, 0))
    return pl.pallas_call(
        _pool_conv_kernel,
        out_shape=(jax.ShapeDtypeStruct((bsz, SEQ, HALF), BF16),) * 2,
        grid=(bsz,),
        in_specs=[part(0), part(1), part(2), part(3),
                  pl.BlockSpec((len(POOL_WINDOWS), GROUP_DIM, GROUP_DIM), lambda b: (0, 0, 0)),
                  pl.BlockSpec((1, HALF), lambda b: (0, 0)),
                  pl.BlockSpec((3, HALF), lambda b: (0, 0))],
        out_specs=(out, out),
        scratch_shapes=[pltpu.VMEM((SEQ + 2 * HALO, HALF), F32),
                        pltpu.VMEM((len(POOL_WINDOWS), SEQ, GROUP_DIM), BF16),
                        pltpu.VMEM((len(POOL_WINDOWS), 3, POOL_TILE, POOL_K), BF16)],
        compiler_params=_params(1),
        name="pool_short_conv",
    )(proj3d, proj3d, proj3d, proj3d, c_w, c_scale, d_conv_w)


def _out_ffn_kernel(x_ref, y0_ref, y1_ref, wo_ref, g_ref, wgu_ref, wd_ref, o_ref, h_ref, a_ref):
    sub = ROW_TILE // FFN_SUBTILES
    rows = [slice(s * sub, (s + 1) * sub) for s in range(FFN_SUBTILES)]
    for r in rows:
        mix = _bdot(y0_ref[r, :], wo_ref[:HALF, :]) + _bdot(y1_ref[r, :], wo_ref[HALF:, :])
        o_ref[r, :] = x_ref[r, :] + _rms(mix, g_ref[1:2, :])
        h_ref[r, :] = _rms(o_ref[r, :], g_ref[2:3, :]).astype(BF16)
    for r in rows:
        for k in range(D_FF // FF_CHUNK):
            gate = _bdot(h_ref[r, :], wgu_ref[:, k * FF_CHUNK:(k + 1) * FF_CHUNK])
            up = _bdot(h_ref[r, :], wgu_ref[:, D_FF + k * FF_CHUNK:D_FF + (k + 1) * FF_CHUNK])
            a_ref[r, k * FF_CHUNK:(k + 1) * FF_CHUNK] = (
                gate * jax.nn.sigmoid(gate) * up).astype(BF16)
    for r in rows:
        f = _bdot(a_ref[r, :], wd_ref[...])
        o_ref[r, :] = o_ref[r, :] + _rms(f, g_ref[3:4, :])


def _out_ffn(x2d, y0, y1, w_out, mix_layer, g, w_gu, w_down, layer):
    m = x2d.shape[0]
    tile = lambda n: pl.BlockSpec((ROW_TILE, n), lambda i: (i, 0))
    return pl.pallas_call(
        _out_ffn_kernel,
        out_shape=jax.ShapeDtypeStruct((m, D_MODEL), F32),
        grid=(m // ROW_TILE,),
        in_specs=[tile(D_MODEL), tile(HALF), tile(HALF),
                  _resident((None, D_MODEL, D_MODEL), lambda i: (mix_layer, 0, 0)),
                  pl.BlockSpec((None, 4, D_MODEL), lambda i: (layer, 0, 0)),
                  _resident((None, D_MODEL, 2 * D_FF), lambda i: (layer, 0, 0)),
                  _resident((None, D_FF, D_MODEL), lambda i: (layer, 0, 0))],
        out_specs=tile(D_MODEL),
        scratch_shapes=[pltpu.VMEM((ROW_TILE, D_MODEL), BF16),
                        pltpu.VMEM((ROW_TILE, D_FF), BF16)],
        compiler_params=_params(1, FFN_VMEM_LIMIT),
        name="out_proj_ffn",
    )(x2d, y0, y1, w_out, g, w_gu, w_down)


def kernel(x, norm_g, ffn_w_gu, ffn_w_down, ab_w_in, ab_w_out, a_ln_g, a_w_s, a_b_s, b_conv_w, b_filt_w1, b_filt_b1, b_filt_freq, b_filt_w2, b_filt_b2, b_filt_w3, b_decay, b_skip, cd_w_in, cd_w_out, c_w, c_scale, d_conv_w):
    bsz, seq, d = x.shape
    assert (seq, d) == (SEQ, D_MODEL) and norm_g.shape[0] == DEPTH
    m = bsz * seq
    bands = jnp.linspace(1e-4, FILTER_BANDS - 1, FILTER_BANDS, dtype=F32)[None, :]
    csv, csh = _dft_tables()
    x2d = x.reshape(m, d)
    ab_w_in, ab_w_out, cd_w_in, cd_w_out, ffn_w_gu, ffn_w_down = (
        w.astype(BF16) for w in (ab_w_in, ab_w_out, cd_w_in, cd_w_out, ffn_w_gu, ffn_w_down))
    for i in range(DEPTH):
        j = i // 2
        g = norm_g[i]
        if i % 2 == 0:
            proj = _in_proj(x2d, g[0:1], ab_w_in, j, 2 * HALF).reshape(bsz, seq, -1)
            y0 = _gmlp(proj, a_ln_g[j][None, :], a_w_s[j].astype(BF16), a_b_s[j][:, :, None])
            hidden = _filter_hidden(bands, b_filt_w1[j], b_filt_b1[j][None, :], b_filt_freq[j],
                                    b_filt_w2[j], b_filt_b2[j][None, :])
            kre, kim, hny = _filter_spectra(hidden, b_filt_w3[j], b_decay[j][None, :], csv)
            y1 = _hyena(proj, b_conv_w[j], b_skip[j], kre, kim, hny, csv, csh)
            w_out = ab_w_out
        else:
            proj = _in_proj(x2d, g[0:1], cd_w_in, j, 0).reshape(bsz, seq, -1)
            y0, y1 = _pool_conv(proj, c_w[j].astype(BF16), c_scale[j][None, :], d_conv_w[j])
            w_out = cd_w_out
        x2d = _out_ffn(x2d, y0.reshape(m, HALF), y1.reshape(m, HALF), w_out, j, norm_g,
                       ffn_w_gu, ffn_w_down, i)
    return x2d.reshape(bsz, seq, d)
```

```python
import functools
import math

import jax
import jax.numpy as jnp
from jax import lax
from jax.experimental import pallas as pl
from jax.experimental.pallas import tpu as pltpu

F32 = jnp.float32
BF16 = jnp.bfloat16
LANES = 128
BF16_ROWS = 16

D_MODEL = 1024
SEQ = 2048
DEPTH = 4
HALF = D_MODEL // 2
CHUNK = 128
N_CHUNKS = SEQ // CHUNK
A_GROUPS = 4
GROUP_DIM = HALF // A_GROUPS
FILTER_BANDS = 16
FILTER_HIDDEN = 64
POOL_WINDOWS = (2, 4, 8, 16)
D_FF = 2816
RMS_EPS = 1e-6
LN_EPS = 1e-5
HY_P = 512
HY_NB = SEQ // HY_P
HY_LAGS = 2 * HY_NB - 1
NFFT = 2 * HY_P
HY_NQ = 2
HY_FQ = HY_P // HY_NQ
HY_COEF = 9

ROW_TILE = 1024
FFN_SUBTILES = 4
FFN_VMEM_LIMIT = 60 * 1024 * 1024
IN_TILE = 2048
IN_SUBTILES = 4
IN_CHUNK = 256
FF_CHUNK = 256
HY_CT = 256
HY_CONV_ROWS = 128
MIX_ROWS = 16
MIX_LANES = 128
PW_ROWS = 128
HALO = 8
POOL_TILE = 256
POOL_K = 512
POOL_EDGE = 16
VMEM_LIMIT = 56 * 1024 * 1024


def _params(n_axes, vmem=VMEM_LIMIT):
    return pltpu.CompilerParams(
        dimension_semantics=("arbitrary",) * n_axes, vmem_limit_bytes=vmem)


def _resident(shape, index_map):
    return pl.BlockSpec(shape, index_map, pipeline_mode=pl.Buffered(1))


def _rms(x, g):
    return x * lax.rsqrt(jnp.mean(x * x, axis=-1, keepdims=True) + RMS_EPS) * g


def _bdot(a, b):
    return jnp.dot(a, b, preferred_element_type=F32)


def _fdot(a, b):
    return jnp.dot(a, b, preferred_element_type=F32, precision=lax.Precision.HIGHEST)


def _shift_rows(x, k, row):
    n = x.shape[0]
    y = pltpu.roll(x, k % n, axis=0)
    if k > 0:
        return jnp.where(row >= k, y, 0.0)
    return jnp.where(row < n + k, y, 0.0)


def _conv3(x, w, row):
    return (_shift_rows(x, 1, row) * w[0:1, :] + x * w[1:2, :]
            + _shift_rows(x, -1, row) * w[2:3, :])


def _dft_kernel(csv_ref, csh_ref):
    P, FQ = HY_P, HY_FQ
    f = lax.broadcasted_iota(jnp.int32, (P, LANES), 0)
    lane = lax.broadcasted_iota(jnp.int32, (P, LANES), 1)
    scale = 2.0 * math.pi / NFFT
    ang_b = ((f * lane) & (NFFT - 1)).astype(F32) * scale
    ang_a = ((f * (lane * LANES)) & (NFFT - 1)).astype(F32) * scale
    cb, sb = jnp.cos(ang_b), jnp.sin(ang_b)
    ca, sa = jnp.cos(ang_a), jnp.sin(ang_a)
    for a in range(P // LANES):
        cols = slice(a * LANES, (a + 1) * LANES)
        ca_a = ca[:, a:a + 1]
        sa_a = sa[:, a:a + 1]
        c = (ca_a * cb - sa_a * sb).astype(BF16)
        s = sa_a * cb + ca_a * sb
        s_v = jnp.where(f == 0, jnp.where((lane & 1) == 0, 1.0, -1.0), s).astype(BF16)
        for q in range(HY_NQ):
            csv_ref[2 * q * FQ:(2 * q + 1) * FQ, cols] = c[q * FQ:(q + 1) * FQ]
            csv_ref[(2 * q + 1) * FQ:(2 * q + 2) * FQ, cols] = s_v[q * FQ:(q + 1) * FQ]
        s_h = s
        if a == 0:
            s_h = jnp.where(lane == 0, jnp.where((f & 1) == 0, 1.0, -1.0), s)
        q, off = divmod(a * LANES, FQ)
        csh_ref[q, :, off:off + LANES] = c
        csh_ref[q, :, FQ + off:FQ + off + LANES] = s_h.astype(BF16)


def _dft_tables():
    return pl.pallas_call(
        _dft_kernel,
        out_shape=(jax.ShapeDtypeStruct((2 * HY_P, HY_P), BF16),
                   jax.ShapeDtypeStruct((HY_NQ, HY_P, 2 * HY_FQ), BF16)),
        compiler_params=_params(0),
        name="dft_tables",
    )()


def _filter_hidden_kernel(bands_ref, w1_ref, b1_ref, freq_ref, w2_ref, b2_ref, h_ref):
    L = SEQ
    r = lax.broadcasted_iota(jnp.int32, (L, 1), 0)
    t = jnp.where(pl.program_id(0) == 0, r, L - 1 - r).astype(F32)
    t01 = t / (L - 1)
    fw = (2 * math.pi * t / L) * bands_ref[...]
    pre = (t01 * w1_ref[0:1, :] + _fdot(jnp.cos(fw), w1_ref[1:1 + FILTER_BANDS, :])
           + _fdot(-jnp.sin(fw), w1_ref[1 + FILTER_BANDS:, :]) + b1_ref[...])
    h = jnp.sin(freq_ref[0:1, :] * pre)
    h_ref[...] = jnp.sin(freq_ref[1:2, :] * (_fdot(h, w2_ref[...]) + b2_ref[...]))


def _filter_hidden(bands, w1, b1, freq, w2, b2):
    const = lambda s: (0, 0)
    return pl.pallas_call(
        _filter_hidden_kernel,
        out_shape=jax.ShapeDtypeStruct((2, SEQ, FILTER_HIDDEN), F32),
        grid=(2,),
        in_specs=[pl.BlockSpec((1, FILTER_BANDS), const),
                  pl.BlockSpec((1 + 2 * FILTER_BANDS, FILTER_HIDDEN), const),
                  pl.BlockSpec((1, FILTER_HIDDEN), const),
                  pl.BlockSpec((2, FILTER_HIDDEN), const),
                  pl.BlockSpec((FILTER_HIDDEN, FILTER_HIDDEN), const),
                  pl.BlockSpec((1, FILTER_HIDDEN), const)],
        out_specs=pl.BlockSpec((None, SEQ, FILTER_HIDDEN), lambda s: (s, 0, 0)),
        compiler_params=_params(1),
        name="hyena_filter_hidden",
    )(bands, w1, b1, freq, w2, b2)


def _filter_kernel(hid_ref, w3f_ref, w3b_ref, decf_ref, decb_ref, csv_ref,
                   kre_ref, kim_ref, hny_ref):
    L, P, NB = SEQ, HY_P, HY_NB
    ct = kre_ref.shape[-1]
    row = lax.broadcasted_iota(jnp.int32, (L, ct), 0)
    t_asc = row.astype(F32) / (L - 1)
    t_dsc = (L - 1 - row).astype(F32) / (L - 1)
    dec_f = jnp.abs(decf_ref[...])
    dec_b = jnp.abs(decb_ref[...])
    h_asc = hid_ref[0]
    h_dsc = hid_ref[1]
    inner = (row & (P - 1)) != 0
    f_asc = _fdot(h_asc, w3f_ref[...]) * jnp.exp(-t_asc * dec_f)
    b_dsc = _fdot(h_dsc, w3b_ref[...]) * jnp.exp(-t_dsc * dec_b)
    f_dsc = _fdot(h_dsc, w3f_ref[...]) * jnp.exp(-t_dsc * dec_f)
    f_dsc = jnp.where(inner, _shift_rows(f_dsc, 1, row), 0.0)
    b_asc = _fdot(h_asc, w3b_ref[...]) * jnp.exp(-t_asc * dec_b)
    b_asc = jnp.where(inner, _shift_rows(b_asc, 1, row), 0.0)

    def blk(x, i):
        return x[i * P:(i + 1) * P, :]

    frow = lax.broadcasted_iota(jnp.int32, (P, ct), 0)
    wgt = jnp.where(frow == 0, 1.0 / NFFT, 2.0 / NFFT)
    sign = jnp.where((frow & 1) == 0, 1.0, -1.0)
    FQ = HY_FQ
    h = {}
    for idx in range(HY_LAGS):
        d = idx - (NB - 1)
        if d >= 1:
            kp, kn = blk(f_asc, d), blk(f_dsc, NB - d)
        elif d == 0:
            kp, kn = blk(f_asc, 0), blk(b_asc, 0)
        else:
            kp, kn = blk(b_dsc, NB + d), blk(b_asc, -d)
        ev = kp + kn
        od = kp - kn
        evb = ev.astype(BF16)
        odb = od.astype(BF16)
        re = jnp.concatenate([_bdot(csv_ref[2 * q * FQ:(2 * q + 1) * FQ, :], evb)
                              for q in range(HY_NQ)], axis=0)
        im = jnp.concatenate([_bdot(csv_ref[(2 * q + 1) * FQ:(2 * q + 2) * FQ, :], odb)
                              for q in range(HY_NQ)], axis=0)
        h[d] = (re * wgt, jnp.where(frow == 0, 0.0, -im * wgt))
        hny_ref[idx:idx + 1, :] = jnp.sum(ev * sign, axis=0, keepdims=True) * (1.0 / NFFT)

    def sub(a, b):
        return (a[0] - b[0], a[1] - b[1])

    def coef3(a0, a_dn, a_up):
        return [a0, sub(a_dn, a0), sub(a_up, a0)]

    coefs = (coef3(h[0], h[1], h[-1])
             + coef3(sub(h[2], h[0]), sub(h[3], h[1]), sub(h[1], h[-1]))
             + coef3(sub(h[-2], h[0]), sub(h[-1], h[1]), sub(h[-3], h[-1])))
    for c, (re, im) in enumerate(coefs):
        kre_ref[c] = re
        kim_ref[c] = im


def _filter_spectra(hidden, w3, decay, csv):
    ct = HY_CT
    nct = HALF // ct
    fcol = lambda o, c: (0, o * nct + c)
    bcol = lambda o, c: (0, (2 + o) * nct + c)
    hspec = pl.BlockSpec((None, HY_COEF, HY_P, ct), lambda o, c: (o, 0, 0, c))
    return pl.pallas_call(
        _filter_kernel,
        out_shape=(jax.ShapeDtypeStruct((2, HY_COEF, HY_P, HALF), F32),
                   jax.ShapeDtypeStruct((2, HY_COEF, HY_P, HALF), F32),
                   jax.ShapeDtypeStruct((2, HY_LAGS, HALF), F32)),
        grid=(2, nct),
        in_specs=[
            pl.BlockSpec((2, SEQ, FILTER_HIDDEN), lambda o, c: (0, 0, 0)),
            pl.BlockSpec((FILTER_HIDDEN, ct), fcol),
            pl.BlockSpec((FILTER_HIDDEN, ct), bcol),
            pl.BlockSpec((1, ct), fcol),
            pl.BlockSpec((1, ct), bcol),
            pl.BlockSpec((2 * HY_P, HY_P), lambda o, c: (0, 0)),
        ],
        out_specs=(hspec, hspec,
                   pl.BlockSpec((None, HY_LAGS, ct), lambda o, c: (o, 0, c))),
        compiler_params=_params(2),
        name="hyena_filter_spectra",
    )(hidden, w3, w3, decay, decay, csv)


def _in_proj_kernel(x_ref, g_ref, w_ref, o_ref, h_ref, *, gelu_cols):
    sub = IN_TILE // IN_SUBTILES
    rows = [slice(s * sub, (s + 1) * sub) for s in range(IN_SUBTILES)]
    for r in rows:
        h_ref[r, :] = _rms(x_ref[r, :], g_ref[...]).astype(BF16)
    n_out = o_ref.shape[1]
    gelu_chunks = list(range(0, gelu_cols, IN_CHUNK))
    plain_chunks = list(range(gelu_cols, n_out, IN_CHUNK))
    order = [c for pair in zip(gelu_chunks, plain_chunks) for c in pair]
    order += gelu_chunks[len(plain_chunks):] + plain_chunks[len(gelu_chunks):]
    for r in rows:
        for c0 in order:
            p = _bdot(h_ref[r, :], w_ref[:, c0:c0 + IN_CHUNK])
            if c0 < gelu_cols:
                p = jax.nn.gelu(p)
            o_ref[r, c0:c0 + IN_CHUNK] = p.astype(BF16)


def _in_proj(x2d, g, w, layer, gelu_cols):
    m = x2d.shape[0]
    n_out = w.shape[2]
    return pl.pallas_call(
        functools.partial(_in_proj_kernel, gelu_cols=gelu_cols),
        out_shape=jax.ShapeDtypeStruct((m, n_out), BF16),
        grid=(m // IN_TILE,),
        in_specs=[pl.BlockSpec((IN_TILE, D_MODEL), lambda i: (i, 0)),
                  pl.BlockSpec((1, D_MODEL), lambda i: (0, 0)),
                  _resident((None, D_MODEL, n_out), lambda i: (layer, 0, 0))],
        out_specs=pl.BlockSpec((IN_TILE, n_out), lambda i: (i, 0)),
        scratch_shapes=[pltpu.VMEM((IN_TILE, D_MODEL), BF16)],
        compiler_params=_params(1),
        name="rms_in_proj",
    )(x2d, g, w)


def _in_proj_gmlp_kernel(x_ref, g_ref, w_ref, lng_ref, ws_ref, bs_ref, o_ref, ya_ref,
                         h_ref, za_ref, vn_ref):
    sub = IN_TILE // IN_SUBTILES
    rows = [slice(s * sub, (s + 1) * sub) for s in range(IN_SUBTILES)]
    for r in rows:
        h_ref[r, :] = _rms(x_ref[r, :], g_ref[...]).astype(BF16)
    n_gelu = 2 * HALF
    n_out = w_ref.shape[1]
    gelu_chunks = list(range(0, n_gelu, IN_CHUNK))
    plain_chunks = list(range(n_gelu, n_out, IN_CHUNK))
    order = [c for pair in zip(gelu_chunks, plain_chunks) for c in pair]
    order += gelu_chunks[len(plain_chunks):] + plain_chunks[len(gelu_chunks):]
    def layer_norm(s):
        for i in range(sub // PW_ROWS):
            rr = slice(s * sub + i * PW_ROWS, s * sub + (i + 1) * PW_ROWS)
            v = za_ref[rr, HALF:].astype(F32)
            xc = v - jnp.mean(v, axis=-1, keepdims=True)
            y = xc * lax.rsqrt(jnp.mean(xc * xc, axis=-1, keepdims=True) + LN_EPS) * lng_ref[...]
            vn_ref[rr, :] = y.astype(BF16)

    for s, r in enumerate(rows):
        for k, c0 in enumerate(order):
            p = _bdot(h_ref[r, :], w_ref[:, c0:c0 + IN_CHUNK])
            if c0 < n_gelu:
                za_ref[r, c0:c0 + IN_CHUNK] = jax.nn.gelu(p).astype(BF16)
            else:
                o_ref[r, c0 - n_gelu:c0 - n_gelu + IN_CHUNK] = p.astype(BF16)
            if k == 1 and s > 0:
                layer_norm(s - 1)
    layer_norm(len(rows) - 1)

    for g in range(A_GROUPS):
        cols = slice(g * GROUP_DIM, (g + 1) * GROUP_DIM)
        vg = jnp.concatenate(
            [vn_ref[n * CHUNK:(n + 1) * CHUNK, cols] for n in range(N_CHUNKS)], axis=1)
        sg = _bdot(ws_ref[g], vg) + bs_ref[g]
        for n in range(N_CHUNKS):
            rr = slice(n * CHUNK, (n + 1) * CHUNK)
            u = za_ref[rr, cols].astype(F32)
            ya_ref[rr, cols] = (u * sg[:, n * GROUP_DIM:(n + 1) * GROUP_DIM]).astype(BF16)


def _in_proj_gmlp(x2d, g, w, layer, ln_g, w_s, b_s):
    assert IN_TILE == SEQ
    m = x2d.shape[0]
    n_out = w.shape[2]
    n_hy = n_out - 2 * HALF
    return pl.pallas_call(
        _in_proj_gmlp_kernel,
        out_shape=(jax.ShapeDtypeStruct((m, n_hy), BF16), jax.ShapeDtypeStruct((m, HALF), BF16)),
        grid=(m // IN_TILE,),
        in_specs=[pl.BlockSpec((IN_TILE, D_MODEL), lambda i: (i, 0)),
                  pl.BlockSpec((1, D_MODEL), lambda i: (0, 0)),
                  _resident((None, D_MODEL, n_out), lambda i: (layer, 0, 0)),
                  pl.BlockSpec((1, HALF), lambda i: (0, 0)),
                  pl.BlockSpec((A_GROUPS, CHUNK, CHUNK), lambda i: (0, 0, 0)),
                  pl.BlockSpec((A_GROUPS, CHUNK, 1), lambda i: (0, 0, 0))],
        out_specs=(pl.BlockSpec((IN_TILE, n_hy), lambda i: (i, 0)),
                   pl.BlockSpec((IN_TILE, HALF), lambda i: (i, 0))),
        scratch_shapes=[pltpu.VMEM((IN_TILE, D_MODEL), BF16),
                        pltpu.VMEM((IN_TILE, 2 * HALF), BF16),
                        pltpu.VMEM((IN_TILE, HALF), BF16)],
        compiler_params=_params(1, FFN_VMEM_LIMIT),
        name="rms_in_proj_gmlp",
    )(x2d, g, w, ln_g, w_s, b_s)


def _gmlp_kernel(za_ref, lng_ref, ws_ref, bs_ref, o_ref, vn_ref):
    def ln_body(i, carry):
        r = pl.ds(pl.multiple_of(i * PW_ROWS, PW_ROWS), PW_ROWS)
        v = za_ref[r, HALF:].astype(F32)
        xc = v - jnp.mean(v, axis=-1, keepdims=True)
        y = xc * lax.rsqrt(jnp.mean(xc * xc, axis=-1, keepdims=True) + LN_EPS) * lng_ref[...]
        vn_ref[r, :] = y.astype(BF16)
        return carry
    lax.fori_loop(0, SEQ // PW_ROWS, ln_body, 0, unroll=4)

    for g in range(A_GROUPS):
        cols = slice(g * GROUP_DIM, (g + 1) * GROUP_DIM)
        vg = jnp.concatenate(
            [vn_ref[n * CHUNK:(n + 1) * CHUNK, cols] for n in range(N_CHUNKS)], axis=1)
        s = _bdot(ws_ref[g], vg) + bs_ref[g]
        for n in range(N_CHUNKS):
            rows = slice(n * CHUNK, (n + 1) * CHUNK)
            u = za_ref[rows, cols].astype(F32)
            o_ref[rows, cols] = (u * s[:, n * GROUP_DIM:(n + 1) * GROUP_DIM]).astype(BF16)


def _gmlp(proj3d, ln_g, w_s, b_s):
    bsz = proj3d.shape[0]
    return pl.pallas_call(
        _gmlp_kernel,
        out_shape=jax.ShapeDtypeStruct((bsz, SEQ, HALF), BF16),
        grid=(bsz,),
        in_specs=[pl.BlockSpec((None, SEQ, 2 * HALF), lambda b: (b, 0, 0)),
                  pl.BlockSpec((1, HALF), lambda b: (0, 0)),
                  pl.BlockSpec((A_GROUPS, CHUNK, CHUNK), lambda b: (0, 0, 0)),
                  pl.BlockSpec((A_GROUPS, CHUNK, 1), lambda b: (0, 0, 0))],
        out_specs=pl.BlockSpec((None, SEQ, HALF), lambda b: (b, 0, 0)),
        scratch_shapes=[pltpu.VMEM((SEQ, HALF), BF16)],
        compiler_params=_params(1),
        name="gmlp_spatial_gating",
    )(proj3d, ln_g, w_s, b_s)


def _cmul(z, h):
    return z[0] * h[0] + z[1] * h[1], z[1] * h[0] - z[0] * h[1]


def _cadd(a, b):
    return a[0] + b[0], a[1] + b[1]


def _mix_tile(o, q, row0, lh, zs_ref, kre_ref, kim_ref, ys_ref):
    FQ = HY_FQ
    fr = slice(q * FQ + row0, q * FQ + row0 + MIX_ROWS)
    rr = slice(row0, row0 + MIX_ROWS)
    rm = slice(FQ + row0, FQ + row0 + MIX_ROWS)
    lanes = slice(lh * MIX_LANES, (lh + 1) * MIX_LANES)

    def coef(c):
        return kre_ref[o, c, fr, lanes], kim_ref[o, c, fr, lanes]

    def toep2(c, v0, v1):
        q1 = _cmul(_cadd(v0, v1), coef(c))
        q2 = _cmul(v0, coef(c + 1))
        q3 = _cmul(v1, coef(c + 2))
        return _cadd(q1, q3), _cadd(q1, q2)

    z = [(zs_ref[j, rr, lanes], zs_ref[j, rm, lanes]) for j in range(HY_NB)]
    p1 = toep2(0, _cadd(z[0], z[2]), _cadd(z[1], z[3]))
    p2 = toep2(3, z[0], z[1])
    p3 = toep2(6, z[2], z[3])
    y = [_cadd(p1[0], p3[0]), _cadd(p1[1], p3[1]), _cadd(p1[0], p2[0]), _cadd(p1[1], p2[1])]
    for i in range(HY_NB):
        ys_ref[i, rr, lanes] = y[i][0].astype(BF16)
        ys_ref[i, rm, lanes] = y[i][1].astype(BF16)


def _conv3_rows(src_ref, w, start):
    L, R, T, H = SEQ, HY_CONV_ROWS, BF16_ROWS, HALO
    x = src_ref[start:start + R, :].astype(F32)
    zeros = jnp.zeros((H, x.shape[1]), F32)
    prev = src_ref[start - T:start, :].astype(F32)[T - H:] if start > 0 else zeros
    nxt = src_ref[start + R:start + R + T, :].astype(F32)[:H] if start + R < L else zeros
    a = jnp.concatenate([prev, x, nxt], axis=0)
    n = R + 2 * H
    y = pltpu.roll(a, 1, axis=0) * w[0:1, :] + a * w[1:2, :] + pltpu.roll(a, n - 1, axis=0) * w[2:3, :]
    return y[H:H + R]


def _hyena_kernel(v_ref, g1_ref, g2_ref, wv_ref, wg1_ref, wg2_ref, skip_ref,
                  kre_ref, kim_ref, hny_ref, csv_ref, csh_ref, o_ref,
                  z_ref, gate_ref, zcat_ref, zs0_ref, zs1_ref, ys0_ref, ys1_ref, acc_ref):
    P, NB, FQ, R = HY_P, HY_NB, HY_FQ, HY_CONV_ROWS
    ct = o_ref.shape[1]
    steps = FQ // MIX_ROWS // NB

    def chunk(q):
        return slice(2 * q * FQ, 2 * (q + 1) * FQ)

    def mix_share(o, q, k, zs_ref, ys_ref):
        for s in range(steps):
            for lh in range(ct // MIX_LANES):
                _mix_tile(o, q, (k * steps + s) * MIX_ROWS, lh, zs_ref, kre_ref, kim_ref, ys_ref)

    for j in range(NB):
        for r in range(P // R):
            zr = _conv3_rows(v_ref, wv_ref[...], j * P + r * R)
            z_ref[j * P + r * R:j * P + (r + 1) * R, :] = zr
            zcat_ref[j, r * R:(r + 1) * R, :] = zr.astype(BF16)

    def always(k):
        return pl.program_id(1) + k >= 0

    def phase_a(o, g_ref, wg_ref):
        for j in range(NB):
            zs0_ref[j] = _bdot(csv_ref[chunk(0), :], zcat_ref[j])
            for r in range(P // R):
                start = j * P + r * R
                gate_ref[start:start + R, :] = _conv3_rows(g_ref, wg_ref[...], start)

    def phase_b(o):
        for j in range(NB):
            zs1_ref[j] = _bdot(csv_ref[chunk(1), :], zcat_ref[j])
            mix_share(o, 0, j, zs0_ref, ys0_ref)
        tile = slice(FQ, FQ + MIX_ROWS)
        first = lax.broadcasted_iota(jnp.int32, (MIX_ROWS, ct), 0) == 0
        z_ny = [zs0_ref[j, tile, :] for j in range(NB)]
        for i in range(NB):
            y_ny = sum(z_ny[j] * hny_ref[o, i - j + NB - 1:i - j + NB, :] for j in range(NB))
            ys0_ref[i, tile, :] = jnp.where(
                first, y_ny, ys0_ref[i, tile, :].astype(F32)).astype(BF16)

    def phase_c(o):
        for i in range(NB):
            acc_ref[i] = _bdot(csh_ref[0], ys0_ref[i])
            mix_share(o, 1, i, zs1_ref, ys1_ref)

    def phase_d(o):
        for i in range(NB):
            rows = slice(i * P, (i + 1) * P)
            conv = acc_ref[i] + _bdot(csh_ref[1], ys1_ref[i])
            z_new = gate_ref[rows, :] * (conv + z_ref[rows, :] * skip_ref[o:o + 1, :])
            if o == 0:
                z_ref[rows, :] = z_new
                zcat_ref[i] = z_new.astype(BF16)
            else:
                o_ref[rows, :] = z_new.astype(BF16)

    for o, (g_ref, wg_ref) in enumerate(((g1_ref, wg1_ref), (g2_ref, wg2_ref))):
        phases = (functools.partial(phase_a, o, g_ref, wg_ref), functools.partial(phase_b, o),
                  functools.partial(phase_c, o), functools.partial(phase_d, o))
        for k, phase in enumerate(phases):
            pl.when(always(len(phases) * o + k))(phase)


def _hyena(proj3d, conv_w, skip, kre, kim, hny, csv, csh):
    bsz = proj3d.shape[0]
    ct = HY_CT
    nct = HALF // ct
    assert proj3d.shape[2] == 3 * HALF

    def slab(part):
        return pl.BlockSpec((None, SEQ, ct), lambda c, b: (b, 0, part * nct + c))

    def cw(part):
        return pl.BlockSpec((3, ct), lambda c, b: (0, part * nct + c))

    assert HY_NB == 4 and HY_NQ == 2 and HY_FQ % (MIX_ROWS * HY_NB) == 0 and ct % MIX_LANES == 0
    hspec = _resident((2, HY_COEF, HY_P, ct), lambda c, b: (0, 0, 0, c))
    const = lambda c, b: (0, 0)
    return pl.pallas_call(
        _hyena_kernel,
        out_shape=jax.ShapeDtypeStruct((bsz, SEQ, HALF), BF16),
        grid=(nct, bsz),
        in_specs=[slab(0), slab(1), slab(2), cw(0), cw(1), cw(2),
                  pl.BlockSpec((2, ct), lambda c, b: (0, c)),
                  hspec, hspec,
                  pl.BlockSpec((2, HY_LAGS, ct), lambda c, b: (0, 0, c)),
                  pl.BlockSpec((2 * HY_P, HY_P), const), pl.BlockSpec((HY_NQ, HY_P, 2 * HY_FQ), lambda c, b: (0, 0, 0))],
        out_specs=pl.BlockSpec((None, SEQ, ct), lambda c, b: (b, 0, c)),
        scratch_shapes=[pltpu.VMEM((SEQ, ct), F32), pltpu.VMEM((SEQ, ct), F32),
                        pltpu.VMEM((HY_NB, HY_P, ct), BF16),
                        pltpu.VMEM((HY_NB, 2 * HY_FQ, ct), F32),
                        pltpu.VMEM((HY_NB, 2 * HY_FQ, ct), F32),
                        pltpu.VMEM((HY_NB, 2 * HY_FQ, ct), BF16),
                        pltpu.VMEM((HY_NB, 2 * HY_FQ, ct), BF16),
                        pltpu.VMEM((HY_NB, HY_P, ct), F32)],
        compiler_params=_params(2),
        name="hyena_long_conv",
    )(proj3d, proj3d, proj3d, conv_w, conv_w, conv_w, skip, kre, kim, hny, csv, csh)


def _window_sum(a, w):
    n = a.shape[0]
    right = w - w // 2 - 1
    s = a
    k = 1
    while k < w:
        s = s + pltpu.roll(s, k, axis=0)
        k *= 2
    return pltpu.roll(s, n - right, axis=0) if right else s


def _pool_conv_kernel(p_ref, b_ref, c_ref, h_ref, cw_ref, scale_ref, dw_ref, yc_ref, yd_ref,
                      chs_ref, pooled_ref, band_ref):
    L, R, n = SEQ, PW_ROWS, PW_ROWS + 2 * HALO
    offs = (0, (POOL_K - POOL_TILE) // 2, POOL_K - POOL_TILE)

    @pl.when(pl.program_id(0) == 0)
    def _():
        r = lax.broadcasted_iota(jnp.int32, (POOL_TILE, POOL_K), 0)
        c = lax.broadcasted_iota(jnp.int32, (POOL_TILE, POOL_K), 1)
        for g, w in enumerate(POOL_WINDOWS):
            left = w // 2
            for kind, off in enumerate(offs):
                d = c - off - r + left
                m = jnp.where((d >= 0) & (d < w), 1.0 / w, 0.0) - jnp.where(d == left, 1.0, 0.0)
                band_ref[g, kind] = m.astype(BF16)

    zeros = jnp.zeros((HALO, HALF), F32)
    chs_ref[0:HALO, :] = zeros
    chs_ref[L + HALO:L + 2 * HALO, :] = zeros

    for i in range(L // R):
        r = slice(i * R, (i + 1) * R)
        chs_ref[HALO + i * R:HALO + (i + 1) * R, :] = (
            c_ref[r, :].astype(F32) * h_ref[r, :].astype(F32))

    n_tiles = L // POOL_TILE
    for g, w in enumerate(POOL_WINDOWS):
        cols = slice(g * GROUP_DIM, (g + 1) * GROUP_DIM)
        left = w // 2
        right = w - left - 1

        for k in range(n_tiles):
            kind = 0 if k == 0 else (2 if k == n_tiles - 1 else 1)
            t0 = k * POOL_TILE
            ws = t0 - offs[kind]
            pooled_ref[g, t0:t0 + POOL_TILE, :] = _bdot(
                band_ref[g, kind], p_ref[ws:ws + POOL_K, cols]).astype(BF16)

        for top in (True, False):
            pad = jnp.zeros((POOL_EDGE, GROUP_DIM), F32)
            if top:
                a = jnp.concatenate([pad, p_ref[0:2 * POOL_EDGE, cols].astype(F32)], axis=0)
                first = 0
            else:
                a = jnp.concatenate([p_ref[L - 2 * POOL_EDGE:L, cols].astype(F32), pad], axis=0)
                first = L - POOL_EDGE
            keep = slice(POOL_EDGE, 2 * POOL_EDGE)
            t = first + lax.broadcasted_iota(jnp.int32, (POOL_EDGE, GROUP_DIM), 0)
            cnt = (jnp.minimum(t + right, L - 1) - jnp.maximum(t - left, 0) + 1).astype(F32)
            pooled = _window_sum(a, w)[keep] / cnt - a[keep]
            pooled_ref[g, first:first + POOL_EDGE, :] = pooled.astype(BF16)

        dw = dw_ref[:, cols]
        for i in range(L // R):
            r0 = i * R
            e = chs_ref[r0:r0 + n, cols]
            conv = (pltpu.roll(e, 1, axis=0) * dw[0:1, :] + e * dw[1:2, :]
                    + pltpu.roll(e, n - 1, axis=0) * dw[2:3, :])
            yd = b_ref[r0:r0 + R, cols].astype(F32) * conv[HALO:HALO + R]
            yd_ref[r0:r0 + R, cols] = yd.astype(BF16)

        y = _bdot(pooled_ref[g], cw_ref[g]) * scale_ref[:, cols]
        yc_ref[:, cols] = y.astype(BF16)


def _pool_conv(proj3d, c_w, c_scale, d_conv_w):
    bsz = proj3d.shape[0]

    def part(k):
        return pl.BlockSpec((None, SEQ, HALF), lambda b: (b, 0, k))

    out = pl.BlockSpec((None, SEQ, HALF), lambda b: (b, 0, 0))
    return pl.pallas_call(
        _pool_conv_kernel,
        out_shape=(jax.ShapeDtypeStruct((bsz, SEQ, HALF), BF16),) * 2,
        grid=(bsz,),
        in_specs=[part(0), part(1), part(2), part(3),
                  pl.BlockSpec((len(POOL_WINDOWS), GROUP_DIM, GROUP_DIM), lambda b: (0, 0, 0)),
                  pl.BlockSpec((1, HALF), lambda b: (0, 0)),
                  pl.BlockSpec((3, HALF), lambda b: (0, 0))],
        out_specs=(out, out),
        scratch_shapes=[pltpu.VMEM((SEQ + 2 * HALO, HALF), F32),
                        pltpu.VMEM((len(POOL_WINDOWS), SEQ, GROUP_DIM), BF16),
                        pltpu.VMEM((len(POOL_WINDOWS), 3, POOL_TILE, POOL_K), BF16)],
        compiler_params=_params(1),
        name="pool_short_conv",
    )(proj3d, proj3d, proj3d, proj3d, c_w, c_scale, d_conv_w)


def _out_ffn_kernel(x_ref, y0_ref, y1_ref, wo_ref, g_ref, wgu_ref, wd_ref, o_ref, h_ref, a_ref):
    sub = ROW_TILE // FFN_SUBTILES
    rows = [slice(s * sub, (s + 1) * sub) for s in range(FFN_SUBTILES)]
    for r in rows:
        mix = _bdot(y0_ref[r, :], wo_ref[:HALF, :]) + _bdot(y1_ref[r, :], wo_ref[HALF:, :])
        o_ref[r, :] = x_ref[r, :] + _rms(mix, g_ref[1:2, :])
        h_ref[r, :] = _rms(o_ref[r, :], g_ref[2:3, :]).astype(BF16)
    for r in rows:
        for k in range(D_FF // FF_CHUNK):
            gate = _bdot(h_ref[r, :], wgu_ref[:, k * FF_CHUNK:(k + 1) * FF_CHUNK])
            up = _bdot(h_ref[r, :], wgu_ref[:, D_FF + k * FF_CHUNK:D_FF + (k + 1) * FF_CHUNK])
            a_ref[r, k * FF_CHUNK:(k + 1) * FF_CHUNK] = (
                gate * jax.nn.sigmoid(gate) * up).astype(BF16)
    for r in rows:
        f = _bdot(a_ref[r, :], wd_ref[...])
        o_ref[r, :] = o_ref[r, :] + _rms(f, g_ref[3:4, :])


def _out_ffn(x2d, y0, y1, w_out, mix_layer, g, w_gu, w_down, layer):
    m = x2d.shape[0]
    tile = lambda n: pl.BlockSpec((ROW_TILE, n), lambda i: (i, 0))
    return pl.pallas_call(
        _out_ffn_kernel,
        out_shape=jax.ShapeDtypeStruct((m, D_MODEL), F32),
        grid=(m // ROW_TILE,),
        in_specs=[tile(D_MODEL), tile(HALF), tile(HALF),
                  _resident((None, D_MODEL, D_MODEL), lambda i: (mix_layer, 0, 0)),
                  pl.BlockSpec((None, 4, D_MODEL), lambda i: (layer, 0, 0)),
                  _resident((None, D_MODEL, 2 * D_FF), lambda i: (layer, 0, 0)),
                  _resident((None, D_FF, D_MODEL), lambda i: (layer, 0, 0))],
        out_specs=tile(D_MODEL),
        scratch_shapes=[pltpu.VMEM((ROW_TILE, D_MODEL), BF16),
                        pltpu.VMEM((ROW_TILE, D_FF), BF16)],
        compiler_params=_params(1, FFN_VMEM_LIMIT),
        name="out_proj_ffn",
    )(x2d, y0, y1, w_out, g, w_gu, w_down)


def kernel(x, norm_g, ffn_w_gu, ffn_w_down, ab_w_in, ab_w_out, a_ln_g, a_w_s, a_b_s, b_conv_w, b_filt_w1, b_filt_b1, b_filt_freq, b_filt_w2, b_filt_b2, b_filt_w3, b_decay, b_skip, cd_w_in, cd_w_out, c_w, c_scale, d_conv_w):
    bsz, seq, d = x.shape
    assert (seq, d) == (SEQ, D_MODEL) and norm_g.shape[0] == DEPTH
    m = bsz * seq
    bands = jnp.linspace(1e-4, FILTER_BANDS - 1, FILTER_BANDS, dtype=F32)[None, :]
    csv, csh = _dft_tables()
    x2d = x.reshape(m, d)
    ab_w_in, ab_w_out, cd_w_in, cd_w_out, ffn_w_gu, ffn_w_down = (
        w.astype(BF16) for w in (ab_w_in, ab_w_out, cd_w_in, cd_w_out, ffn_w_gu, ffn_w_down))
    for i in range(DEPTH):
        j = i // 2
        g = norm_g[i]
        if i % 2 == 0:
            proj, y0 = _in_proj_gmlp(x2d, g[0:1], ab_w_in, j, a_ln_g[j][None, :],
                                     a_w_s[j].astype(BF16), a_b_s[j][:, :, None])
            proj = proj.reshape(bsz, seq, -1)
            hidden = _filter_hidden(bands, b_filt_w1[j], b_filt_b1[j][None, :], b_filt_freq[j],
                                    b_filt_w2[j], b_filt_b2[j][None, :])
            kre, kim, hny = _filter_spectra(hidden, b_filt_w3[j], b_decay[j][None, :], csv)
            y1 = _hyena(proj, b_conv_w[j], b_skip[j], kre, kim, hny, csv, csh)
            w_out = ab_w_out
        else:
            proj = _in_proj(x2d, g[0:1], cd_w_in, j, 0).reshape(bsz, seq, -1)
            y0, y1 = _pool_conv(proj, c_w[j].astype(BF16), c_scale[j][None, :], d_conv_w[j])
            w_out = cd_w_out
        x2d = _out_ffn(x2d, y0.reshape(m, HALF), y1.reshape(m, HALF), w_out, j, norm_g,
                       ffn_w_gu, ffn_w_down, i)
    return x2d.reshape(bsz, seq, d)
```

```python
import functools
import math

import jax
import jax.numpy as jnp
from jax import lax
from jax.experimental import pallas as pl
from jax.experimental.pallas import tpu as pltpu

F32 = jnp.float32
BF16 = jnp.bfloat16
LANES = 128
BF16_ROWS = 16

D_MODEL = 1024
SEQ = 2048
DEPTH = 4
HALF = D_MODEL // 2
CHUNK = 128
N_CHUNKS = SEQ // CHUNK
A_GROUPS = 4
GROUP_DIM = HALF // A_GROUPS
FILTER_BANDS = 16
FILTER_HIDDEN = 64
POOL_WINDOWS = (2, 4, 8, 16)
D_FF = 2816
RMS_EPS = 1e-6
LN_EPS = 1e-5
HY_P = 512
HY_NB = SEQ // HY_P
HY_LAGS = 2 * HY_NB - 1
NFFT = 2 * HY_P
HY_NQ = 2
HY_FQ = HY_P // HY_NQ
HY_COEF = 9

ROW_TILE = 1024
FFN_SUBTILES = 4
FFN_VMEM_LIMIT = 60 * 1024 * 1024
IN_TILE = 2048
IN_SUBTILES = 4
IN_CHUNK = 256
FF_CHUNK = 256
HY_CT = 256
HY_CONV_ROWS = 128
MIX_ROWS = 16
MIX_LANES = 128
PW_ROWS = 128
HALO = 8
POOL_TILE = 256
POOL_K = 512
POOL_EDGE = 16
VMEM_LIMIT = 56 * 1024 * 1024


def _params(n_axes, vmem=VMEM_LIMIT):
    return pltpu.CompilerParams(
        dimension_semantics=("arbitrary",) * n_axes, vmem_limit_bytes=vmem)


def _resident(shape, index_map):
    return pl.BlockSpec(shape, index_map, pipeline_mode=pl.Buffered(1))


def _rms(x, g):
    return x * lax.rsqrt(jnp.mean(x * x, axis=-1, keepdims=True) + RMS_EPS) * g


def _bdot(a, b):
    return jnp.dot(a, b, preferred_element_type=F32)


def _fdot(a, b):
    return jnp.dot(a, b, preferred_element_type=F32, precision=lax.Precision.HIGHEST)


def _shift_rows(x, k, row):
    n = x.shape[0]
    y = pltpu.roll(x, k % n, axis=0)
    if k > 0:
        return jnp.where(row >= k, y, 0.0)
    return jnp.where(row < n + k, y, 0.0)


def _conv3(x, w, row):
    return (_shift_rows(x, 1, row) * w[0:1, :] + x * w[1:2, :]
            + _shift_rows(x, -1, row) * w[2:3, :])


def _dft_kernel(csv_ref, csh_ref):
    P, FQ = HY_P, HY_FQ
    f = lax.broadcasted_iota(jnp.int32, (P, LANES), 0)
    lane = lax.broadcasted_iota(jnp.int32, (P, LANES), 1)
    scale = 2.0 * math.pi / NFFT
    ang_b = ((f * lane) & (NFFT - 1)).astype(F32) * scale
    ang_a = ((f * (lane * LANES)) & (NFFT - 1)).astype(F32) * scale
    cb, sb = jnp.cos(ang_b), jnp.sin(ang_b)
    ca, sa = jnp.cos(ang_a), jnp.sin(ang_a)
    for a in range(P // LANES):
        cols = slice(a * LANES, (a + 1) * LANES)
        ca_a = ca[:, a:a + 1]
        sa_a = sa[:, a:a + 1]
        c = (ca_a * cb - sa_a * sb).astype(BF16)
        s = sa_a * cb + ca_a * sb
        s_v = jnp.where(f == 0, jnp.where((lane & 1) == 0, 1.0, -1.0), s).astype(BF16)
        for q in range(HY_NQ):
            csv_ref[2 * q * FQ:(2 * q + 1) * FQ, cols] = c[q * FQ:(q + 1) * FQ]
            csv_ref[(2 * q + 1) * FQ:(2 * q + 2) * FQ, cols] = s_v[q * FQ:(q + 1) * FQ]
        s_h = s
        if a == 0:
            s_h = jnp.where(lane == 0, jnp.where((f & 1) == 0, 1.0, -1.0), s)
        q, off = divmod(a * LANES, FQ)
        csh_ref[q, :, off:off + LANES] = c
        csh_ref[q, :, FQ + off:FQ + off + LANES] = s_h.astype(BF16)


def _dft_tables():
    return pl.pallas_call(
        _dft_kernel,
        out_shape=(jax.ShapeDtypeStruct((2 * HY_P, HY_P), BF16),
                   jax.ShapeDtypeStruct((HY_NQ, HY_P, 2 * HY_FQ), BF16)),
        compiler_params=_params(0),
        name="dft_tables",
    )()


def _filter_hidden_kernel(bands_ref, w1_ref, b1_ref, freq_ref, w2_ref, b2_ref, h_ref):
    L = SEQ
    r = lax.broadcasted_iota(jnp.int32, (L, 1), 0)
    t = jnp.where(pl.program_id(0) == 0, r, L - 1 - r).astype(F32)
    t01 = t / (L - 1)
    fw = (2 * math.pi * t / L) * bands_ref[...]
    pre = (t01 * w1_ref[0:1, :] + _fdot(jnp.cos(fw), w1_ref[1:1 + FILTER_BANDS, :])
           + _fdot(-jnp.sin(fw), w1_ref[1 + FILTER_BANDS:, :]) + b1_ref[...])
    h = jnp.sin(freq_ref[0:1, :] * pre)
    h_ref[...] = jnp.sin(freq_ref[1:2, :] * (_fdot(h, w2_ref[...]) + b2_ref[...]))


def _filter_hidden(bands, w1, b1, freq, w2, b2):
    const = lambda s: (0, 0)
    return pl.pallas_call(
        _filter_hidden_kernel,
        out_shape=jax.ShapeDtypeStruct((2, SEQ, FILTER_HIDDEN), F32),
        grid=(2,),
        in_specs=[pl.BlockSpec((1, FILTER_BANDS), const),
                  pl.BlockSpec((1 + 2 * FILTER_BANDS, FILTER_HIDDEN), const),
                  pl.BlockSpec((1, FILTER_HIDDEN), const),
                  pl.BlockSpec((2, FILTER_HIDDEN), const),
                  pl.BlockSpec((FILTER_HIDDEN, FILTER_HIDDEN), const),
                  pl.BlockSpec((1, FILTER_HIDDEN), const)],
        out_specs=pl.BlockSpec((None, SEQ, FILTER_HIDDEN), lambda s: (s, 0, 0)),
        compiler_params=_params(1),
        name="hyena_filter_hidden",
    )(bands, w1, b1, freq, w2, b2)


def _filter_kernel(hid_ref, w3f_ref, w3b_ref, decf_ref, decb_ref, csv_ref,
                   kre_ref, kim_ref, hny_ref):
    L, P, NB = SEQ, HY_P, HY_NB
    ct = kre_ref.shape[-1]
    row = lax.broadcasted_iota(jnp.int32, (L, ct), 0)
    t_asc = row.astype(F32) / (L - 1)
    t_dsc = (L - 1 - row).astype(F32) / (L - 1)
    dec_f = jnp.abs(decf_ref[...])
    dec_b = jnp.abs(decb_ref[...])
    h_asc = hid_ref[0]
    h_dsc = hid_ref[1]
    inner = (row & (P - 1)) != 0
    f_asc = _fdot(h_asc, w3f_ref[...]) * jnp.exp(-t_asc * dec_f)
    b_dsc = _fdot(h_dsc, w3b_ref[...]) * jnp.exp(-t_dsc * dec_b)
    f_dsc = _fdot(h_dsc, w3f_ref[...]) * jnp.exp(-t_dsc * dec_f)
    f_dsc = jnp.where(inner, _shift_rows(f_dsc, 1, row), 0.0)
    b_asc = _fdot(h_asc, w3b_ref[...]) * jnp.exp(-t_asc * dec_b)
    b_asc = jnp.where(inner, _shift_rows(b_asc, 1, row), 0.0)

    def blk(x, i):
        return x[i * P:(i + 1) * P, :]

    frow = lax.broadcasted_iota(jnp.int32, (P, ct), 0)
    wgt = jnp.where(frow == 0, 1.0 / NFFT, 2.0 / NFFT)
    sign = jnp.where((frow & 1) == 0, 1.0, -1.0)
    FQ = HY_FQ
    h = {}
    for idx in range(HY_LAGS):
        d = idx - (NB - 1)
        if d >= 1:
            kp, kn = blk(f_asc, d), blk(f_dsc, NB - d)
        elif d == 0:
            kp, kn = blk(f_asc, 0), blk(b_asc, 0)
        else:
            kp, kn = blk(b_dsc, NB + d), blk(b_asc, -d)
        ev = kp + kn
        od = kp - kn
        evb = ev.astype(BF16)
        odb = od.astype(BF16)
        re = jnp.concatenate([_bdot(csv_ref[2 * q * FQ:(2 * q + 1) * FQ, :], evb)
                              for q in range(HY_NQ)], axis=0)
        im = jnp.concatenate([_bdot(csv_ref[(2 * q + 1) * FQ:(2 * q + 2) * FQ, :], odb)
                              for q in range(HY_NQ)], axis=0)
        h[d] = (re * wgt, jnp.where(frow == 0, 0.0, -im * wgt))
        hny_ref[idx:idx + 1, :] = jnp.sum(ev * sign, axis=0, keepdims=True) * (1.0 / NFFT)

    def sub(a, b):
        return (a[0] - b[0], a[1] - b[1])

    def coef3(a0, a_dn, a_up):
        return [a0, sub(a_dn, a0), sub(a_up, a0)]

    coefs = (coef3(h[0], h[1], h[-1])
             + coef3(sub(h[2], h[0]), sub(h[3], h[1]), sub(h[1], h[-1]))
             + coef3(sub(h[-2], h[0]), sub(h[-1], h[1]), sub(h[-3], h[-1])))
    for c, (re, im) in enumerate(coefs):
        kre_ref[c] = re
        kim_ref[c] = im


def _filter_spectra(hidden, w3, decay, csv):
    ct = HY_CT
    nct = HALF // ct
    fcol = lambda o, c: (0, o * nct + c)
    bcol = lambda o, c: (0, (2 + o) * nct + c)
    hspec = pl.BlockSpec((None, HY_COEF, HY_P, ct), lambda o, c: (o, 0, 0, c))
    return pl.pallas_call(
        _filter_kernel,
        out_shape=(jax.ShapeDtypeStruct((2, HY_COEF, HY_P, HALF), F32),
                   jax.ShapeDtypeStruct((2, HY_COEF, HY_P, HALF), F32),
                   jax.ShapeDtypeStruct((2, HY_LAGS, HALF), F32)),
        grid=(2, nct),
        in_specs=[
            pl.BlockSpec((2, SEQ, FILTER_HIDDEN), lambda o, c: (0, 0, 0)),
            pl.BlockSpec((FILTER_HIDDEN, ct), fcol),
            pl.BlockSpec((FILTER_HIDDEN, ct), bcol),
            pl.BlockSpec((1, ct), fcol),
            pl.BlockSpec((1, ct), bcol),
            pl.BlockSpec((2 * HY_P, HY_P), lambda o, c: (0, 0)),
        ],
        out_specs=(hspec, hspec,
                   pl.BlockSpec((None, HY_LAGS, ct), lambda o, c: (o, 0, c))),
        compiler_params=_params(2),
        name="hyena_filter_spectra",
    )(hidden, w3, w3, decay, decay, csv)


def _in_proj_kernel(x_ref, g_ref, w_ref, o_ref, h_ref):
    sub = IN_TILE // IN_SUBTILES
    rows = [slice(s * sub, (s + 1) * sub) for s in range(IN_SUBTILES)]
    for r in rows:
        h_ref[r, :] = _rms(x_ref[r, :], g_ref[...]).astype(BF16)
    for r in rows:
        for c0 in range(0, o_ref.shape[1], IN_CHUNK):
            o_ref[r, c0:c0 + IN_CHUNK] = _bdot(
                h_ref[r, :], w_ref[:, c0:c0 + IN_CHUNK]).astype(BF16)


def _in_proj(x2d, g, w, layer):
    m = x2d.shape[0]
    n_out = w.shape[2]
    return pl.pallas_call(
        _in_proj_kernel,
        out_shape=jax.ShapeDtypeStruct((m, n_out), BF16),
        grid=(m // IN_TILE,),
        in_specs=[pl.BlockSpec((IN_TILE, D_MODEL), lambda i: (i, 0)),
                  pl.BlockSpec((1, D_MODEL), lambda i: (0, 0)),
                  _resident((None, D_MODEL, n_out), lambda i: (layer, 0, 0))],
        out_specs=pl.BlockSpec((IN_TILE, n_out), lambda i: (i, 0)),
        scratch_shapes=[pltpu.VMEM((IN_TILE, D_MODEL), BF16)],
        compiler_params=_params(1),
        name="rms_in_proj",
    )(x2d, g, w)


def _in_proj_gmlp_kernel(x_ref, g_ref, w_ref, lng_ref, ws_ref, bs_ref, o_ref, ya_ref,
                         h_ref, za_ref, vn_ref):
    sub = IN_TILE // IN_SUBTILES
    rows = [slice(s * sub, (s + 1) * sub) for s in range(IN_SUBTILES)]
    for r in rows:
        h_ref[r, :] = _rms(x_ref[r, :], g_ref[...]).astype(BF16)
    n_gelu = 2 * HALF
    n_out = w_ref.shape[1]
    gelu_chunks = list(range(0, n_gelu, IN_CHUNK))
    plain_chunks = list(range(n_gelu, n_out, IN_CHUNK))
    order = [c for pair in zip(gelu_chunks, plain_chunks) for c in pair]
    order += gelu_chunks[len(plain_chunks):] + plain_chunks[len(gelu_chunks):]
    def layer_norm(s):
        for i in range(sub // PW_ROWS):
            rr = slice(s * sub + i * PW_ROWS, s * sub + (i + 1) * PW_ROWS)
            v = za_ref[rr, HALF:].astype(F32)
            xc = v - jnp.mean(v, axis=-1, keepdims=True)
            y = xc * lax.rsqrt(jnp.mean(xc * xc, axis=-1, keepdims=True) + LN_EPS) * lng_ref[...]
            vn_ref[rr, :] = y.astype(BF16)

    for s, r in enumerate(rows):
        for k, c0 in enumerate(order):
            p = _bdot(h_ref[r, :], w_ref[:, c0:c0 + IN_CHUNK])
            if c0 < n_gelu:
                za_ref[r, c0:c0 + IN_CHUNK] = jax.nn.gelu(p).astype(BF16)
            else:
                o_ref[r, c0 - n_gelu:c0 - n_gelu + IN_CHUNK] = p.astype(BF16)
            if k == 1 and s > 0:
                layer_norm(s - 1)
    layer_norm(len(rows) - 1)

    for g in range(A_GROUPS):
        cols = slice(g * GROUP_DIM, (g + 1) * GROUP_DIM)
        vg = jnp.concatenate(
            [vn_ref[n * CHUNK:(n + 1) * CHUNK, cols] for n in range(N_CHUNKS)], axis=1)
        sg = _bdot(ws_ref[g], vg) + bs_ref[g]
        for n in range(N_CHUNKS):
            rr = slice(n * CHUNK, (n + 1) * CHUNK)
            u = za_ref[rr, cols].astype(F32)
            ya_ref[rr, cols] = (u * sg[:, n * GROUP_DIM:(n + 1) * GROUP_DIM]).astype(BF16)


def _in_proj_gmlp(x2d, g, w, layer, ln_g, w_s, b_s):
    assert IN_TILE == SEQ
    m = x2d.shape[0]
    n_out = w.shape[2]
    n_hy = n_out - 2 * HALF
    return pl.pallas_call(
        _in_proj_gmlp_kernel,
        out_shape=(jax.ShapeDtypeStruct((m, n_hy), BF16), jax.ShapeDtypeStruct((m, HALF), BF16)),
        grid=(m // IN_TILE,),
        in_specs=[pl.BlockSpec((IN_TILE, D_MODEL), lambda i: (i, 0)),
                  pl.BlockSpec((1, D_MODEL), lambda i: (0, 0)),
                  _resident((None, D_MODEL, n_out), lambda i: (layer, 0, 0)),
                  pl.BlockSpec((1, HALF), lambda i: (0, 0)),
                  pl.BlockSpec((A_GROUPS, CHUNK, CHUNK), lambda i: (0, 0, 0)),
                  pl.BlockSpec((A_GROUPS, CHUNK, 1), lambda i: (0, 0, 0))],
        out_specs=(pl.BlockSpec((IN_TILE, n_hy), lambda i: (i, 0)),
                   pl.BlockSpec((IN_TILE, HALF), lambda i: (i, 0))),
        scratch_shapes=[pltpu.VMEM((IN_TILE, D_MODEL), BF16),
                        pltpu.VMEM((IN_TILE, 2 * HALF), BF16),
                        pltpu.VMEM((IN_TILE, HALF), BF16)],
        compiler_params=_params(1, FFN_VMEM_LIMIT),
        name="rms_in_proj_gmlp",
    )(x2d, g, w, ln_g, w_s, b_s)


def _cmul(z, h):
    return z[0] * h[0] + z[1] * h[1], z[1] * h[0] - z[0] * h[1]


def _cadd(a, b):
    return a[0] + b[0], a[1] + b[1]


def _mix_tile(o, q, row0, lh, zs_ref, kre_ref, kim_ref, ys_ref):
    FQ = HY_FQ
    fr = slice(q * FQ + row0, q * FQ + row0 + MIX_ROWS)
    rr = slice(row0, row0 + MIX_ROWS)
    rm = slice(FQ + row0, FQ + row0 + MIX_ROWS)
    lanes = slice(lh * MIX_LANES, (lh + 1) * MIX_LANES)

    def coef(c):
        return kre_ref[o, c, fr, lanes], kim_ref[o, c, fr, lanes]

    def toep2(c, v0, v1):
        q1 = _cmul(_cadd(v0, v1), coef(c))
        q2 = _cmul(v0, coef(c + 1))
        q3 = _cmul(v1, coef(c + 2))
        return _cadd(q1, q3), _cadd(q1, q2)

    z = [(zs_ref[j, rr, lanes], zs_ref[j, rm, lanes]) for j in range(HY_NB)]
    p1 = toep2(0, _cadd(z[0], z[2]), _cadd(z[1], z[3]))
    p2 = toep2(3, z[0], z[1])
    p3 = toep2(6, z[2], z[3])
    y = [_cadd(p1[0], p3[0]), _cadd(p1[1], p3[1]), _cadd(p1[0], p2[0]), _cadd(p1[1], p2[1])]
    for i in range(HY_NB):
        ys_ref[i, rr, lanes] = y[i][0].astype(BF16)
        ys_ref[i, rm, lanes] = y[i][1].astype(BF16)


def _conv3_rows(src_ref, w, start):
    L, R, T, H = SEQ, HY_CONV_ROWS, BF16_ROWS, HALO
    x = src_ref[start:start + R, :].astype(F32)
    zeros = jnp.zeros((H, x.shape[1]), F32)
    prev = src_ref[start - T:start, :].astype(F32)[T - H:] if start > 0 else zeros
    nxt = src_ref[start + R:start + R + T, :].astype(F32)[:H] if start + R < L else zeros
    a = jnp.concatenate([prev, x, nxt], axis=0)
    n = R + 2 * H
    y = pltpu.roll(a, 1, axis=0) * w[0:1, :] + a * w[1:2, :] + pltpu.roll(a, n - 1, axis=0) * w[2:3, :]
    return y[H:H + R]


def _hyena_kernel(v_ref, g1_ref, g2_ref, wv_ref, wg1_ref, wg2_ref, skip_ref,
                  kre_ref, kim_ref, hny_ref, csv_ref, csh_ref, o_ref,
                  z_ref, gate_ref, zcat_ref, zs0_ref, zs1_ref, ys0_ref, ys1_ref, acc_ref):
    P, NB, FQ, R = HY_P, HY_NB, HY_FQ, HY_CONV_ROWS
    ct = o_ref.shape[1]
    steps = FQ // MIX_ROWS // NB

    def chunk(q):
        return slice(2 * q * FQ, 2 * (q + 1) * FQ)

    def mix_share(o, q, k, zs_ref, ys_ref):
        for s in range(steps):
            for lh in range(ct // MIX_LANES):
                _mix_tile(o, q, (k * steps + s) * MIX_ROWS, lh, zs_ref, kre_ref, kim_ref, ys_ref)

    for j in range(NB):
        for r in range(P // R):
            zr = _conv3_rows(v_ref, wv_ref[...], j * P + r * R)
            z_ref[j * P + r * R:j * P + (r + 1) * R, :] = zr
            zcat_ref[j, r * R:(r + 1) * R, :] = zr.astype(BF16)

    def always(k):
        return pl.program_id(1) + k >= 0

    def phase_a(o, g_ref, wg_ref):
        for j in range(NB):
            zs0_ref[j] = _bdot(csv_ref[chunk(0), :], zcat_ref[j])
            for r in range(P // R):
                start = j * P + r * R
                gate_ref[start:start + R, :] = _conv3_rows(g_ref, wg_ref[...], start)

    def phase_b(o):
        for j in range(NB):
            zs1_ref[j] = _bdot(csv_ref[chunk(1), :], zcat_ref[j])
            mix_share(o, 0, j, zs0_ref, ys0_ref)
        tile = slice(FQ, FQ + MIX_ROWS)
        first = lax.broadcasted_iota(jnp.int32, (MIX_ROWS, ct), 0) == 0
        z_ny = [zs0_ref[j, tile, :] for j in range(NB)]
        for i in range(NB):
            y_ny = sum(z_ny[j] * hny_ref[o, i - j + NB - 1:i - j + NB, :] for j in range(NB))
            ys0_ref[i, tile, :] = jnp.where(
                first, y_ny, ys0_ref[i, tile, :].astype(F32)).astype(BF16)

    def phase_c(o):
        for i in range(NB):
            acc_ref[i] = _bdot(csh_ref[0], ys0_ref[i])
            mix_share(o, 1, i, zs1_ref, ys1_ref)

    def phase_d(o):
        for i in range(NB):
            rows = slice(i * P, (i + 1) * P)
            conv = acc_ref[i] + _bdot(csh_ref[1], ys1_ref[i])
            z_new = gate_ref[rows, :] * (conv + z_ref[rows, :] * skip_ref[o:o + 1, :])
            if o == 0:
                z_ref[rows, :] = z_new
                zcat_ref[i] = z_new.astype(BF16)
            else:
                o_ref[rows, :] = z_new.astype(BF16)

    for o, (g_ref, wg_ref) in enumerate(((g1_ref, wg1_ref), (g2_ref, wg2_ref))):
        phases = (functools.partial(phase_a, o, g_ref, wg_ref), functools.partial(phase_b, o),
                  functools.partial(phase_c, o), functools.partial(phase_d, o))
        for k, phase in enumerate(phases):
            pl.when(always(len(phases) * o + k))(phase)


def _hyena(proj3d, conv_w, skip, kre, kim, hny, csv, csh):
    bsz = proj3d.shape[0]
    ct = HY_CT
    nct = HALF // ct
    assert proj3d.shape[2] == 3 * HALF

    def slab(part):
        return pl.BlockSpec((None, SEQ, ct), lambda c, b: (b, 0, part * nct + c))

    def cw(part):
        return pl.BlockSpec((3, ct), lambda c, b: (0, part * nct + c))

    assert HY_NB == 4 and HY_NQ == 2 and HY_FQ % (MIX_ROWS * HY_NB) == 0 and ct % MIX_LANES == 0
    hspec = _resident((2, HY_COEF, HY_P, ct), lambda c, b: (0, 0, 0, c))
    const = lambda c, b: (0, 0)
    return pl.pallas_call(
        _hyena_kernel,
        out_shape=jax.ShapeDtypeStruct((bsz, SEQ, HALF), BF16),
        grid=(nct, bsz),
        in_specs=[slab(0), slab(1), slab(2), cw(0), cw(1), cw(2),
                  pl.BlockSpec((2, ct), lambda c, b: (0, c)),
                  hspec, hspec,
                  pl.BlockSpec((2, HY_LAGS, ct), lambda c, b: (0, 0, c)),
                  pl.BlockSpec((2 * HY_P, HY_P), const), pl.BlockSpec((HY_NQ, HY_P, 2 * HY_FQ), lambda c, b: (0, 0, 0))],
        out_specs=pl.BlockSpec((None, SEQ, ct), lambda c, b: (b, 0, c)),
        scratch_shapes=[pltpu.VMEM((SEQ, ct), F32), pltpu.VMEM((SEQ, ct), F32),
                        pltpu.VMEM((HY_NB, HY_P, ct), BF16),
                        pltpu.VMEM((HY_NB, 2 * HY_FQ, ct), F32),
                        pltpu.VMEM((HY_NB, 2 * HY_FQ, ct), F32),
                        pltpu.VMEM((HY_NB, 2 * HY_FQ, ct), BF16),
                        pltpu.VMEM((HY_NB, 2 * HY_FQ, ct), BF16),
                        pltpu.VMEM((HY_NB, HY_P, ct), F32)],
        compiler_params=_params(2),
        name="hyena_long_conv",
    )(proj3d, proj3d, proj3d, conv_w, conv_w, conv_w, skip, kre, kim, hny, csv, csh)


def _window_sum(a, w):
    n = a.shape[0]
    right = w - w // 2 - 1
    s = a
    k = 1
    while k < w:
        s = s + pltpu.roll(s, k, axis=0)
        k *= 2
    return pltpu.roll(s, n - right, axis=0) if right else s


def _pool_conv_kernel(p_ref, b_ref, c_ref, h_ref, cw_ref, scale_ref, dw_ref, yc_ref, yd_ref,
                      chs_ref, pooled_ref, band_ref):
    L, R, n = SEQ, PW_ROWS, PW_ROWS + 2 * HALO
    offs = (0, (POOL_K - POOL_TILE) // 2, POOL_K - POOL_TILE)

    @pl.when(pl.program_id(0) == 0)
    def _():
        r = lax.broadcasted_iota(jnp.int32, (POOL_TILE, POOL_K), 0)
        c = lax.broadcasted_iota(jnp.int32, (POOL_TILE, POOL_K), 1)
        for g, w in enumerate(POOL_WINDOWS):
            left = w // 2
            for kind, off in enumerate(offs):
                d = c - off - r + left
                m = jnp.where((d >= 0) & (d < w), 1.0 / w, 0.0) - jnp.where(d == left, 1.0, 0.0)
                band_ref[g, kind] = m.astype(BF16)

    zeros = jnp.zeros((HALO, HALF), F32)
    chs_ref[0:HALO, :] = zeros
    chs_ref[L + HALO:L + 2 * HALO, :] = zeros

    for i in range(L // R):
        r = slice(i * R, (i + 1) * R)
        chs_ref[HALO + i * R:HALO + (i + 1) * R, :] = (
            c_ref[r, :].astype(F32) * h_ref[r, :].astype(F32))

    n_tiles = L // POOL_TILE
    for g, w in enumerate(POOL_WINDOWS):
        cols = slice(g * GROUP_DIM, (g + 1) * GROUP_DIM)
        left = w // 2
        right = w - left - 1

        for k in range(n_tiles):
            kind = 0 if k == 0 else (2 if k == n_tiles - 1 else 1)
            t0 = k * POOL_TILE
            ws = t0 - offs[kind]
            pooled_ref[g, t0:t0 + POOL_TILE, :] = _bdot(
                band_ref[g, kind], p_ref[ws:ws + POOL_K, cols]).astype(BF16)

        for top in (True, False):
            pad = jnp.zeros((POOL_EDGE, GROUP_DIM), F32)
            if top:
                a = jnp.concatenate([pad, p_ref[0:2 * POOL_EDGE, cols].astype(F32)], axis=0)
                first = 0
            else:
                a = jnp.concatenate([p_ref[L - 2 * POOL_EDGE:L, cols].astype(F32), pad], axis=0)
                first = L - POOL_EDGE
            keep = slice(POOL_EDGE, 2 * POOL_EDGE)
            t = first + lax.broadcasted_iota(jnp.int32, (POOL_EDGE, GROUP_DIM), 0)
            cnt = (jnp.minimum(t + right, L - 1) - jnp.maximum(t - left, 0) + 1).astype(F32)
            pooled = _window_sum(a, w)[keep] / cnt - a[keep]
            pooled_ref[g, first:first + POOL_EDGE, :] = pooled.astype(BF16)

        dw = dw_ref[:, cols]
        for i in range(L // R):
            r0 = i * R
            e = chs_ref[r0:r0 + n, cols]
            conv = (pltpu.roll(e, 1, axis=0) * dw[0:1, :] + e * dw[1:2, :]
                    + pltpu.roll(e, n - 1, axis=0) * dw[2:3, :])
            yd = b_ref[r0:r0 + R, cols].astype(F32) * conv[HALO:HALO + R]
            yd_ref[r0:r0 + R, cols] = yd.astype(BF16)

        y = _bdot(pooled_ref[g], cw_ref[g]) * scale_ref[:, cols]
        yc_ref[:, cols] = y.astype(BF16)


def _pool_conv(proj3d, c_w, c_scale, d_conv_w):
    bsz = proj3d.shape[0]

    def part(k):
        return pl.BlockSpec((None, SEQ, HALF), lambda b: (b, 0, k))

    out = pl.BlockSpec((None, SEQ, HALF), lambda b: (b, 0, 0))
    return pl.pallas_call(
        _pool_conv_kernel,
        out_shape=(jax.ShapeDtypeStruct((bsz, SEQ, HALF), BF16),) * 2,
        grid=(bsz,),
        in_specs=[part(0), part(1), part(2), part(3),
                  pl.BlockSpec((len(POOL_WINDOWS), GROUP_DIM, GROUP_DIM), lambda b: (0, 0, 0)),
                  pl.BlockSpec((1, HALF), lambda b: (0, 0)),
                  pl.BlockSpec((3, HALF), lambda b: (0, 0))],
        out_specs=(out, out),
        scratch_shapes=[pltpu.VMEM((SEQ + 2 * HALO, HALF), F32),
                        pltpu.VMEM((len(POOL_WINDOWS), SEQ, GROUP_DIM), BF16),
                        pltpu.VMEM((len(POOL_WINDOWS), 3, POOL_TILE, POOL_K), BF16)],
        compiler_params=_params(1),
        name="pool_short_conv",
    )(proj3d, proj3d, proj3d, proj3d, c_w, c_scale, d_conv_w)


def _out_ffn_kernel(x_ref, y0_ref, y1_ref, wo_ref, g_ref, wgu_ref, wd_ref, o_ref, h_ref, a_ref):
    sub = ROW_TILE // FFN_SUBTILES
    rows = [slice(s * sub, (s + 1) * sub) for s in range(FFN_SUBTILES)]
    for r in rows:
        mix = _bdot(y0_ref[r, :], wo_ref[:HALF, :]) + _bdot(y1_ref[r, :], wo_ref[HALF:, :])
        o_ref[r, :] = x_ref[r, :] + _rms(mix, g_ref[1:2, :])
        h_ref[r, :] = _rms(o_ref[r, :], g_ref[2:3, :]).astype(BF16)
    for r in rows:
        for k in range(D_FF // FF_CHUNK):
            gate = _bdot(h_ref[r, :], wgu_ref[:, k * FF_CHUNK:(k + 1) * FF_CHUNK])
            up = _bdot(h_ref[r, :], wgu_ref[:, D_FF + k * FF_CHUNK:D_FF + (k + 1) * FF_CHUNK])
            a_ref[r, k * FF_CHUNK:(k + 1) * FF_CHUNK] = (
                gate * jax.nn.sigmoid(gate) * up).astype(BF16)
    for r in rows:
        f = _bdot(a_ref[r, :], wd_ref[...])
        o_ref[r, :] = o_ref[r, :] + _rms(f, g_ref[3:4, :])


def _out_ffn(x2d, y0, y1, w_out, mix_layer, g, w_gu, w_down, layer):
    m = x2d.shape[0]
    tile = lambda n: pl.BlockSpec((ROW_TILE, n), lambda i: (i, 0))
    return pl.pallas_call(
        _out_ffn_kernel,
        out_shape=jax.ShapeDtypeStruct((m, D_MODEL), F32),
        grid=(m // ROW_TILE,),
        in_specs=[tile(D_MODEL), tile(HALF), tile(HALF),
                  _resident((None, D_MODEL, D_MODEL), lambda i: (mix_layer, 0, 0)),
                  pl.BlockSpec((None, 4, D_MODEL), lambda i: (layer, 0, 0)),
                  _resident((None, D_MODEL, 2 * D_FF), lambda i: (layer, 0, 0)),
                  _resident((None, D_FF, D_MODEL), lambda i: (layer, 0, 0))],
        out_specs=tile(D_MODEL),
        scratch_shapes=[pltpu.VMEM((ROW_TILE, D_MODEL), BF16),
                        pltpu.VMEM((ROW_TILE, D_FF), BF16)],
        compiler_params=_params(1, FFN_VMEM_LIMIT),
        name="out_proj_ffn",
    )(x2d, y0, y1, w_out, g, w_gu, w_down)


def kernel(x, norm_g, ffn_w_gu, ffn_w_down, ab_w_in, ab_w_out, a_ln_g, a_w_s, a_b_s, b_conv_w, b_filt_w1, b_filt_b1, b_filt_freq, b_filt_w2, b_filt_b2, b_filt_w3, b_decay, b_skip, cd_w_in, cd_w_out, c_w, c_scale, d_conv_w):
    bsz, seq, d = x.shape
    assert (seq, d) == (SEQ, D_MODEL) and norm_g.shape[0] == DEPTH
    m = bsz * seq
    bands = jnp.linspace(1e-4, FILTER_BANDS - 1, FILTER_BANDS, dtype=F32)[None, :]
    csv, csh = _dft_tables()
    x2d = x.reshape(m, d)
    ab_w_in, ab_w_out, cd_w_in, cd_w_out, ffn_w_gu, ffn_w_down = (
        w.astype(BF16) for w in (ab_w_in, ab_w_out, cd_w_in, cd_w_out, ffn_w_gu, ffn_w_down))
    for i in range(DEPTH):
        j = i // 2
        g = norm_g[i]
        if i % 2 == 0:
            proj, y0 = _in_proj_gmlp(x2d, g[0:1], ab_w_in, j, a_ln_g[j][None, :],
                                     a_w_s[j].astype(BF16), a_b_s[j][:, :, None])
            proj = proj.reshape(bsz, seq, -1)
            hidden = _filter_hidden(bands, b_filt_w1[j], b_filt_b1[j][None, :], b_filt_freq[j],
                                    b_filt_w2[j], b_filt_b2[j][None, :])
            kre, kim, hny = _filter_spectra(hidden, b_filt_w3[j], b_decay[j][None, :], csv)
            y1 = _hyena(proj, b_conv_w[j], b_skip[j], kre, kim, hny, csv, csh)
            w_out = ab_w_out
        else:
            proj = _in_proj(x2d, g[0:1], cd_w_in, j).reshape(bsz, seq, -1)
            y0, y1 = _pool_conv(proj, c_w[j].astype(BF16), c_scale[j][None, :], d_conv_w[j])
            w_out = cd_w_out
        x2d = _out_ffn(x2d, y0.reshape(m, HALF), y1.reshape(m, HALF), w_out, j, norm_g,
                       ffn_w_gu, ffn_w_down, i)
    return x2d.reshape(bsz, seq, d)
```

```python
import functools
import math

import jax
import jax.numpy as jnp
from jax import lax
from jax.experimental import pallas as pl
from jax.experimental.pallas import tpu as pltpu

F32 = jnp.float32
BF16 = jnp.bfloat16
LANES = 128
BF16_ROWS = 16

D_MODEL = 1024
SEQ = 2048
DEPTH = 4
HALF = D_MODEL // 2
CHUNK = 128
N_CHUNKS = SEQ // CHUNK
A_GROUPS = 4
GROUP_DIM = HALF // A_GROUPS
FILTER_BANDS = 16
FILTER_HIDDEN = 64
POOL_WINDOWS = (2, 4, 8, 16)
D_FF = 2816
RMS_EPS = 1e-6
LN_EPS = 1e-5
HY_P = 512
HY_NB = SEQ // HY_P
HY_LAGS = 2 * HY_NB - 1
NFFT = 2 * HY_P
HY_NQ = 2
HY_FQ = HY_P // HY_NQ
HY_COEF = 9

ROW_TILE = 1024
FFN_SUBTILES = 4
FFN_VMEM_LIMIT = 60 * 1024 * 1024
IN_TILE = 2048
IN_SUBTILES = 4
IN_GMLP_SUBTILES = 8
IN_CHUNK = 256
FF_CHUNK = 256
HY_CT = 256
HY_CONV_ROWS = 128
MIX_ROWS = 16
MIX_LANES = 128
PW_ROWS = 128
HALO = 8
POOL_TILE = 256
POOL_K = 512
POOL_EDGE = 16
VMEM_LIMIT = 56 * 1024 * 1024


def _params(n_axes, vmem=VMEM_LIMIT):
    return pltpu.CompilerParams(
        dimension_semantics=("arbitrary",) * n_axes, vmem_limit_bytes=vmem)


def _resident(shape, index_map):
    return pl.BlockSpec(shape, index_map, pipeline_mode=pl.Buffered(1))


def _rms(x, g):
    return x * lax.rsqrt(jnp.mean(x * x, axis=-1, keepdims=True) + RMS_EPS) * g


def _bdot(a, b):
    return jnp.dot(a, b, preferred_element_type=F32)


def _fdot(a, b):
    return jnp.dot(a, b, preferred_element_type=F32, precision=lax.Precision.HIGHEST)


def _shift_rows(x, k, row):
    n = x.shape[0]
    y = pltpu.roll(x, k % n, axis=0)
    if k > 0:
        return jnp.where(row >= k, y, 0.0)
    return jnp.where(row < n + k, y, 0.0)


def _conv3(x, w, row):
    return (_shift_rows(x, 1, row) * w[0:1, :] + x * w[1:2, :]
            + _shift_rows(x, -1, row) * w[2:3, :])


def _dft_kernel(csv_ref, csh_ref):
    P, FQ = HY_P, HY_FQ
    f = lax.broadcasted_iota(jnp.int32, (P, LANES), 0)
    lane = lax.broadcasted_iota(jnp.int32, (P, LANES), 1)
    scale = 2.0 * math.pi / NFFT
    ang_b = ((f * lane) & (NFFT - 1)).astype(F32) * scale
    ang_a = ((f * (lane * LANES)) & (NFFT - 1)).astype(F32) * scale
    cb, sb = jnp.cos(ang_b), jnp.sin(ang_b)
    ca, sa = jnp.cos(ang_a), jnp.sin(ang_a)
    for a in range(P // LANES):
        cols = slice(a * LANES, (a + 1) * LANES)
        ca_a = ca[:, a:a + 1]
        sa_a = sa[:, a:a + 1]
        c = (ca_a * cb - sa_a * sb).astype(BF16)
        s = sa_a * cb + ca_a * sb
        s_v = jnp.where(f == 0, jnp.where((lane & 1) == 0, 1.0, -1.0), s).astype(BF16)
        for q in range(HY_NQ):
            csv_ref[2 * q * FQ:(2 * q + 1) * FQ, cols] = c[q * FQ:(q + 1) * FQ]
            csv_ref[(2 * q + 1) * FQ:(2 * q + 2) * FQ, cols] = s_v[q * FQ:(q + 1) * FQ]
        s_h = s
        if a == 0:
            s_h = jnp.where(lane == 0, jnp.where((f & 1) == 0, 1.0, -1.0), s)
        q, off = divmod(a * LANES, FQ)
        csh_ref[q, :, off:off + LANES] = c
        csh_ref[q, :, FQ + off:FQ + off + LANES] = s_h.astype(BF16)


def _dft_tables():
    return pl.pallas_call(
        _dft_kernel,
        out_shape=(jax.ShapeDtypeStruct((2 * HY_P, HY_P), BF16),
                   jax.ShapeDtypeStruct((HY_NQ, HY_P, 2 * HY_FQ), BF16)),
        compiler_params=_params(0),
        name="dft_tables",
    )()


def _filter_hidden_kernel(bands_ref, w1_ref, b1_ref, freq_ref, w2_ref, b2_ref, h_ref):
    L = SEQ
    r = lax.broadcasted_iota(jnp.int32, (L, 1), 0)
    t = jnp.where(pl.program_id(0) == 0, r, L - 1 - r).astype(F32)
    t01 = t / (L - 1)
    fw = (2 * math.pi * t / L) * bands_ref[...]
    pre = (t01 * w1_ref[0:1, :] + _fdot(jnp.cos(fw), w1_ref[1:1 + FILTER_BANDS, :])
           + _fdot(-jnp.sin(fw), w1_ref[1 + FILTER_BANDS:, :]) + b1_ref[...])
    h = jnp.sin(freq_ref[0:1, :] * pre)
    h_ref[...] = jnp.sin(freq_ref[1:2, :] * (_fdot(h, w2_ref[...]) + b2_ref[...]))


def _filter_hidden(bands, w1, b1, freq, w2, b2):
    const = lambda s: (0, 0)
    return pl.pallas_call(
        _filter_hidden_kernel,
        out_shape=jax.ShapeDtypeStruct((2, SEQ, FILTER_HIDDEN), F32),
        grid=(2,),
        in_specs=[pl.BlockSpec((1, FILTER_BANDS), const),
                  pl.BlockSpec((1 + 2 * FILTER_BANDS, FILTER_HIDDEN), const),
                  pl.BlockSpec((1, FILTER_HIDDEN), const),
                  pl.BlockSpec((2, FILTER_HIDDEN), const),
                  pl.BlockSpec((FILTER_HIDDEN, FILTER_HIDDEN), const),
                  pl.BlockSpec((1, FILTER_HIDDEN), const)],
        out_specs=pl.BlockSpec((None, SEQ, FILTER_HIDDEN), lambda s: (s, 0, 0)),
        compiler_params=_params(1),
        name="hyena_filter_hidden",
    )(bands, w1, b1, freq, w2, b2)


def _filter_kernel(hid_ref, w3f_ref, w3b_ref, decf_ref, decb_ref, csv_ref,
                   kre_ref, kim_ref, hny_ref):
    L, P, NB = SEQ, HY_P, HY_NB
    ct = kre_ref.shape[-1]
    row = lax.broadcasted_iota(jnp.int32, (L, ct), 0)
    t_asc = row.astype(F32) / (L - 1)
    t_dsc = (L - 1 - row).astype(F32) / (L - 1)
    dec_f = jnp.abs(decf_ref[...])
    dec_b = jnp.abs(decb_ref[...])
    h_asc = hid_ref[0]
    h_dsc = hid_ref[1]
    inner = (row & (P - 1)) != 0
    f_asc = _fdot(h_asc, w3f_ref[...]) * jnp.exp(-t_asc * dec_f)
    b_dsc = _fdot(h_dsc, w3b_ref[...]) * jnp.exp(-t_dsc * dec_b)
    f_dsc = _fdot(h_dsc, w3f_ref[...]) * jnp.exp(-t_dsc * dec_f)
    f_dsc = jnp.where(inner, _shift_rows(f_dsc, 1, row), 0.0)
    b_asc = _fdot(h_asc, w3b_ref[...]) * jnp.exp(-t_asc * dec_b)
    b_asc = jnp.where(inner, _shift_rows(b_asc, 1, row), 0.0)

    def blk(x, i):
        return x[i * P:(i + 1) * P, :]

    frow = lax.broadcasted_iota(jnp.int32, (P, ct), 0)
    wgt = jnp.where(frow == 0, 1.0 / NFFT, 2.0 / NFFT)
    sign = jnp.where((frow & 1) == 0, 1.0, -1.0)
    FQ = HY_FQ
    h = {}
    for idx in range(HY_LAGS):
        d = idx - (NB - 1)
        if d >= 1:
            kp, kn = blk(f_asc, d), blk(f_dsc, NB - d)
        elif d == 0:
            kp, kn = blk(f_asc, 0), blk(b_asc, 0)
        else:
            kp, kn = blk(b_dsc, NB + d), blk(b_asc, -d)
        ev = kp + kn
        od = kp - kn
        evb = ev.astype(BF16)
        odb = od.astype(BF16)
        re = jnp.concatenate([_bdot(csv_ref[2 * q * FQ:(2 * q + 1) * FQ, :], evb)
                              for q in range(HY_NQ)], axis=0)
        im = jnp.concatenate([_bdot(csv_ref[(2 * q + 1) * FQ:(2 * q + 2) * FQ, :], odb)
                              for q in range(HY_NQ)], axis=0)
        h[d] = (re * wgt, jnp.where(frow == 0, 0.0, -im * wgt))
        hny_ref[idx:idx + 1, :] = jnp.sum(ev * sign, axis=0, keepdims=True) * (1.0 / NFFT)

    def sub(a, b):
        return (a[0] - b[0], a[1] - b[1])

    def coef3(a0, a_dn, a_up):
        return [a0, sub(a_dn, a0), sub(a_up, a0)]

    coefs = (coef3(h[0], h[1], h[-1])
             + coef3(sub(h[2], h[0]), sub(h[3], h[1]), sub(h[1], h[-1]))
             + coef3(sub(h[-2], h[0]), sub(h[-1], h[1]), sub(h[-3], h[-1])))
    for c, (re, im) in enumerate(coefs):
        kre_ref[c] = re
        kim_ref[c] = im


def _filter_spectra(hidden, w3, decay, csv):
    ct = HY_CT
    nct = HALF // ct
    fcol = lambda o, c: (0, o * nct + c)
    bcol = lambda o, c: (0, (2 + o) * nct + c)
    hspec = pl.BlockSpec((None, HY_COEF, HY_P, ct), lambda o, c: (o, 0, 0, c))
    return pl.pallas_call(
        _filter_kernel,
        out_shape=(jax.ShapeDtypeStruct((2, HY_COEF, HY_P, HALF), F32),
                   jax.ShapeDtypeStruct((2, HY_COEF, HY_P, HALF), F32),
                   jax.ShapeDtypeStruct((2, HY_LAGS, HALF), F32)),
        grid=(2, nct),
        in_specs=[
            pl.BlockSpec((2, SEQ, FILTER_HIDDEN), lambda o, c: (0, 0, 0)),
            pl.BlockSpec((FILTER_HIDDEN, ct), fcol),
            pl.BlockSpec((FILTER_HIDDEN, ct), bcol),
            pl.BlockSpec((1, ct), fcol),
            pl.BlockSpec((1, ct), bcol),
            pl.BlockSpec((2 * HY_P, HY_P), lambda o, c: (0, 0)),
        ],
        out_specs=(hspec, hspec,
                   pl.BlockSpec((None, HY_LAGS, ct), lambda o, c: (o, 0, c))),
        compiler_params=_params(2),
        name="hyena_filter_spectra",
    )(hidden, w3, w3, decay, decay, csv)


def _in_proj_kernel(x_ref, g_ref, w_ref, o_ref, h_ref):
    sub = IN_TILE // IN_SUBTILES
    rows = [slice(s * sub, (s + 1) * sub) for s in range(IN_SUBTILES)]
    for r in rows:
        h_ref[r, :] = _rms(x_ref[r, :], g_ref[...]).astype(BF16)
    for r in rows:
        for c0 in range(0, o_ref.shape[1], IN_CHUNK):
            o_ref[r, c0:c0 + IN_CHUNK] = _bdot(
                h_ref[r, :], w_ref[:, c0:c0 + IN_CHUNK]).astype(BF16)


def _in_proj(x2d, g, w, layer):
    m = x2d.shape[0]
    n_out = w.shape[2]
    return pl.pallas_call(
        _in_proj_kernel,
        out_shape=jax.ShapeDtypeStruct((m, n_out), BF16),
        grid=(m // IN_TILE,),
        in_specs=[pl.BlockSpec((IN_TILE, D_MODEL), lambda i: (i, 0)),
                  pl.BlockSpec((1, D_MODEL), lambda i: (0, 0)),
                  _resident((None, D_MODEL, n_out), lambda i: (layer, 0, 0))],
        out_specs=pl.BlockSpec((IN_TILE, n_out), lambda i: (i, 0)),
        scratch_shapes=[pltpu.VMEM((IN_TILE, D_MODEL), BF16)],
        compiler_params=_params(1),
        name="rms_in_proj",
    )(x2d, g, w)


def _in_proj_gmlp_kernel(x_ref, g_ref, w_ref, lng_ref, ws_ref, bs_ref, o_ref, ya_ref,
                         h_ref, za_ref, vn_ref):
    sub = IN_TILE // IN_GMLP_SUBTILES
    rows = [slice(s * sub, (s + 1) * sub) for s in range(IN_GMLP_SUBTILES)]
    for r in rows:
        h_ref[r, :] = _rms(x_ref[r, :], g_ref[...]).astype(BF16)
    n_gelu = 2 * HALF
    n_out = w_ref.shape[1]
    gelu_chunks = list(range(0, n_gelu, IN_CHUNK))
    plain_chunks = list(range(n_gelu, n_out, IN_CHUNK))
    order = [c for pair in zip(gelu_chunks, plain_chunks) for c in pair]
    order += gelu_chunks[len(plain_chunks):] + plain_chunks[len(gelu_chunks):]
    def layer_norm(s):
        for i in range(sub // PW_ROWS):
            rr = slice(s * sub + i * PW_ROWS, s * sub + (i + 1) * PW_ROWS)
            v = za_ref[rr, HALF:].astype(F32)
            xc = v - jnp.mean(v, axis=-1, keepdims=True)
            y = xc * lax.rsqrt(jnp.mean(xc * xc, axis=-1, keepdims=True) + LN_EPS) * lng_ref[...]
            vn_ref[rr, :] = y.astype(BF16)

    for s, r in enumerate(rows):
        for k, c0 in enumerate(order):
            p = _bdot(h_ref[r, :], w_ref[:, c0:c0 + IN_CHUNK])
            if c0 < n_gelu:
                za_ref[r, c0:c0 + IN_CHUNK] = jax.nn.gelu(p).astype(BF16)
            else:
                o_ref[r, c0 - n_gelu:c0 - n_gelu + IN_CHUNK] = p.astype(BF16)
            if k == 1 and s > 0:
                layer_norm(s - 1)
    layer_norm(len(rows) - 1)

    for g in range(A_GROUPS):
        cols = slice(g * GROUP_DIM, (g + 1) * GROUP_DIM)
        vg = jnp.concatenate(
            [vn_ref[n * CHUNK:(n + 1) * CHUNK, cols] for n in range(N_CHUNKS)], axis=1)
        sg = _bdot(ws_ref[g], vg) + bs_ref[g]
        for n in range(N_CHUNKS):
            rr = slice(n * CHUNK, (n + 1) * CHUNK)
            u = za_ref[rr, cols].astype(F32)
            ya_ref[rr, cols] = (u * sg[:, n * GROUP_DIM:(n + 1) * GROUP_DIM]).astype(BF16)


def _in_proj_gmlp(x2d, g, w, layer, ln_g, w_s, b_s):
    assert IN_TILE == SEQ
    m = x2d.shape[0]
    n_out = w.shape[2]
    n_hy = n_out - 2 * HALF
    return pl.pallas_call(
        _in_proj_gmlp_kernel,
        out_shape=(jax.ShapeDtypeStruct((m, n_hy), BF16), jax.ShapeDtypeStruct((m, HALF), BF16)),
        grid=(m // IN_TILE,),
        in_specs=[pl.BlockSpec((IN_TILE, D_MODEL), lambda i: (i, 0)),
                  pl.BlockSpec((1, D_MODEL), lambda i: (0, 0)),
                  _resident((None, D_MODEL, n_out), lambda i: (layer, 0, 0)),
                  pl.BlockSpec((1, HALF), lambda i: (0, 0)),
                  pl.BlockSpec((A_GROUPS, CHUNK, CHUNK), lambda i: (0, 0, 0)),
                  pl.BlockSpec((A_GROUPS, CHUNK, 1), lambda i: (0, 0, 0))],
        out_specs=(pl.BlockSpec((IN_TILE, n_hy), lambda i: (i, 0)),
                   pl.BlockSpec((IN_TILE, HALF), lambda i: (i, 0))),
        scratch_shapes=[pltpu.VMEM((IN_TILE, D_MODEL), BF16),
                        pltpu.VMEM((IN_TILE, 2 * HALF), BF16),
                        pltpu.VMEM((IN_TILE, HALF), BF16)],
        compiler_params=_params(1, FFN_VMEM_LIMIT),
        name="rms_in_proj_gmlp",
    )(x2d, g, w, ln_g, w_s, b_s)


def _cmul(z, h):
    return z[0] * h[0] + z[1] * h[1], z[1] * h[0] - z[0] * h[1]


def _cadd(a, b):
    return a[0] + b[0], a[1] + b[1]


def _mix_tile(o, q, row0, lh, zs_ref, kre_ref, kim_ref, ys_ref):
    FQ = HY_FQ
    fr = slice(q * FQ + row0, q * FQ + row0 + MIX_ROWS)
    rr = slice(row0, row0 + MIX_ROWS)
    rm = slice(FQ + row0, FQ + row0 + MIX_ROWS)
    lanes = slice(lh * MIX_LANES, (lh + 1) * MIX_LANES)

    def coef(c):
        return kre_ref[o, c, fr, lanes], kim_ref[o, c, fr, lanes]

    def toep2(c, v0, v1):
        q1 = _cmul(_cadd(v0, v1), coef(c))
        q2 = _cmul(v0, coef(c + 1))
        q3 = _cmul(v1, coef(c + 2))
        return _cadd(q1, q3), _cadd(q1, q2)

    z = [(zs_ref[j, rr, lanes], zs_ref[j, rm, lanes]) for j in range(HY_NB)]
    p1 = toep2(0, _cadd(z[0], z[2]), _cadd(z[1], z[3]))
    p2 = toep2(3, z[0], z[1])
    p3 = toep2(6, z[2], z[3])
    y = [_cadd(p1[0], p3[0]), _cadd(p1[1], p3[1]), _cadd(p1[0], p2[0]), _cadd(p1[1], p2[1])]
    for i in range(HY_NB):
        ys_ref[i, rr, lanes] = y[i][0].astype(BF16)
        ys_ref[i, rm, lanes] = y[i][1].astype(BF16)


def _conv3_rows(src_ref, w, start):
    L, R, T, H = SEQ, HY_CONV_ROWS, BF16_ROWS, HALO
    x = src_ref[start:start + R, :].astype(F32)
    zeros = jnp.zeros((H, x.shape[1]), F32)
    prev = src_ref[start - T:start, :].astype(F32)[T - H:] if start > 0 else zeros
    nxt = src_ref[start + R:start + R + T, :].astype(F32)[:H] if start + R < L else zeros
    a = jnp.concatenate([prev, x, nxt], axis=0)
    n = R + 2 * H
    y = pltpu.roll(a, 1, axis=0) * w[0:1, :] + a * w[1:2, :] + pltpu.roll(a, n - 1, axis=0) * w[2:3, :]
    return y[H:H + R]


def _hyena_kernel(v_ref, g1_ref, g2_ref, wv_ref, wg1_ref, wg2_ref, skip_ref,
                  kre_ref, kim_ref, hny_ref, csv_ref, csh_ref, o_ref,
                  z_ref, gate_ref, zcat_ref, zs0_ref, zs1_ref, ys0_ref, ys1_ref, acc_ref):
    P, NB, FQ, R = HY_P, HY_NB, HY_FQ, HY_CONV_ROWS
    ct = o_ref.shape[1]
    steps = FQ // MIX_ROWS // NB

    def chunk(q):
        return slice(2 * q * FQ, 2 * (q + 1) * FQ)

    def mix_share(o, q, k, zs_ref, ys_ref):
        for s in range(steps):
            for lh in range(ct // MIX_LANES):
                _mix_tile(o, q, (k * steps + s) * MIX_ROWS, lh, zs_ref, kre_ref, kim_ref, ys_ref)

    for j in range(NB):
        for r in range(P // R):
            zr = _conv3_rows(v_ref, wv_ref[...], j * P + r * R)
            z_ref[j * P + r * R:j * P + (r + 1) * R, :] = zr
            zcat_ref[j, r * R:(r + 1) * R, :] = zr.astype(BF16)

    def always(k):
        return pl.program_id(1) + k >= 0

    def phase_a(o, g_ref, wg_ref):
        for j in range(NB):
            zs0_ref[j] = _bdot(csv_ref[chunk(0), :], zcat_ref[j])
            for r in range(P // R):
                start = j * P + r * R
                gate_ref[start:start + R, :] = _conv3_rows(g_ref, wg_ref[...], start)

    def phase_b(o):
        for j in range(NB):
            zs1_ref[j] = _bdot(csv_ref[chunk(1), :], zcat_ref[j])
            mix_share(o, 0, j, zs0_ref, ys0_ref)
        tile = slice(FQ, FQ + MIX_ROWS)
        first = lax.broadcasted_iota(jnp.int32, (MIX_ROWS, ct), 0) == 0
        z_ny = [zs0_ref[j, tile, :] for j in range(NB)]
        for i in range(NB):
            y_ny = sum(z_ny[j] * hny_ref[o, i - j + NB - 1:i - j + NB, :] for j in range(NB))
            ys0_ref[i, tile, :] = jnp.where(
                first, y_ny, ys0_ref[i, tile, :].astype(F32)).astype(BF16)

    def phase_c(o):
        for i in range(NB):
            acc_ref[i] = _bdot(csh_ref[0], ys0_ref[i])
            mix_share(o, 1, i, zs1_ref, ys1_ref)

    def phase_d(o):
        for i in range(NB):
            rows = slice(i * P, (i + 1) * P)
            conv = acc_ref[i] + _bdot(csh_ref[1], ys1_ref[i])
            z_new = gate_ref[rows, :] * (conv + z_ref[rows, :] * skip_ref[o:o + 1, :])
            if o == 0:
                z_ref[rows, :] = z_new
                zcat_ref[i] = z_new.astype(BF16)
            else:
                o_ref[rows, :] = z_new.astype(BF16)

    for o, (g_ref, wg_ref) in enumerate(((g1_ref, wg1_ref), (g2_ref, wg2_ref))):
        phases = (functools.partial(phase_a, o, g_ref, wg_ref), functools.partial(phase_b, o),
                  functools.partial(phase_c, o), functools.partial(phase_d, o))
        for k, phase in enumerate(phases):
            pl.when(always(len(phases) * o + k))(phase)


def _hyena(proj3d, conv_w, skip, kre, kim, hny, csv, csh):
    bsz = proj3d.shape[0]
    ct = HY_CT
    nct = HALF // ct
    assert proj3d.shape[2] == 3 * HALF

    def slab(part):
        return pl.BlockSpec((None, SEQ, ct), lambda c, b: (b, 0, part * nct + c))

    def cw(part):
        return pl.BlockSpec((3, ct), lambda c, b: (0, part * nct + c))

    assert HY_NB == 4 and HY_NQ == 2 and HY_FQ % (MIX_ROWS * HY_NB) == 0 and ct % MIX_LANES == 0
    hspec = _resident((2, HY_COEF, HY_P, ct), lambda c, b: (0, 0, 0, c))
    const = lambda c, b: (0, 0)
    return pl.pallas_call(
        _hyena_kernel,
        out_shape=jax.ShapeDtypeStruct((bsz, SEQ, HALF), BF16),
        grid=(nct, bsz),
        in_specs=[slab(0), slab(1), slab(2), cw(0), cw(1), cw(2),
                  pl.BlockSpec((2, ct), lambda c, b: (0, c)),
                  hspec, hspec,
                  pl.BlockSpec((2, HY_LAGS, ct), lambda c, b: (0, 0, c)),
                  pl.BlockSpec((2 * HY_P, HY_P), const), pl.BlockSpec((HY_NQ, HY_P, 2 * HY_FQ), lambda c, b: (0, 0, 0))],
        out_specs=pl.BlockSpec((None, SEQ, ct), lambda c, b: (b, 0, c)),
        scratch_shapes=[pltpu.VMEM((SEQ, ct), F32), pltpu.VMEM((SEQ, ct), F32),
                        pltpu.VMEM((HY_NB, HY_P, ct), BF16),
                        pltpu.VMEM((HY_NB, 2 * HY_FQ, ct), F32),
                        pltpu.VMEM((HY_NB, 2 * HY_FQ, ct), F32),
                        pltpu.VMEM((HY_NB, 2 * HY_FQ, ct), BF16),
                        pltpu.VMEM((HY_NB, 2 * HY_FQ, ct), BF16),
                        pltpu.VMEM((HY_NB, HY_P, ct), F32)],
        compiler_params=_params(2),
        name="hyena_long_conv",
    )(proj3d, proj3d, proj3d, conv_w, conv_w, conv_w, skip, kre, kim, hny, csv, csh)


def _window_sum(a, w):
    n = a.shape[0]
    right = w - w // 2 - 1
    s = a
    k = 1
    while k < w:
        s = s + pltpu.roll(s, k, axis=0)
        k *= 2
    return pltpu.roll(s, n - right, axis=0) if right else s


def _pool_conv_kernel(p_ref, b_ref, c_ref, h_ref, cw_ref, scale_ref, dw_ref, yc_ref, yd_ref,
                      chs_ref, pooled_ref, band_ref):
    L, R, n = SEQ, PW_ROWS, PW_ROWS + 2 * HALO
    offs = (0, (POOL_K - POOL_TILE) // 2, POOL_K - POOL_TILE)

    @pl.when(pl.program_id(0) == 0)
    def _():
        r = lax.broadcasted_iota(jnp.int32, (POOL_TILE, POOL_K), 0)
        c = lax.broadcasted_iota(jnp.int32, (POOL_TILE, POOL_K), 1)
        for g, w in enumerate(POOL_WINDOWS):
            left = w // 2
            for kind, off in enumerate(offs):
                d = c - off - r + left
                m = jnp.where((d >= 0) & (d < w), 1.0 / w, 0.0) - jnp.where(d == left, 1.0, 0.0)
                band_ref[g, kind] = m.astype(BF16)

    zeros = jnp.zeros((HALO, HALF), F32)
    chs_ref[0:HALO, :] = zeros
    chs_ref[L + HALO:L + 2 * HALO, :] = zeros

    for i in range(L // R):
        r = slice(i * R, (i + 1) * R)
        chs_ref[HALO + i * R:HALO + (i + 1) * R, :] = (
            c_ref[r, :].astype(F32) * h_ref[r, :].astype(F32))

    n_tiles = L // POOL_TILE
    for g, w in enumerate(POOL_WINDOWS):
        cols = slice(g * GROUP_DIM, (g + 1) * GROUP_DIM)
        left = w // 2
        right = w - left - 1

        for k in range(n_tiles):
            kind = 0 if k == 0 else (2 if k == n_tiles - 1 else 1)
            t0 = k * POOL_TILE
            ws = t0 - offs[kind]
            pooled_ref[g, t0:t0 + POOL_TILE, :] = _bdot(
                band_ref[g, kind], p_ref[ws:ws + POOL_K, cols]).astype(BF16)

        for top in (True, False):
            pad = jnp.zeros((POOL_EDGE, GROUP_DIM), F32)
            if top:
                a = jnp.concatenate([pad, p_ref[0:2 * POOL_EDGE, cols].astype(F32)], axis=0)
                first = 0
            else:
                a = jnp.concatenate([p_ref[L - 2 * POOL_EDGE:L, cols].astype(F32), pad], axis=0)
                first = L - POOL_EDGE
            keep = slice(POOL_EDGE, 2 * POOL_EDGE)
            t = first + lax.broadcasted_iota(jnp.int32, (POOL_EDGE, GROUP_DIM), 0)
            cnt = (jnp.minimum(t + right, L - 1) - jnp.maximum(t - left, 0) + 1).astype(F32)
            pooled = _window_sum(a, w)[keep] / cnt - a[keep]
            pooled_ref[g, first:first + POOL_EDGE, :] = pooled.astype(BF16)

        dw = dw_ref[:, cols]
        for i in range(L // R):
            r0 = i * R
            e = chs_ref[r0:r0 + n, cols]
            conv = (pltpu.roll(e, 1, axis=0) * dw[0:1, :] + e * dw[1:2, :]
                    + pltpu.roll(e, n - 1, axis=0) * dw[2:3, :])
            yd = b_ref[r0:r0 + R, cols].astype(F32) * conv[HALO:HALO + R]
            yd_ref[r0:r0 + R, cols] = yd.astype(BF16)

        y = _bdot(pooled_ref[g], cw_ref[g]) * scale_ref[:, cols]
        yc_ref[:, cols] = y.astype(BF16)


def _pool_conv(proj3d, c_w, c_scale, d_conv_w):
    bsz = proj3d.shape[0]

    def part(k):
        return pl.BlockSpec((None, SEQ, HALF), lambda b: (b, 0, k))

    out = pl.BlockSpec((None, SEQ, HALF), lambda b: (b, 0, 0))
    return pl.pallas_call(
        _pool_conv_kernel,
        out_shape=(jax.ShapeDtypeStruct((bsz, SEQ, HALF), BF16),) * 2,
        grid=(bsz,),
        in_specs=[part(0), part(1), part(2), part(3),
                  pl.BlockSpec((len(POOL_WINDOWS), GROUP_DIM, GROUP_DIM), lambda b: (0, 0, 0)),
                  pl.BlockSpec((1, HALF), lambda b: (0, 0)),
                  pl.BlockSpec((3, HALF), lambda b: (0, 0))],
        out_specs=(out, out),
        scratch_shapes=[pltpu.VMEM((SEQ + 2 * HALO, HALF), F32),
                        pltpu.VMEM((len(POOL_WINDOWS), SEQ, GROUP_DIM), BF16),
                        pltpu.VMEM((len(POOL_WINDOWS), 3, POOL_TILE, POOL_K), BF16)],
        compiler_params=_params(1),
        name="pool_short_conv",
    )(proj3d, proj3d, proj3d, proj3d, c_w, c_scale, d_conv_w)


def _out_ffn_kernel(x_ref, y0_ref, y1_ref, wo_ref, g_ref, wgu_ref, wd_ref, o_ref, h_ref, a_ref):
    sub = ROW_TILE // FFN_SUBTILES
    rows = [slice(s * sub, (s + 1) * sub) for s in range(FFN_SUBTILES)]
    for r in rows:
        mix = _bdot(y0_ref[r, :], wo_ref[:HALF, :]) + _bdot(y1_ref[r, :], wo_ref[HALF:, :])
        o_ref[r, :] = x_ref[r, :] + _rms(mix, g_ref[1:2, :])
        h_ref[r, :] = _rms(o_ref[r, :], g_ref[2:3, :]).astype(BF16)
    for r in rows:
        for k in range(D_FF // FF_CHUNK):
            gate = _bdot(h_ref[r, :], wgu_ref[:, k * FF_CHUNK:(k + 1) * FF_CHUNK])
            up = _bdot(h_ref[r, :], wgu_ref[:, D_FF + k * FF_CHUNK:D_FF + (k + 1) * FF_CHUNK])
            a_ref[r, k * FF_CHUNK:(k + 1) * FF_CHUNK] = (
                gate * jax.nn.sigmoid(gate) * up).astype(BF16)
    for r in rows:
        f = _bdot(a_ref[r, :], wd_ref[...])
        o_ref[r, :] = o_ref[r, :] + _rms(f, g_ref[3:4, :])


def _out_ffn(x2d, y0, y1, w_out, mix_layer, g, w_gu, w_down, layer):
    m = x2d.shape[0]
    tile = lambda n: pl.BlockSpec((ROW_TILE, n), lambda i: (i, 0))
    return pl.pallas_call(
        _out_ffn_kernel,
        out_shape=jax.ShapeDtypeStruct((m, D_MODEL), F32),
        grid=(m // ROW_TILE,),
        in_specs=[tile(D_MODEL), tile(HALF), tile(HALF),
                  _resident((None, D_MODEL, D_MODEL), lambda i: (mix_layer, 0, 0)),
                  pl.BlockSpec((None, 4, D_MODEL), lambda i: (layer, 0, 0)),
                  _resident((None, D_MODEL, 2 * D_FF), lambda i: (layer, 0, 0)),
                  _resident((None, D_FF, D_MODEL), lambda i: (layer, 0, 0))],
        out_specs=tile(D_MODEL),
        scratch_shapes=[pltpu.VMEM((ROW_TILE, D_MODEL), BF16),
                        pltpu.VMEM((ROW_TILE, D_FF), BF16)],
        compiler_params=_params(1, FFN_VMEM_LIMIT),
        name="out_proj_ffn",
    )(x2d, y0, y1, w_out, g, w_gu, w_down)


def kernel(x, norm_g, ffn_w_gu, ffn_w_down, ab_w_in, ab_w_out, a_ln_g, a_w_s, a_b_s, b_conv_w, b_filt_w1, b_filt_b1, b_filt_freq, b_filt_w2, b_filt_b2, b_filt_w3, b_decay, b_skip, cd_w_in, cd_w_out, c_w, c_scale, d_conv_w):
    bsz, seq, d = x.shape
    assert (seq, d) == (SEQ, D_MODEL) and norm_g.shape[0] == DEPTH
    m = bsz * seq
    bands = jnp.linspace(1e-4, FILTER_BANDS - 1, FILTER_BANDS, dtype=F32)[None, :]
    csv, csh = _dft_tables()
    x2d = x.reshape(m, d)
    ab_w_in, ab_w_out, cd_w_in, cd_w_out, ffn_w_gu, ffn_w_down = (
        w.astype(BF16) for w in (ab_w_in, ab_w_out, cd_w_in, cd_w_out, ffn_w_gu, ffn_w_down))
    for i in range(DEPTH):
        j = i // 2
        g = norm_g[i]
        if i % 2 == 0:
            proj, y0 = _in_proj_gmlp(x2d, g[0:1], ab_w_in, j, a_ln_g[j][None, :],
                                     a_w_s[j].astype(BF16), a_b_s[j][:, :, None])
            proj = proj.reshape(bsz, seq, -1)
            hidden = _filter_hidden(bands, b_filt_w1[j], b_filt_b1[j][None, :], b_filt_freq[j],
                                    b_filt_w2[j], b_filt_b2[j][None, :])
            kre, kim, hny = _filter_spectra(hidden, b_filt_w3[j], b_decay[j][None, :], csv)
            y1 = _hyena(proj, b_conv_w[j], b_skip[j], kre, kim, hny, csv, csh)
            w_out = ab_w_out
        else:
            proj = _in_proj(x2d, g[0:1], cd_w_in, j).reshape(bsz, seq, -1)
            y0, y1 = _pool_conv(proj, c_w[j].astype(BF16), c_scale[j][None, :], d_conv_w[j])
            w_out = cd_w_out
        x2d = _out_ffn(x2d, y0.reshape(m, HALF), y1.reshape(m, HALF), w_out, j, norm_g,
                       ffn_w_gu, ffn_w_down, i)
    return x2d.reshape(bsz, seq, d)
```

```python
import functools
import math

import jax
import jax.numpy as jnp
from jax import lax
from jax.experimental import pallas as pl
from jax.experimental.pallas import tpu as pltpu

F32 = jnp.float32
BF16 = jnp.bfloat16
LANES = 128
BF16_ROWS = 16

D_MODEL = 1024
SEQ = 2048
DEPTH = 4
HALF = D_MODEL // 2
CHUNK = 128
N_CHUNKS = SEQ // CHUNK
A_GROUPS = 4
GROUP_DIM = HALF // A_GROUPS
FILTER_BANDS = 16
FILTER_HIDDEN = 64
POOL_WINDOWS = (2, 4, 8, 16)
D_FF = 2816
RMS_EPS = 1e-6
LN_EPS = 1e-5
HY_P = 512
HY_NB = SEQ // HY_P
HY_LAGS = 2 * HY_NB - 1
NFFT = 2 * HY_P
HY_NQ = 2
HY_FQ = HY_P // HY_NQ
HY_COEF = 9

ROW_TILE = 1024
FFN_SUBTILES = 4
FFN_VMEM_LIMIT = 60 * 1024 * 1024
IN_TILE = 2048
IN_SUBTILES = 4
IN_GMLP_SUBTILES = 8
IN_CHUNK = 256
FF_CHUNK = 256
HY_CT = 256
HY_CONV_ROWS = 128
MIX_ROWS = 16
MIX_LANES = 128
PW_ROWS = 128
HALO = 8
POOL_TILE = 256
POOL_K = 512
POOL_EDGE = 16
VMEM_LIMIT = 56 * 1024 * 1024


def _params(n_axes, vmem=VMEM_LIMIT):
    return pltpu.CompilerParams(
        dimension_semantics=("arbitrary",) * n_axes, vmem_limit_bytes=vmem)


def _resident(shape, index_map):
    return pl.BlockSpec(shape, index_map, pipeline_mode=pl.Buffered(1))


def _rms(x, g):
    return x * lax.rsqrt(jnp.mean(x * x, axis=-1, keepdims=True) + RMS_EPS) * g


def _bdot(a, b):
    return jnp.dot(a, b, preferred_element_type=F32)


def _fdot(a, b):
    return jnp.dot(a, b, preferred_element_type=F32, precision=lax.Precision.HIGHEST)


def _shift_rows(x, k, row):
    n = x.shape[0]
    y = pltpu.roll(x, k % n, axis=0)
    if k > 0:
        return jnp.where(row >= k, y, 0.0)
    return jnp.where(row < n + k, y, 0.0)


def _conv3(x, w, row):
    return (_shift_rows(x, 1, row) * w[0:1, :] + x * w[1:2, :]
            + _shift_rows(x, -1, row) * w[2:3, :])


def _dft_kernel(csv_ref, csh_ref):
    P, FQ = HY_P, HY_FQ
    f = lax.broadcasted_iota(jnp.int32, (P, LANES), 0)
    lane = lax.broadcasted_iota(jnp.int32, (P, LANES), 1)
    scale = 2.0 * math.pi / NFFT
    ang_b = ((f * lane) & (NFFT - 1)).astype(F32) * scale
    ang_a = ((f * (lane * LANES)) & (NFFT - 1)).astype(F32) * scale
    cb, sb = jnp.cos(ang_b), jnp.sin(ang_b)
    ca, sa = jnp.cos(ang_a), jnp.sin(ang_a)
    for a in range(P // LANES):
        cols = slice(a * LANES, (a + 1) * LANES)
        ca_a = ca[:, a:a + 1]
        sa_a = sa[:, a:a + 1]
        c = (ca_a * cb - sa_a * sb).astype(BF16)
        s = sa_a * cb + ca_a * sb
        s_v = jnp.where(f == 0, jnp.where((lane & 1) == 0, 1.0, -1.0), s).astype(BF16)
        for q in range(HY_NQ):
            csv_ref[2 * q * FQ:(2 * q + 1) * FQ, cols] = c[q * FQ:(q + 1) * FQ]
            csv_ref[(2 * q + 1) * FQ:(2 * q + 2) * FQ, cols] = s_v[q * FQ:(q + 1) * FQ]
        s_h = s
        if a == 0:
            s_h = jnp.where(lane == 0, jnp.where((f & 1) == 0, 1.0, -1.0), s)
        q, off = divmod(a * LANES, FQ)
        csh_ref[q, :, off:off + LANES] = c
        csh_ref[q, :, FQ + off:FQ + off + LANES] = s_h.astype(BF16)


def _dft_tables():
    return pl.pallas_call(
        _dft_kernel,
        out_shape=(jax.ShapeDtypeStruct((2 * HY_P, HY_P), BF16),
                   jax.ShapeDtypeStruct((HY_NQ, HY_P, 2 * HY_FQ), BF16)),
        compiler_params=_params(0),
        name="dft_tables",
    )()


def _filter_hidden_kernel(bands_ref, w1_ref, b1_ref, freq_ref, w2_ref, b2_ref, h_ref):
    L = SEQ
    r = lax.broadcasted_iota(jnp.int32, (L, 1), 0)
    t = jnp.where(pl.program_id(0) == 0, r, L - 1 - r).astype(F32)
    t01 = t / (L - 1)
    fw = (2 * math.pi * t / L) * bands_ref[...]
    pre = (t01 * w1_ref[0:1, :] + _fdot(jnp.cos(fw), w1_ref[1:1 + FILTER_BANDS, :])
           + _fdot(-jnp.sin(fw), w1_ref[1 + FILTER_BANDS:, :]) + b1_ref[...])
    h = jnp.sin(freq_ref[0:1, :] * pre)
    h_ref[...] = jnp.sin(freq_ref[1:2, :] * (_fdot(h, w2_ref[...]) + b2_ref[...]))


def _filter_hidden(bands, w1, b1, freq, w2, b2):
    const = lambda s: (0, 0)
    return pl.pallas_call(
        _filter_hidden_kernel,
        out_shape=jax.ShapeDtypeStruct((2, SEQ, FILTER_HIDDEN), F32),
        grid=(2,),
        in_specs=[pl.BlockSpec((1, FILTER_BANDS), const),
                  pl.BlockSpec((1 + 2 * FILTER_BANDS, FILTER_HIDDEN), const),
                  pl.BlockSpec((1, FILTER_HIDDEN), const),
                  pl.BlockSpec((2, FILTER_HIDDEN), const),
                  pl.BlockSpec((FILTER_HIDDEN, FILTER_HIDDEN), const),
                  pl.BlockSpec((1, FILTER_HIDDEN), const)],
        out_specs=pl.BlockSpec((None, SEQ, FILTER_HIDDEN), lambda s: (s, 0, 0)),
        compiler_params=_params(1),
        name="hyena_filter_hidden",
    )(bands, w1, b1, freq, w2, b2)


def _filter_kernel(hid_ref, w3f_ref, w3b_ref, decf_ref, decb_ref, csv_ref,
                   kre_ref, kim_ref, hny_ref):
    L, P, NB = SEQ, HY_P, HY_NB
    ct = kre_ref.shape[-1]
    row = lax.broadcasted_iota(jnp.int32, (L, ct), 0)
    t_asc = row.astype(F32) / (L - 1)
    t_dsc = (L - 1 - row).astype(F32) / (L - 1)
    dec_f = jnp.abs(decf_ref[...])
    dec_b = jnp.abs(decb_ref[...])
    h_asc = hid_ref[0]
    h_dsc = hid_ref[1]
    inner = (row & (P - 1)) != 0
    f_asc = _fdot(h_asc, w3f_ref[...]) * jnp.exp(-t_asc * dec_f)
    b_dsc = _fdot(h_dsc, w3b_ref[...]) * jnp.exp(-t_dsc * dec_b)
    f_dsc = _fdot(h_dsc, w3f_ref[...]) * jnp.exp(-t_dsc * dec_f)
    f_dsc = jnp.where(inner, _shift_rows(f_dsc, 1, row), 0.0)
    b_asc = _fdot(h_asc, w3b_ref[...]) * jnp.exp(-t_asc * dec_b)
    b_asc = jnp.where(inner, _shift_rows(b_asc, 1, row), 0.0)

    def blk(x, i):
        return x[i * P:(i + 1) * P, :]

    frow = lax.broadcasted_iota(jnp.int32, (P, ct), 0)
    wgt = jnp.where(frow == 0, 1.0 / NFFT, 2.0 / NFFT)
    sign = jnp.where((frow & 1) == 0, 1.0, -1.0)
    FQ = HY_FQ
    h = {}
    for idx in range(HY_LAGS):
        d = idx - (NB - 1)
        if d >= 1:
            kp, kn = blk(f_asc, d), blk(f_dsc, NB - d)
        elif d == 0:
            kp, kn = blk(f_asc, 0), blk(b_asc, 0)
        else:
            kp, kn = blk(b_dsc, NB + d), blk(b_asc, -d)
        ev = kp + kn
        od = kp - kn
        evb = ev.astype(BF16)
        odb = od.astype(BF16)
        re = jnp.concatenate([_bdot(csv_ref[2 * q * FQ:(2 * q + 1) * FQ, :], evb)
                              for q in range(HY_NQ)], axis=0)
        im = jnp.concatenate([_bdot(csv_ref[(2 * q + 1) * FQ:(2 * q + 2) * FQ, :], odb)
                              for q in range(HY_NQ)], axis=0)
        h[d] = (re * wgt, jnp.where(frow == 0, 0.0, -im * wgt))
        hny_ref[idx:idx + 1, :] = jnp.sum(ev * sign, axis=0, keepdims=True) * (1.0 / NFFT)

    def sub(a, b):
        return (a[0] - b[0], a[1] - b[1])

    def coef3(a0, a_dn, a_up):
        return [a0, sub(a_dn, a0), sub(a_up, a0)]

    coefs = (coef3(h[0], h[1], h[-1])
             + coef3(sub(h[2], h[0]), sub(h[3], h[1]), sub(h[1], h[-1]))
             + coef3(sub(h[-2], h[0]), sub(h[-1], h[1]), sub(h[-3], h[-1])))
    for c, (re, im) in enumerate(coefs):
        kre_ref[c] = re
        kim_ref[c] = im


def _filter_spectra(hidden, w3, decay, csv):
    ct = HY_CT
    nct = HALF // ct
    fcol = lambda o, c: (0, o * nct + c)
    bcol = lambda o, c: (0, (2 + o) * nct + c)
    hspec = pl.BlockSpec((None, HY_COEF, HY_P, ct), lambda o, c: (o, 0, 0, c))
    return pl.pallas_call(
        _filter_kernel,
        out_shape=(jax.ShapeDtypeStruct((2, HY_COEF, HY_P, HALF), F32),
                   jax.ShapeDtypeStruct((2, HY_COEF, HY_P, HALF), F32),
                   jax.ShapeDtypeStruct((2, HY_LAGS, HALF), F32)),
        grid=(2, nct),
        in_specs=[
            pl.BlockSpec((2, SEQ, FILTER_HIDDEN), lambda o, c: (0, 0, 0)),
            pl.BlockSpec((FILTER_HIDDEN, ct), fcol),
            pl.BlockSpec((FILTER_HIDDEN, ct), bcol),
            pl.BlockSpec((1, ct), fcol),
            pl.BlockSpec((1, ct), bcol),
            pl.BlockSpec((2 * HY_P, HY_P), lambda o, c: (0, 0)),
        ],
        out_specs=(hspec, hspec,
                   pl.BlockSpec((None, HY_LAGS, ct), lambda o, c: (o, 0, c))),
        compiler_params=_params(2),
        name="hyena_filter_spectra",
    )(hidden, w3, w3, decay, decay, csv)


def _in_proj_kernel(x_ref, g_ref, w_ref, o_ref, h_ref):
    sub = IN_TILE // IN_SUBTILES
    rows = [slice(s * sub, (s + 1) * sub) for s in range(IN_SUBTILES)]
    for r in rows:
        h_ref[r, :] = _rms(x_ref[r, :], g_ref[...]).astype(BF16)
    for r in rows:
        for c0 in range(0, o_ref.shape[1], IN_CHUNK):
            o_ref[r, c0:c0 + IN_CHUNK] = _bdot(
                h_ref[r, :], w_ref[:, c0:c0 + IN_CHUNK]).astype(BF16)


def _in_proj(x2d, g, w, layer):
    m = x2d.shape[0]
    n_out = w.shape[2]
    return pl.pallas_call(
        _in_proj_kernel,
        out_shape=jax.ShapeDtypeStruct((m, n_out), BF16),
        grid=(m // IN_TILE,),
        in_specs=[pl.BlockSpec((IN_TILE, D_MODEL), lambda i: (i, 0)),
                  pl.BlockSpec((1, D_MODEL), lambda i: (0, 0)),
                  _resident((None, D_MODEL, n_out), lambda i: (layer, 0, 0))],
        out_specs=pl.BlockSpec((IN_TILE, n_out), lambda i: (i, 0)),
        scratch_shapes=[pltpu.VMEM((IN_TILE, D_MODEL), BF16)],
        compiler_params=_params(1),
        name="rms_in_proj",
    )(x2d, g, w)


def _in_proj_gmlp_kernel(x_ref, g_ref, w_ref, lng_ref, ws_ref, bs_ref, o_ref, ya_ref,
                         h_ref, za_ref, vn_ref):
    sub = IN_TILE // IN_GMLP_SUBTILES
    rows = [slice(s * sub, (s + 1) * sub) for s in range(IN_GMLP_SUBTILES)]
    for r in rows:
        h_ref[r, :] = _rms(x_ref[r, :], g_ref[...]).astype(BF16)
    n_gelu = 2 * HALF
    n_out = w_ref.shape[1]
    gelu_chunks = list(range(0, n_gelu, IN_CHUNK))
    plain_chunks = list(range(n_gelu, n_out, IN_CHUNK))
    order = [c for pair in zip(gelu_chunks, plain_chunks) for c in pair]
    order += gelu_chunks[len(plain_chunks):] + plain_chunks[len(gelu_chunks):]
    chunks_per_sub = sub // CHUNK

    def gate(s):
        for i in range(chunks_per_sub):
            rr = slice(s * sub + i * CHUNK, s * sub + (i + 1) * CHUNK)
            v = za_ref[rr, HALF:].astype(F32)
            xc = v - jnp.mean(v, axis=-1, keepdims=True)
            y = xc * lax.rsqrt(jnp.mean(xc * xc, axis=-1, keepdims=True) + LN_EPS) * lng_ref[...]
            vn_ref[rr, :] = y.astype(BF16)
        for g in range(A_GROUPS):
            cols = slice(g * GROUP_DIM, (g + 1) * GROUP_DIM)
            chunk_rows = [slice(s * sub + i * CHUNK, s * sub + (i + 1) * CHUNK)
                          for i in range(chunks_per_sub)]
            vg = jnp.concatenate([vn_ref[rr, cols] for rr in chunk_rows], axis=1)
            sg = _bdot(ws_ref[g], vg) + bs_ref[g]
            for i, rr in enumerate(chunk_rows):
                u = za_ref[rr, cols].astype(F32)
                ya_ref[rr, cols] = (u * sg[:, i * GROUP_DIM:(i + 1) * GROUP_DIM]).astype(BF16)

    for s, r in enumerate(rows):
        for k, c0 in enumerate(order):
            p = _bdot(h_ref[r, :], w_ref[:, c0:c0 + IN_CHUNK])
            if c0 < n_gelu:
                za_ref[r, c0:c0 + IN_CHUNK] = jax.nn.gelu(p).astype(BF16)
            else:
                o_ref[r, c0 - n_gelu:c0 - n_gelu + IN_CHUNK] = p.astype(BF16)
            if k == 1 and s > 0:
                gate(s - 1)
    gate(len(rows) - 1)


def _in_proj_gmlp(x2d, g, w, layer, ln_g, w_s, b_s):
    assert IN_TILE == SEQ
    m = x2d.shape[0]
    n_out = w.shape[2]
    n_hy = n_out - 2 * HALF
    return pl.pallas_call(
        _in_proj_gmlp_kernel,
        out_shape=(jax.ShapeDtypeStruct((m, n_hy), BF16), jax.ShapeDtypeStruct((m, HALF), BF16)),
        grid=(m // IN_TILE,),
        in_specs=[pl.BlockSpec((IN_TILE, D_MODEL), lambda i: (i, 0)),
                  pl.BlockSpec((1, D_MODEL), lambda i: (0, 0)),
                  _resident((None, D_MODEL, n_out), lambda i: (layer, 0, 0)),
                  pl.BlockSpec((1, HALF), lambda i: (0, 0)),
                  pl.BlockSpec((A_GROUPS, CHUNK, CHUNK), lambda i: (0, 0, 0)),
                  pl.BlockSpec((A_GROUPS, CHUNK, 1), lambda i: (0, 0, 0))],
        out_specs=(pl.BlockSpec((IN_TILE, n_hy), lambda i: (i, 0)),
                   pl.BlockSpec((IN_TILE, HALF), lambda i: (i, 0))),
        scratch_shapes=[pltpu.VMEM((IN_TILE, D_MODEL), BF16),
                        pltpu.VMEM((IN_TILE, 2 * HALF), BF16),
                        pltpu.VMEM((IN_TILE, HALF), BF16)],
        compiler_params=_params(1, FFN_VMEM_LIMIT),
        name="rms_in_proj_gmlp",
    )(x2d, g, w, ln_g, w_s, b_s)


def _cmul(z, h):
    return z[0] * h[0] + z[1] * h[1], z[1] * h[0] - z[0] * h[1]


def _cadd(a, b):
    return a[0] + b[0], a[1] + b[1]


def _mix_tile(o, q, row0, lh, zs_ref, kre_ref, kim_ref, ys_ref):
    FQ = HY_FQ
    fr = slice(q * FQ + row0, q * FQ + row0 + MIX_ROWS)
    rr = slice(row0, row0 + MIX_ROWS)
    rm = slice(FQ + row0, FQ + row0 + MIX_ROWS)
    lanes = slice(lh * MIX_LANES, (lh + 1) * MIX_LANES)

    def coef(c):
        return kre_ref[o, c, fr, lanes], kim_ref[o, c, fr, lanes]

    def toep2(c, v0, v1):
        q1 = _cmul(_cadd(v0, v1), coef(c))
        q2 = _cmul(v0, coef(c + 1))
        q3 = _cmul(v1, coef(c + 2))
        return _cadd(q1, q3), _cadd(q1, q2)

    z = [(zs_ref[j, rr, lanes], zs_ref[j, rm, lanes]) for j in range(HY_NB)]
    p1 = toep2(0, _cadd(z[0], z[2]), _cadd(z[1], z[3]))
    p2 = toep2(3, z[0], z[1])
    p3 = toep2(6, z[2], z[3])
    y = [_cadd(p1[0], p3[0]), _cadd(p1[1], p3[1]), _cadd(p1[0], p2[0]), _cadd(p1[1], p2[1])]
    for i in range(HY_NB):
        ys_ref[i, rr, lanes] = y[i][0].astype(BF16)
        ys_ref[i, rm, lanes] = y[i][1].astype(BF16)


def _conv3_rows(src_ref, w, start):
    L, R, T, H = SEQ, HY_CONV_ROWS, BF16_ROWS, HALO
    x = src_ref[start:start + R, :].astype(F32)
    zeros = jnp.zeros((H, x.shape[1]), F32)
    prev = src_ref[start - T:start, :].astype(F32)[T - H:] if start > 0 else zeros
    nxt = src_ref[start + R:start + R + T, :].astype(F32)[:H] if start + R < L else zeros
    a = jnp.concatenate([prev, x, nxt], axis=0)
    n = R + 2 * H
    y = pltpu.roll(a, 1, axis=0) * w[0:1, :] + a * w[1:2, :] + pltpu.roll(a, n - 1, axis=0) * w[2:3, :]
    return y[H:H + R]


def _hyena_kernel(v_ref, g1_ref, g2_ref, wv_ref, wg1_ref, wg2_ref, skip_ref,
                  kre_ref, kim_ref, hny_ref, csv_ref, csh_ref, o_ref,
                  z_ref, gate_ref, zcat_ref, zs0_ref, zs1_ref, ys0_ref, ys1_ref, acc_ref):
    P, NB, FQ, R = HY_P, HY_NB, HY_FQ, HY_CONV_ROWS
    ct = o_ref.shape[1]
    steps = FQ // MIX_ROWS // NB

    def chunk(q):
        return slice(2 * q * FQ, 2 * (q + 1) * FQ)

    def mix_share(o, q, k, zs_ref, ys_ref):
        for s in range(steps):
            for lh in range(ct // MIX_LANES):
                _mix_tile(o, q, (k * steps + s) * MIX_ROWS, lh, zs_ref, kre_ref, kim_ref, ys_ref)

    for j in range(NB):
        for r in range(P // R):
            zr = _conv3_rows(v_ref, wv_ref[...], j * P + r * R)
            z_ref[j * P + r * R:j * P + (r + 1) * R, :] = zr
            zcat_ref[j, r * R:(r + 1) * R, :] = zr.astype(BF16)

    def always(k):
        return pl.program_id(1) + k >= 0

    def phase_a(o, g_ref, wg_ref):
        for j in range(NB):
            zs0_ref[j] = _bdot(csv_ref[chunk(0), :], zcat_ref[j])
            for r in range(P // R):
                start = j * P + r * R
                gate_ref[start:start + R, :] = _conv3_rows(g_ref, wg_ref[...], start)

    def phase_b(o):
        for j in range(NB):
            zs1_ref[j] = _bdot(csv_ref[chunk(1), :], zcat_ref[j])
            mix_share(o, 0, j, zs0_ref, ys0_ref)
        tile = slice(FQ, FQ + MIX_ROWS)
        first = lax.broadcasted_iota(jnp.int32, (MIX_ROWS, ct), 0) == 0
        z_ny = [zs0_ref[j, tile, :] for j in range(NB)]
        for i in range(NB):
            y_ny = sum(z_ny[j] * hny_ref[o, i - j + NB - 1:i - j + NB, :] for j in range(NB))
            ys0_ref[i, tile, :] = jnp.where(
                first, y_ny, ys0_ref[i, tile, :].astype(F32)).astype(BF16)

    def phase_c(o):
        for i in range(NB):
            acc_ref[i] = _bdot(csh_ref[0], ys0_ref[i])
            mix_share(o, 1, i, zs1_ref, ys1_ref)

    def phase_d(o):
        for i in range(NB):
            rows = slice(i * P, (i + 1) * P)
            conv = acc_ref[i] + _bdot(csh_ref[1], ys1_ref[i])
            z_new = gate_ref[rows, :] * (conv + z_ref[rows, :] * skip_ref[o:o + 1, :])
            if o == 0:
                z_ref[rows, :] = z_new
                zcat_ref[i] = z_new.astype(BF16)
            else:
                o_ref[rows, :] = z_new.astype(BF16)

    for o, (g_ref, wg_ref) in enumerate(((g1_ref, wg1_ref), (g2_ref, wg2_ref))):
        phases = (functools.partial(phase_a, o, g_ref, wg_ref), functools.partial(phase_b, o),
                  functools.partial(phase_c, o), functools.partial(phase_d, o))
        for k, phase in enumerate(phases):
            pl.when(always(len(phases) * o + k))(phase)


def _hyena(proj3d, conv_w, skip, kre, kim, hny, csv, csh):
    bsz = proj3d.shape[0]
    ct = HY_CT
    nct = HALF // ct
    assert proj3d.shape[2] == 3 * HALF

    def slab(part):
        return pl.BlockSpec((None, SEQ, ct), lambda c, b: (b, 0, part * nct + c))

    def cw(part):
        return pl.BlockSpec((3, ct), lambda c, b: (0, part * nct + c))

    assert HY_NB == 4 and HY_NQ == 2 and HY_FQ % (MIX_ROWS * HY_NB) == 0 and ct % MIX_LANES == 0
    hspec = _resident((2, HY_COEF, HY_P, ct), lambda c, b: (0, 0, 0, c))
    const = lambda c, b: (0, 0)
    return pl.pallas_call(
        _hyena_kernel,
        out_shape=jax.ShapeDtypeStruct((bsz, SEQ, HALF), BF16),
        grid=(nct, bsz),
        in_specs=[slab(0), slab(1), slab(2), cw(0), cw(1), cw(2),
                  pl.BlockSpec((2, ct), lambda c, b: (0, c)),
                  hspec, hspec,
                  pl.BlockSpec((2, HY_LAGS, ct), lambda c, b: (0, 0, c)),
                  pl.BlockSpec((2 * HY_P, HY_P), const), pl.BlockSpec((HY_NQ, HY_P, 2 * HY_FQ), lambda c, b: (0, 0, 0))],
        out_specs=pl.BlockSpec((None, SEQ, ct), lambda c, b: (b, 0, c)),
        scratch_shapes=[pltpu.VMEM((SEQ, ct), F32), pltpu.VMEM((SEQ, ct), F32),
                        pltpu.VMEM((HY_NB, HY_P, ct), BF16),
                        pltpu.VMEM((HY_NB, 2 * HY_FQ, ct), F32),
                        pltpu.VMEM((HY_NB, 2 * HY_FQ, ct), F32),
                        pltpu.VMEM((HY_NB, 2 * HY_FQ, ct), BF16),
                        pltpu.VMEM((HY_NB, 2 * HY_FQ, ct), BF16),
                        pltpu.VMEM((HY_NB, HY_P, ct), F32)],
        compiler_params=_params(2),
        name="hyena_long_conv",
    )(proj3d, proj3d, proj3d, conv_w, conv_w, conv_w, skip, kre, kim, hny, csv, csh)


def _window_sum(a, w):
    n = a.shape[0]
    right = w - w // 2 - 1
    s = a
    k = 1
    while k < w:
        s = s + pltpu.roll(s, k, axis=0)
        k *= 2
    return pltpu.roll(s, n - right, axis=0) if right else s


def _pool_conv_kernel(p_ref, b_ref, c_ref, h_ref, cw_ref, scale_ref, dw_ref, yc_ref, yd_ref,
                      chs_ref, pooled_ref, band_ref):
    L, R, n = SEQ, PW_ROWS, PW_ROWS + 2 * HALO
    offs = (0, (POOL_K - POOL_TILE) // 2, POOL_K - POOL_TILE)

    @pl.when(pl.program_id(0) == 0)
    def _():
        r = lax.broadcasted_iota(jnp.int32, (POOL_TILE, POOL_K), 0)
        c = lax.broadcasted_iota(jnp.int32, (POOL_TILE, POOL_K), 1)
        for g, w in enumerate(POOL_WINDOWS):
            left = w // 2
            for kind, off in enumerate(offs):
                d = c - off - r + left
                m = jnp.where((d >= 0) & (d < w), 1.0 / w, 0.0) - jnp.where(d == left, 1.0, 0.0)
                band_ref[g, kind] = m.astype(BF16)

    zeros = jnp.zeros((HALO, HALF), F32)
    chs_ref[0:HALO, :] = zeros
    chs_ref[L + HALO:L + 2 * HALO, :] = zeros

    for i in range(L // R):
        r = slice(i * R, (i + 1) * R)
        chs_ref[HALO + i * R:HALO + (i + 1) * R, :] = (
            c_ref[r, :].astype(F32) * h_ref[r, :].astype(F32))

    n_tiles = L // POOL_TILE
    for g, w in enumerate(POOL_WINDOWS):
        cols = slice(g * GROUP_DIM, (g + 1) * GROUP_DIM)
        left = w // 2
        right = w - left - 1

        for k in range(n_tiles):
            kind = 0 if k == 0 else (2 if k == n_tiles - 1 else 1)
            t0 = k * POOL_TILE
            ws = t0 - offs[kind]
            pooled_ref[g, t0:t0 + POOL_TILE, :] = _bdot(
                band_ref[g, kind], p_ref[ws:ws + POOL_K, cols]).astype(BF16)

        for top in (True, False):
            pad = jnp.zeros((POOL_EDGE, GROUP_DIM), F32)
            if top:
                a = jnp.concatenate([pad, p_ref[0:2 * POOL_EDGE, cols].astype(F32)], axis=0)
                first = 0
            else:
                a = jnp.concatenate([p_ref[L - 2 * POOL_EDGE:L, cols].astype(F32), pad], axis=0)
                first = L - POOL_EDGE
            keep = slice(POOL_EDGE, 2 * POOL_EDGE)
            t = first + lax.broadcasted_iota(jnp.int32, (POOL_EDGE, GROUP_DIM), 0)
            cnt = (jnp.minimum(t + right, L - 1) - jnp.maximum(t - left, 0) + 1).astype(F32)
            pooled = _window_sum(a, w)[keep] / cnt - a[keep]
            pooled_ref[g, first:first + POOL_EDGE, :] = pooled.astype(BF16)

        dw = dw_ref[:, cols]
        for i in range(L // R):
            r0 = i * R
            e = chs_ref[r0:r0 + n, cols]
            conv = (pltpu.roll(e, 1, axis=0) * dw[0:1, :] + e * dw[1:2, :]
                    + pltpu.roll(e, n - 1, axis=0) * dw[2:3, :])
            yd = b_ref[r0:r0 + R, cols].astype(F32) * conv[HALO:HALO + R]
            yd_ref[r0:r0 + R, cols] = yd.astype(BF16)

        y = _bdot(pooled_ref[g], cw_ref[g]) * scale_ref[:, cols]
        yc_ref[:, cols] = y.astype(BF16)


def _pool_conv(proj3d, c_w, c_scale, d_conv_w):
    bsz = proj3d.shape[0]

    def part(k):
        return pl.BlockSpec((None, SEQ, HALF), lambda b: (b, 0, k))

    out = pl.BlockSpec((None, SEQ, HALF), lambda b: (b, 0, 0))
    return pl.pallas_call(
        _pool_conv_kernel,
        out_shape=(jax.ShapeDtypeStruct((bsz, SEQ, HALF), BF16),) * 2,
        grid=(bsz,),
        in_specs=[part(0), part(1), part(2), part(3),
                  pl.BlockSpec((len(POOL_WINDOWS), GROUP_DIM, GROUP_DIM), lambda b: (0, 0, 0)),
                  pl.BlockSpec((1, HALF), lambda b: (0, 0)),
                  pl.BlockSpec((3, HALF), lambda b: (0, 0))],
        out_specs=(out, out),
        scratch_shapes=[pltpu.VMEM((SEQ + 2 * HALO, HALF), F32),
                        pltpu.VMEM((len(POOL_WINDOWS), SEQ, GROUP_DIM), BF16),
                        pltpu.VMEM((len(POOL_WINDOWS), 3, POOL_TILE, POOL_K), BF16)],
        compiler_params=_params(1),
        name="pool_short_conv",
    )(proj3d, proj3d, proj3d, proj3d, c_w, c_scale, d_conv_w)


def _out_ffn_kernel(x_ref, y0_ref, y1_ref, wo_ref, g_ref, wgu_ref, wd_ref, o_ref, h_ref, a_ref):
    sub = ROW_TILE // FFN_SUBTILES
    rows = [slice(s * sub, (s + 1) * sub) for s in range(FFN_SUBTILES)]
    def mix_norm(r):
        mix = _bdot(y0_ref[r, :], wo_ref[:HALF, :]) + _bdot(y1_ref[r, :], wo_ref[HALF:, :])
        o_ref[r, :] = x_ref[r, :] + _rms(mix, g_ref[1:2, :])
        h_ref[r, :] = _rms(o_ref[r, :], g_ref[2:3, :]).astype(BF16)

    ahead = 2
    for r in rows[:ahead]:
        mix_norm(r)
    for s, r in enumerate(rows):
        for k in range(D_FF // FF_CHUNK):
            gate = _bdot(h_ref[r, :], wgu_ref[:, k * FF_CHUNK:(k + 1) * FF_CHUNK])
            up = _bdot(h_ref[r, :], wgu_ref[:, D_FF + k * FF_CHUNK:D_FF + (k + 1) * FF_CHUNK])
            a_ref[r, k * FF_CHUNK:(k + 1) * FF_CHUNK] = (
                gate * jax.nn.sigmoid(gate) * up).astype(BF16)
            if k == 0 and s + ahead < len(rows):
                mix_norm(rows[s + ahead])
    for r in rows:
        f = _bdot(a_ref[r, :], wd_ref[...])
        o_ref[r, :] = o_ref[r, :] + _rms(f, g_ref[3:4, :])


def _out_ffn(x2d, y0, y1, w_out, mix_layer, g, w_gu, w_down, layer):
    m = x2d.shape[0]
    tile = lambda n: pl.BlockSpec((ROW_TILE, n), lambda i: (i, 0))
    return pl.pallas_call(
        _out_ffn_kernel,
        out_shape=jax.ShapeDtypeStruct((m, D_MODEL), F32),
        grid=(m // ROW_TILE,),
        in_specs=[tile(D_MODEL), tile(HALF), tile(HALF),
                  _resident((None, D_MODEL, D_MODEL), lambda i: (mix_layer, 0, 0)),
                  pl.BlockSpec((None, 4, D_MODEL), lambda i: (layer, 0, 0)),
                  _resident((None, D_MODEL, 2 * D_FF), lambda i: (layer, 0, 0)),
                  _resident((None, D_FF, D_MODEL), lambda i: (layer, 0, 0))],
        out_specs=tile(D_MODEL),
        scratch_shapes=[pltpu.VMEM((ROW_TILE, D_MODEL), BF16),
                        pltpu.VMEM((ROW_TILE, D_FF), BF16)],
        compiler_params=_params(1, FFN_VMEM_LIMIT),
        name="out_proj_ffn",
    )(x2d, y0, y1, w_out, g, w_gu, w_down)


def kernel(x, norm_g, ffn_w_gu, ffn_w_down, ab_w_in, ab_w_out, a_ln_g, a_w_s, a_b_s, b_conv_w, b_filt_w1, b_filt_b1, b_filt_freq, b_filt_w2, b_filt_b2, b_filt_w3, b_decay, b_skip, cd_w_in, cd_w_out, c_w, c_scale, d_conv_w):
    bsz, seq, d = x.shape
    assert (seq, d) == (SEQ, D_MODEL) and norm_g.shape[0] == DEPTH
    m = bsz * seq
    bands = jnp.linspace(1e-4, FILTER_BANDS - 1, FILTER_BANDS, dtype=F32)[None, :]
    csv, csh = _dft_tables()
    x2d = x.reshape(m, d)
    ab_w_in, ab_w_out, cd_w_in, cd_w_out, ffn_w_gu, ffn_w_down = (
        w.astype(BF16) for w in (ab_w_in, ab_w_out, cd_w_in, cd_w_out, ffn_w_gu, ffn_w_down))
    for i in range(DEPTH):
        j = i // 2
        g = norm_g[i]
        if i % 2 == 0:
            proj, y0 = _in_proj_gmlp(x2d, g[0:1], ab_w_in, j, a_ln_g[j][None, :],
                                     a_w_s[j].astype(BF16), a_b_s[j][:, :, None])
            proj = proj.reshape(bsz, seq, -1)
            hidden = _filter_hidden(bands, b_filt_w1[j], b_filt_b1[j][None, :], b_filt_freq[j],
                                    b_filt_w2[j], b_filt_b2[j][None, :])
            kre, kim, hny = _filter_spectra(hidden, b_filt_w3[j], b_decay[j][None, :], csv)
            y1 = _hyena(proj, b_conv_w[j], b_skip[j], kre, kim, hny, csv, csh)
            w_out = ab_w_out
        else:
            proj = _in_proj(x2d, g[0:1], cd_w_in, j).reshape(bsz, seq, -1)
            y0, y1 = _pool_conv(proj, c_w[j].astype(BF16), c_scale[j][None, :], d_conv_w[j])
            w_out = cd_w_out
        x2d = _out_ffn(x2d, y0.reshape(m, HALF), y1.reshape(m, HALF), w_out, j, norm_g,
                       ffn_w_gu, ffn_w_down, i)
    return x2d.reshape(bsz, seq, d)
```

```python
import functools
import math

import jax
import jax.numpy as jnp
from jax import lax
from jax.experimental import pallas as pl
from jax.experimental.pallas import tpu as pltpu

F32 = jnp.float32
BF16 = jnp.bfloat16
LANES = 128
BF16_ROWS = 16

D_MODEL = 1024
SEQ = 2048
DEPTH = 4
HALF = D_MODEL // 2
CHUNK = 128
N_CHUNKS = SEQ // CHUNK
A_GROUPS = 4
GROUP_DIM = HALF // A_GROUPS
FILTER_BANDS = 16
FILTER_HIDDEN = 64
POOL_WINDOWS = (2, 4, 8, 16)
D_FF = 2816
RMS_EPS = 1e-6
LN_EPS = 1e-5
HY_P = 512
HY_NB = SEQ // HY_P
HY_LAGS = 2 * HY_NB - 1
NFFT = 2 * HY_P
HY_NQ = 2
HY_FQ = HY_P // HY_NQ
HY_COEF = 9

ROW_TILE = 1024
FFN_SUBTILES = 4
FFN_VMEM_LIMIT = 60 * 1024 * 1024
IN_TILE = 2048
IN_SUBTILES = 4
IN_GMLP_SUBTILES = 8
IN_CHUNK = 256
FF_CHUNK = 256
HY_CT = 256
HY_CONV_ROWS = 128
MIX_ROWS = 16
MIX_LANES = 128
PW_ROWS = 128
HALO = 8
POOL_TILE = 256
POOL_K = 512
POOL_EDGE = 16
VMEM_LIMIT = 56 * 1024 * 1024


def _params(n_axes, vmem=VMEM_LIMIT):
    return pltpu.CompilerParams(
        dimension_semantics=("arbitrary",) * n_axes, vmem_limit_bytes=vmem)


def _resident(shape, index_map):
    return pl.BlockSpec(shape, index_map, pipeline_mode=pl.Buffered(1))


def _rms(x, g):
    return x * lax.rsqrt(jnp.mean(x * x, axis=-1, keepdims=True) + RMS_EPS) * g


def _bdot(a, b):
    return jnp.dot(a, b, preferred_element_type=F32)


def _fdot(a, b):
    return jnp.dot(a, b, preferred_element_type=F32, precision=lax.Precision.HIGHEST)


def _shift_rows(x, k, row):
    n = x.shape[0]
    y = pltpu.roll(x, k % n, axis=0)
    if k > 0:
        return jnp.where(row >= k, y, 0.0)
    return jnp.where(row < n + k, y, 0.0)


def _conv3(x, w, row):
    return (_shift_rows(x, 1, row) * w[0:1, :] + x * w[1:2, :]
            + _shift_rows(x, -1, row) * w[2:3, :])


def _dft_kernel(csv_ref, csh_ref):
    P, FQ = HY_P, HY_FQ
    f = lax.broadcasted_iota(jnp.int32, (P, LANES), 0)
    lane = lax.broadcasted_iota(jnp.int32, (P, LANES), 1)
    scale = 2.0 * math.pi / NFFT
    ang_b = ((f * lane) & (NFFT - 1)).astype(F32) * scale
    ang_a = ((f * (lane * LANES)) & (NFFT - 1)).astype(F32) * scale
    cb, sb = jnp.cos(ang_b), jnp.sin(ang_b)
    ca, sa = jnp.cos(ang_a), jnp.sin(ang_a)
    for a in range(P // LANES):
        cols = slice(a * LANES, (a + 1) * LANES)
        ca_a = ca[:, a:a + 1]
        sa_a = sa[:, a:a + 1]
        c = (ca_a * cb - sa_a * sb).astype(BF16)
        s = sa_a * cb + ca_a * sb
        s_v = jnp.where(f == 0, jnp.where((lane & 1) == 0, 1.0, -1.0), s).astype(BF16)
        for q in range(HY_NQ):
            csv_ref[2 * q * FQ:(2 * q + 1) * FQ, cols] = c[q * FQ:(q + 1) * FQ]
            csv_ref[(2 * q + 1) * FQ:(2 * q + 2) * FQ, cols] = s_v[q * FQ:(q + 1) * FQ]
        s_h = s
        if a == 0:
            s_h = jnp.where(lane == 0, jnp.where((f & 1) == 0, 1.0, -1.0), s)
        q, off = divmod(a * LANES, FQ)
        csh_ref[q, :, off:off + LANES] = c
        csh_ref[q, :, FQ + off:FQ + off + LANES] = s_h.astype(BF16)


def _dft_tables():
    return pl.pallas_call(
        _dft_kernel,
        out_shape=(jax.ShapeDtypeStruct((2 * HY_P, HY_P), BF16),
                   jax.ShapeDtypeStruct((HY_NQ, HY_P, 2 * HY_FQ), BF16)),
        compiler_params=_params(0),
        name="dft_tables",
    )()


def _filter_hidden_kernel(bands_ref, w1_ref, b1_ref, freq_ref, w2_ref, b2_ref, h_ref):
    L = SEQ
    r = lax.broadcasted_iota(jnp.int32, (L, 1), 0)
    t = jnp.where(pl.program_id(0) == 0, r, L - 1 - r).astype(F32)
    t01 = t / (L - 1)
    fw = (2 * math.pi * t / L) * bands_ref[...]
    pre = (t01 * w1_ref[0:1, :] + _fdot(jnp.cos(fw), w1_ref[1:1 + FILTER_BANDS, :])
           + _fdot(-jnp.sin(fw), w1_ref[1 + FILTER_BANDS:, :]) + b1_ref[...])
    h = jnp.sin(freq_ref[0:1, :] * pre)
    h_ref[...] = jnp.sin(freq_ref[1:2, :] * (_fdot(h, w2_ref[...]) + b2_ref[...]))


def _filter_hidden(bands, w1, b1, freq, w2, b2):
    const = lambda s: (0, 0)
    return pl.pallas_call(
        _filter_hidden_kernel,
        out_shape=jax.ShapeDtypeStruct((2, SEQ, FILTER_HIDDEN), F32),
        grid=(2,),
        in_specs=[pl.BlockSpec((1, FILTER_BANDS), const),
                  pl.BlockSpec((1 + 2 * FILTER_BANDS, FILTER_HIDDEN), const),
                  pl.BlockSpec((1, FILTER_HIDDEN), const),
                  pl.BlockSpec((2, FILTER_HIDDEN), const),
                  pl.BlockSpec((FILTER_HIDDEN, FILTER_HIDDEN), const),
                  pl.BlockSpec((1, FILTER_HIDDEN), const)],
        out_specs=pl.BlockSpec((None, SEQ, FILTER_HIDDEN), lambda s: (s, 0, 0)),
        compiler_params=_params(1),
        name="hyena_filter_hidden",
    )(bands, w1, b1, freq, w2, b2)


def _filter_kernel(hid_ref, w3f_ref, w3b_ref, decf_ref, decb_ref, csv_ref,
                   kre_ref, kim_ref, hny_ref):
    L, P, NB = SEQ, HY_P, HY_NB
    ct = kre_ref.shape[-1]
    row = lax.broadcasted_iota(jnp.int32, (L, ct), 0)
    t_asc = row.astype(F32) / (L - 1)
    t_dsc = (L - 1 - row).astype(F32) / (L - 1)
    dec_f = jnp.abs(decf_ref[...])
    dec_b = jnp.abs(decb_ref[...])
    h_asc = hid_ref[0]
    h_dsc = hid_ref[1]
    inner = (row & (P - 1)) != 0
    f_asc = _fdot(h_asc, w3f_ref[...]) * jnp.exp(-t_asc * dec_f)
    b_dsc = _fdot(h_dsc, w3b_ref[...]) * jnp.exp(-t_dsc * dec_b)
    f_dsc = _fdot(h_dsc, w3f_ref[...]) * jnp.exp(-t_dsc * dec_f)
    f_dsc = jnp.where(inner, _shift_rows(f_dsc, 1, row), 0.0)
    b_asc = _fdot(h_asc, w3b_ref[...]) * jnp.exp(-t_asc * dec_b)
    b_asc = jnp.where(inner, _shift_rows(b_asc, 1, row), 0.0)

    def blk(x, i):
        return x[i * P:(i + 1) * P, :]

    frow = lax.broadcasted_iota(jnp.int32, (P, ct), 0)
    wgt = jnp.where(frow == 0, 1.0 / NFFT, 2.0 / NFFT)
    sign = jnp.where((frow & 1) == 0, 1.0, -1.0)
    FQ = HY_FQ
    h = {}
    for idx in range(HY_LAGS):
        d = idx - (NB - 1)
        if d >= 1:
            kp, kn = blk(f_asc, d), blk(f_dsc, NB - d)
        elif d == 0:
            kp, kn = blk(f_asc, 0), blk(b_asc, 0)
        else:
            kp, kn = blk(b_dsc, NB + d), blk(b_asc, -d)
        ev = kp + kn
        od = kp - kn
        evb = ev.astype(BF16)
        odb = od.astype(BF16)
        re = jnp.concatenate([_bdot(csv_ref[2 * q * FQ:(2 * q + 1) * FQ, :], evb)
                              for q in range(HY_NQ)], axis=0)
        im = jnp.concatenate([_bdot(csv_ref[(2 * q + 1) * FQ:(2 * q + 2) * FQ, :], odb)
                              for q in range(HY_NQ)], axis=0)
        h[d] = (re * wgt, jnp.where(frow == 0, 0.0, -im * wgt))
        hny_ref[idx:idx + 1, :] = jnp.sum(ev * sign, axis=0, keepdims=True) * (1.0 / NFFT)

    def sub(a, b):
        return (a[0] - b[0], a[1] - b[1])

    def coef3(a0, a_dn, a_up):
        return [a0, sub(a_dn, a0), sub(a_up, a0)]

    coefs = (coef3(h[0], h[1], h[-1])
             + coef3(sub(h[2], h[0]), sub(h[3], h[1]), sub(h[1], h[-1]))
             + coef3(sub(h[-2], h[0]), sub(h[-1], h[1]), sub(h[-3], h[-1])))
    for c, (re, im) in enumerate(coefs):
        kre_ref[c] = re
        kim_ref[c] = im


def _filter_spectra(hidden, w3, decay, csv):
    ct = HY_CT
    nct = HALF // ct
    fcol = lambda o, c: (0, o * nct + c)
    bcol = lambda o, c: (0, (2 + o) * nct + c)
    hspec = pl.BlockSpec((None, HY_COEF, HY_P, ct), lambda o, c: (o, 0, 0, c))
    return pl.pallas_call(
        _filter_kernel,
        out_shape=(jax.ShapeDtypeStruct((2, HY_COEF, HY_P, HALF), F32),
                   jax.ShapeDtypeStruct((2, HY_COEF, HY_P, HALF), F32),
                   jax.ShapeDtypeStruct((2, HY_LAGS, HALF), F32)),
        grid=(2, nct),
        in_specs=[
            pl.BlockSpec((2, SEQ, FILTER_HIDDEN), lambda o, c: (0, 0, 0)),
            pl.BlockSpec((FILTER_HIDDEN, ct), fcol),
            pl.BlockSpec((FILTER_HIDDEN, ct), bcol),
            pl.BlockSpec((1, ct), fcol),
            pl.BlockSpec((1, ct), bcol),
            pl.BlockSpec((2 * HY_P, HY_P), lambda o, c: (0, 0)),
        ],
        out_specs=(hspec, hspec,
                   pl.BlockSpec((None, HY_LAGS, ct), lambda o, c: (o, 0, c))),
        compiler_params=_params(2),
        name="hyena_filter_spectra",
    )(hidden, w3, w3, decay, decay, csv)


def _in_proj_kernel(x_ref, g_ref, w_ref, o_ref, h_ref):
    sub = IN_TILE // IN_SUBTILES
    rows = [slice(s * sub, (s + 1) * sub) for s in range(IN_SUBTILES)]
    for r in rows:
        h_ref[r, :] = _rms(x_ref[r, :], g_ref[...]).astype(BF16)
    for r in rows:
        for c0 in range(0, o_ref.shape[1], IN_CHUNK):
            o_ref[r, c0:c0 + IN_CHUNK] = _bdot(
                h_ref[r, :], w_ref[:, c0:c0 + IN_CHUNK]).astype(BF16)


def _in_proj(x2d, g, w, layer):
    m = x2d.shape[0]
    n_out = w.shape[2]
    return pl.pallas_call(
        _in_proj_kernel,
        out_shape=jax.ShapeDtypeStruct((m, n_out), BF16),
        grid=(m // IN_TILE,),
        in_specs=[pl.BlockSpec((IN_TILE, D_MODEL), lambda i: (i, 0)),
                  pl.BlockSpec((1, D_MODEL), lambda i: (0, 0)),
                  _resident((None, D_MODEL, n_out), lambda i: (layer, 0, 0))],
        out_specs=pl.BlockSpec((IN_TILE, n_out), lambda i: (i, 0)),
        scratch_shapes=[pltpu.VMEM((IN_TILE, D_MODEL), BF16)],
        compiler_params=_params(1),
        name="rms_in_proj",
    )(x2d, g, w)


def _in_proj_gmlp_kernel(x_ref, g_ref, w_ref, lng_ref, ws_ref, bs_ref, o_ref, ya_ref,
                         h_ref, za_ref, vn_ref):
    sub = IN_TILE // IN_GMLP_SUBTILES
    rows = [slice(s * sub, (s + 1) * sub) for s in range(IN_GMLP_SUBTILES)]
    for r in rows:
        h_ref[r, :] = _rms(x_ref[r, :], g_ref[...]).astype(BF16)
    n_gelu = 2 * HALF
    n_out = w_ref.shape[1]
    gelu_chunks = list(range(0, n_gelu, IN_CHUNK))
    plain_chunks = list(range(n_gelu, n_out, IN_CHUNK))
    order = [c for pair in zip(gelu_chunks, plain_chunks) for c in pair]
    order += gelu_chunks[len(plain_chunks):] + plain_chunks[len(gelu_chunks):]
    chunks_per_sub = sub // CHUNK

    def gate(s):
        for i in range(chunks_per_sub):
            rr = slice(s * sub + i * CHUNK, s * sub + (i + 1) * CHUNK)
            v = za_ref[rr, HALF:].astype(F32)
            xc = v - jnp.mean(v, axis=-1, keepdims=True)
            y = xc * lax.rsqrt(jnp.mean(xc * xc, axis=-1, keepdims=True) + LN_EPS) * lng_ref[...]
            vn_ref[rr, :] = y.astype(BF16)
        for g in range(A_GROUPS):
            cols = slice(g * GROUP_DIM, (g + 1) * GROUP_DIM)
            chunk_rows = [slice(s * sub + i * CHUNK, s * sub + (i + 1) * CHUNK)
                          for i in range(chunks_per_sub)]
            vg = jnp.concatenate([vn_ref[rr, cols] for rr in chunk_rows], axis=1)
            sg = _bdot(ws_ref[g], vg) + bs_ref[g]
            for i, rr in enumerate(chunk_rows):
                u = za_ref[rr, cols].astype(F32)
                ya_ref[rr, cols] = (u * sg[:, i * GROUP_DIM:(i + 1) * GROUP_DIM]).astype(BF16)

    for s, r in enumerate(rows):
        for k, c0 in enumerate(order):
            p = _bdot(h_ref[r, :], w_ref[:, c0:c0 + IN_CHUNK])
            if c0 < n_gelu:
                za_ref[r, c0:c0 + IN_CHUNK] = jax.nn.gelu(p).astype(BF16)
            else:
                o_ref[r, c0 - n_gelu:c0 - n_gelu + IN_CHUNK] = p.astype(BF16)
            if k == 1 and s > 0:
                gate(s - 1)
    gate(len(rows) - 1)


def _in_proj_gmlp(x2d, g, w, layer, ln_g, w_s, b_s):
    assert IN_TILE == SEQ
    m = x2d.shape[0]
    n_out = w.shape[2]
    n_hy = n_out - 2 * HALF
    return pl.pallas_call(
        _in_proj_gmlp_kernel,
        out_shape=(jax.ShapeDtypeStruct((m, n_hy), BF16), jax.ShapeDtypeStruct((m, HALF), BF16)),
        grid=(m // IN_TILE,),
        in_specs=[pl.BlockSpec((IN_TILE, D_MODEL), lambda i: (i, 0)),
                  pl.BlockSpec((1, D_MODEL), lambda i: (0, 0)),
                  _resident((None, D_MODEL, n_out), lambda i: (layer, 0, 0)),
                  pl.BlockSpec((1, HALF), lambda i: (0, 0)),
                  pl.BlockSpec((A_GROUPS, CHUNK, CHUNK), lambda i: (0, 0, 0)),
                  pl.BlockSpec((A_GROUPS, CHUNK, 1), lambda i: (0, 0, 0))],
        out_specs=(pl.BlockSpec((IN_TILE, n_hy), lambda i: (i, 0)),
                   pl.BlockSpec((IN_TILE, HALF), lambda i: (i, 0))),
        scratch_shapes=[pltpu.VMEM((IN_TILE, D_MODEL), BF16),
                        pltpu.VMEM((IN_TILE, 2 * HALF), BF16),
                        pltpu.VMEM((IN_TILE, HALF), BF16)],
        compiler_params=_params(1, FFN_VMEM_LIMIT),
        name="rms_in_proj_gmlp",
    )(x2d, g, w, ln_g, w_s, b_s)


def _cmul(z, h):
    return z[0] * h[0] + z[1] * h[1], z[1] * h[0] - z[0] * h[1]


def _cadd(a, b):
    return a[0] + b[0], a[1] + b[1]


def _mix_tile(o, q, row0, lh, zs_ref, kre_ref, kim_ref, ys_ref):
    FQ = HY_FQ
    fr = slice(q * FQ + row0, q * FQ + row0 + MIX_ROWS)
    rr = slice(row0, row0 + MIX_ROWS)
    rm = slice(FQ + row0, FQ + row0 + MIX_ROWS)
    lanes = slice(lh * MIX_LANES, (lh + 1) * MIX_LANES)

    def coef(c):
        return kre_ref[o, c, fr, lanes], kim_ref[o, c, fr, lanes]

    def toep2(c, v0, v1):
        q1 = _cmul(_cadd(v0, v1), coef(c))
        q2 = _cmul(v0, coef(c + 1))
        q3 = _cmul(v1, coef(c + 2))
        return _cadd(q1, q3), _cadd(q1, q2)

    z = [(zs_ref[j, rr, lanes], zs_ref[j, rm, lanes]) for j in range(HY_NB)]
    p1 = toep2(0, _cadd(z[0], z[2]), _cadd(z[1], z[3]))
    p2 = toep2(3, z[0], z[1])
    p3 = toep2(6, z[2], z[3])
    y = [_cadd(p1[0], p3[0]), _cadd(p1[1], p3[1]), _cadd(p1[0], p2[0]), _cadd(p1[1], p2[1])]
    for i in range(HY_NB):
        ys_ref[i, rr, lanes] = y[i][0].astype(BF16)
        ys_ref[i, rm, lanes] = y[i][1].astype(BF16)


def _conv3_rows(src_ref, w, start):
    L, R, T, H = SEQ, HY_CONV_ROWS, BF16_ROWS, HALO
    x = src_ref[start:start + R, :].astype(F32)
    zeros = jnp.zeros((H, x.shape[1]), F32)
    prev = src_ref[start - T:start, :].astype(F32)[T - H:] if start > 0 else zeros
    nxt = src_ref[start + R:start + R + T, :].astype(F32)[:H] if start + R < L else zeros
    a = jnp.concatenate([prev, x, nxt], axis=0)
    n = R + 2 * H
    y = pltpu.roll(a, 1, axis=0) * w[0:1, :] + a * w[1:2, :] + pltpu.roll(a, n - 1, axis=0) * w[2:3, :]
    return y[H:H + R]


def _hyena_kernel(v_ref, g1_ref, g2_ref, wv_ref, wg1_ref, wg2_ref, skip_ref,
                  kre_ref, kim_ref, hny_ref, csv_ref, csh_ref, o_ref,
                  z_ref, gate_ref, zcat_ref, zs0_ref, zs1_ref, ys0_ref, ys1_ref, acc_ref):
    P, NB, FQ, R = HY_P, HY_NB, HY_FQ, HY_CONV_ROWS
    ct = o_ref.shape[1]
    steps = FQ // MIX_ROWS // NB

    def chunk(q):
        return slice(2 * q * FQ, 2 * (q + 1) * FQ)

    def mix_share(o, q, k, zs_ref, ys_ref):
        for s in range(steps):
            for lh in range(ct // MIX_LANES):
                _mix_tile(o, q, (k * steps + s) * MIX_ROWS, lh, zs_ref, kre_ref, kim_ref, ys_ref)

    for j in range(NB):
        for r in range(P // R):
            zr = _conv3_rows(v_ref, wv_ref[...], j * P + r * R)
            z_ref[j * P + r * R:j * P + (r + 1) * R, :] = zr
            zcat_ref[j, r * R:(r + 1) * R, :] = zr.astype(BF16)

    def always(k):
        return pl.program_id(1) + k >= 0

    def phase_a(o, g_ref, wg_ref):
        for j in range(NB):
            zs0_ref[j] = _bdot(csv_ref[chunk(0), :], zcat_ref[j])
            for r in range(P // R):
                start = j * P + r * R
                gate_ref[start:start + R, :] = _conv3_rows(g_ref, wg_ref[...], start)

    def phase_b(o):
        for j in range(NB):
            zs1_ref[j] = _bdot(csv_ref[chunk(1), :], zcat_ref[j])
            mix_share(o, 0, j, zs0_ref, ys0_ref)
        tile = slice(FQ, FQ + MIX_ROWS)
        first = lax.broadcasted_iota(jnp.int32, (MIX_ROWS, ct), 0) == 0
        z_ny = [zs0_ref[j, tile, :] for j in range(NB)]
        for i in range(NB):
            y_ny = sum(z_ny[j] * hny_ref[o, i - j + NB - 1:i - j + NB, :] for j in range(NB))
            ys0_ref[i, tile, :] = jnp.where(
                first, y_ny, ys0_ref[i, tile, :].astype(F32)).astype(BF16)

    def phase_c(o):
        for i in range(NB):
            acc_ref[i] = _bdot(csh_ref[0], ys0_ref[i])
            mix_share(o, 1, i, zs1_ref, ys1_ref)

    def phase_d(o):
        for i in range(NB):
            rows = slice(i * P, (i + 1) * P)
            conv = acc_ref[i] + _bdot(csh_ref[1], ys1_ref[i])
            z_new = gate_ref[rows, :] * (conv + z_ref[rows, :] * skip_ref[o:o + 1, :])
            if o == 0:
                z_ref[rows, :] = z_new
                zcat_ref[i] = z_new.astype(BF16)
            else:
                o_ref[rows, :] = z_new.astype(BF16)

    for o, (g_ref, wg_ref) in enumerate(((g1_ref, wg1_ref), (g2_ref, wg2_ref))):
        phases = (functools.partial(phase_a, o, g_ref, wg_ref), functools.partial(phase_b, o),
                  functools.partial(phase_c, o), functools.partial(phase_d, o))
        for k, phase in enumerate(phases):
            pl.when(always(len(phases) * o + k))(phase)


def _hyena(proj3d, conv_w, skip, kre, kim, hny, csv, csh):
    bsz = proj3d.shape[0]
    ct = HY_CT
    nct = HALF // ct
    assert proj3d.shape[2] == 3 * HALF

    def slab(part):
        return pl.BlockSpec((None, SEQ, ct), lambda c, b: (b, 0, part * nct + c))

    def cw(part):
        return pl.BlockSpec((3, ct), lambda c, b: (0, part * nct + c))

    assert HY_NB == 4 and HY_NQ == 2 and HY_FQ % (MIX_ROWS * HY_NB) == 0 and ct % MIX_LANES == 0
    hspec = _resident((2, HY_COEF, HY_P, ct), lambda c, b: (0, 0, 0, c))
    const = lambda c, b: (0, 0)
    return pl.pallas_call(
        _hyena_kernel,
        out_shape=jax.ShapeDtypeStruct((bsz, SEQ, HALF), BF16),
        grid=(nct, bsz),
        in_specs=[slab(0), slab(1), slab(2), cw(0), cw(1), cw(2),
                  pl.BlockSpec((2, ct), lambda c, b: (0, c)),
                  hspec, hspec,
                  pl.BlockSpec((2, HY_LAGS, ct), lambda c, b: (0, 0, c)),
                  pl.BlockSpec((2 * HY_P, HY_P), const), pl.BlockSpec((HY_NQ, HY_P, 2 * HY_FQ), lambda c, b: (0, 0, 0))],
        out_specs=pl.BlockSpec((None, SEQ, ct), lambda c, b: (b, 0, c)),
        scratch_shapes=[pltpu.VMEM((SEQ, ct), F32), pltpu.VMEM((SEQ, ct), F32),
                        pltpu.VMEM((HY_NB, HY_P, ct), BF16),
                        pltpu.VMEM((HY_NB, 2 * HY_FQ, ct), F32),
                        pltpu.VMEM((HY_NB, 2 * HY_FQ, ct), F32),
                        pltpu.VMEM((HY_NB, 2 * HY_FQ, ct), BF16),
                        pltpu.VMEM((HY_NB, 2 * HY_FQ, ct), BF16),
                        pltpu.VMEM((HY_NB, HY_P, ct), F32)],
        compiler_params=_params(2),
        name="hyena_long_conv",
    )(proj3d, proj3d, proj3d, conv_w, conv_w, conv_w, skip, kre, kim, hny, csv, csh)


def _window_sum(a, w):
    n = a.shape[0]
    right = w - w // 2 - 1
    s = a
    k = 1
    while k < w:
        s = s + pltpu.roll(s, k, axis=0)
        k *= 2
    return pltpu.roll(s, n - right, axis=0) if right else s


def _pool_conv_kernel(p_ref, b_ref, c_ref, h_ref, cw_ref, scale_ref, dw_ref, yc_ref, yd_ref,
                      chs_ref, pooled_ref, band_ref):
    L, R, n = SEQ, PW_ROWS, PW_ROWS + 2 * HALO
    offs = (0, (POOL_K - POOL_TILE) // 2, POOL_K - POOL_TILE)

    @pl.when(pl.program_id(0) == 0)
    def _():
        r = lax.broadcasted_iota(jnp.int32, (POOL_TILE, POOL_K), 0)
        c = lax.broadcasted_iota(jnp.int32, (POOL_TILE, POOL_K), 1)
        for g, w in enumerate(POOL_WINDOWS):
            left = w // 2
            for kind, off in enumerate(offs):
                d = c - off - r + left
                m = jnp.where((d >= 0) & (d < w), 1.0 / w, 0.0) - jnp.where(d == left, 1.0, 0.0)
                band_ref[g, kind] = m.astype(BF16)

    zeros = jnp.zeros((HALO, HALF), F32)
    chs_ref[0:HALO, :] = zeros
    chs_ref[L + HALO:L + 2 * HALO, :] = zeros

    for i in range(L // R):
        r = slice(i * R, (i + 1) * R)
        chs_ref[HALO + i * R:HALO + (i + 1) * R, :] = (
            c_ref[r, :].astype(F32) * h_ref[r, :].astype(F32))

    n_tiles = L // POOL_TILE
    for g, w in enumerate(POOL_WINDOWS):
        cols = slice(g * GROUP_DIM, (g + 1) * GROUP_DIM)
        left = w // 2
        right = w - left - 1

        for k in range(n_tiles):
            kind = 0 if k == 0 else (2 if k == n_tiles - 1 else 1)
            t0 = k * POOL_TILE
            ws = t0 - offs[kind]
            pooled_ref[g, t0:t0 + POOL_TILE, :] = _bdot(
                band_ref[g, kind], p_ref[ws:ws + POOL_K, cols]).astype(BF16)

        for top in (True, False):
            pad = jnp.zeros((POOL_EDGE, GROUP_DIM), F32)
            if top:
                a = jnp.concatenate([pad, p_ref[0:2 * POOL_EDGE, cols].astype(F32)], axis=0)
                first = 0
            else:
                a = jnp.concatenate([p_ref[L - 2 * POOL_EDGE:L, cols].astype(F32), pad], axis=0)
                first = L - POOL_EDGE
            keep = slice(POOL_EDGE, 2 * POOL_EDGE)
            t = first + lax.broadcasted_iota(jnp.int32, (POOL_EDGE, GROUP_DIM), 0)
            cnt = (jnp.minimum(t + right, L - 1) - jnp.maximum(t - left, 0) + 1).astype(F32)
            pooled = _window_sum(a, w)[keep] / cnt - a[keep]
            pooled_ref[g, first:first + POOL_EDGE, :] = pooled.astype(BF16)

        dw = dw_ref[:, cols]
        for i in range(L // R):
            r0 = i * R
            e = chs_ref[r0:r0 + n, cols]
            conv = (pltpu.roll(e, 1, axis=0) * dw[0:1, :] + e * dw[1:2, :]
                    + pltpu.roll(e, n - 1, axis=0) * dw[2:3, :])
            yd = b_ref[r0:r0 + R, cols].astype(F32) * conv[HALO:HALO + R]
            yd_ref[r0:r0 + R, cols] = yd.astype(BF16)

        y = _bdot(pooled_ref[g], cw_ref[g]) * scale_ref[:, cols]
        yc_ref[:, cols] = y.astype(BF16)


def _pool_conv(proj3d, c_w, c_scale, d_conv_w):
    bsz = proj3d.shape[0]

    def part(k):
        return pl.BlockSpec((None, SEQ, HALF), lambda b: (b, 0, k))

    out = pl.BlockSpec((None, SEQ, HALF), lambda b: (b, 0, 0))
    return pl.pallas_call(
        _pool_conv_kernel,
        out_shape=(jax.ShapeDtypeStruct((bsz, SEQ, HALF), BF16),) * 2,
        grid=(bsz,),
        in_specs=[part(0), part(1), part(2), part(3),
                  pl.BlockSpec((len(POOL_WINDOWS), GROUP_DIM, GROUP_DIM), lambda b: (0, 0, 0)),
                  pl.BlockSpec((1, HALF), lambda b: (0, 0)),
                  pl.BlockSpec((3, HALF), lambda b: (0, 0))],
        out_specs=(out, out),
        scratch_shapes=[pltpu.VMEM((SEQ + 2 * HALO, HALF), F32),
                        pltpu.VMEM((len(POOL_WINDOWS), SEQ, GROUP_DIM), BF16),
                        pltpu.VMEM((len(POOL_WINDOWS), 3, POOL_TILE, POOL_K), BF16)],
        compiler_params=_params(1),
        name="pool_short_conv",
    )(proj3d, proj3d, proj3d, proj3d, c_w, c_scale, d_conv_w)


def _out_ffn_kernel(x_ref, y0_ref, y1_ref, wo_ref, g_ref, wgu_ref, wd_ref, o_ref, h_ref, a_ref):
    sub = ROW_TILE // FFN_SUBTILES
    rows = [slice(s * sub, (s + 1) * sub) for s in range(FFN_SUBTILES)]
    def mix_norm(r):
        mix = _bdot(y0_ref[r, :], wo_ref[:HALF, :]) + _bdot(y1_ref[r, :], wo_ref[HALF:, :])
        o_ref[r, :] = x_ref[r, :] + _rms(mix, g_ref[1:2, :])
        h_ref[r, :] = _rms(o_ref[r, :], g_ref[2:3, :]).astype(BF16)

    def down_norm(r):
        f = _bdot(a_ref[r, :], wd_ref[...])
        o_ref[r, :] = o_ref[r, :] + _rms(f, g_ref[3:4, :])

    ahead = 2
    for r in rows[:ahead]:
        mix_norm(r)
    for s, r in enumerate(rows):
        for k in range(D_FF // FF_CHUNK):
            gate = _bdot(h_ref[r, :], wgu_ref[:, k * FF_CHUNK:(k + 1) * FF_CHUNK])
            up = _bdot(h_ref[r, :], wgu_ref[:, D_FF + k * FF_CHUNK:D_FF + (k + 1) * FF_CHUNK])
            a_ref[r, k * FF_CHUNK:(k + 1) * FF_CHUNK] = (
                gate * jax.nn.sigmoid(gate) * up).astype(BF16)
            if k == 0 and s + ahead < len(rows):
                mix_norm(rows[s + ahead])
            if k == 2 and s > 0:
                down_norm(rows[s - 1])
    down_norm(rows[-1])


def _out_ffn(x2d, y0, y1, w_out, mix_layer, g, w_gu, w_down, layer):
    m = x2d.shape[0]
    tile = lambda n: pl.BlockSpec((ROW_TILE, n), lambda i: (i, 0))
    return pl.pallas_call(
        _out_ffn_kernel,
        out_shape=jax.ShapeDtypeStruct((m, D_MODEL), F32),
        grid=(m // ROW_TILE,),
        in_specs=[tile(D_MODEL), tile(HALF), tile(HALF),
                  _resident((None, D_MODEL, D_MODEL), lambda i: (mix_layer, 0, 0)),
                  pl.BlockSpec((None, 4, D_MODEL), lambda i: (layer, 0, 0)),
                  _resident((None, D_MODEL, 2 * D_FF), lambda i: (layer, 0, 0)),
                  _resident((None, D_FF, D_MODEL), lambda i: (layer, 0, 0))],
        out_specs=tile(D_MODEL),
        scratch_shapes=[pltpu.VMEM((ROW_TILE, D_MODEL), BF16),
                        pltpu.VMEM((ROW_TILE, D_FF), BF16)],
        compiler_params=_params(1, FFN_VMEM_LIMIT),
        name="out_proj_ffn",
    )(x2d, y0, y1, w_out, g, w_gu, w_down)


def kernel(x, norm_g, ffn_w_gu, ffn_w_down, ab_w_in, ab_w_out, a_ln_g, a_w_s, a_b_s, b_conv_w, b_filt_w1, b_filt_b1, b_filt_freq, b_filt_w2, b_filt_b2, b_filt_w3, b_decay, b_skip, cd_w_in, cd_w_out, c_w, c_scale, d_conv_w):
    bsz, seq, d = x.shape
    assert (seq, d) == (SEQ, D_MODEL) and norm_g.shape[0] == DEPTH
    m = bsz * seq
    bands = jnp.linspace(1e-4, FILTER_BANDS - 1, FILTER_BANDS, dtype=F32)[None, :]
    csv, csh = _dft_tables()
    x2d = x.reshape(m, d)
    ab_w_in, ab_w_out, cd_w_in, cd_w_out, ffn_w_gu, ffn_w_down = (
        w.astype(BF16) for w in (ab_w_in, ab_w_out, cd_w_in, cd_w_out, ffn_w_gu, ffn_w_down))
    for i in range(DEPTH):
        j = i // 2
        g = norm_g[i]
        if i % 2 == 0:
            proj, y0 = _in_proj_gmlp(x2d, g[0:1], ab_w_in, j, a_ln_g[j][None, :],
                                     a_w_s[j].astype(BF16), a_b_s[j][:, :, None])
            proj = proj.reshape(bsz, seq, -1)
            hidden = _filter_hidden(bands, b_filt_w1[j], b_filt_b1[j][None, :], b_filt_freq[j],
                                    b_filt_w2[j], b_filt_b2[j][None, :])
            kre, kim, hny = _filter_spectra(hidden, b_filt_w3[j], b_decay[j][None, :], csv)
            y1 = _hyena(proj, b_conv_w[j], b_skip[j], kre, kim, hny, csv, csh)
            w_out = ab_w_out
        else:
            proj = _in_proj(x2d, g[0:1], cd_w_in, j).reshape(bsz, seq, -1)
            y0, y1 = _pool_conv(proj, c_w[j].astype(BF16), c_scale[j][None, :], d_conv_w[j])
            w_out = cd_w_out
        x2d = _out_ffn(x2d, y0.reshape(m, HALF), y1.reshape(m, HALF), w_out, j, norm_g,
                       ffn_w_gu, ffn_w_down, i)
    return x2d.reshape(bsz, seq, d)
```
